```python
import math
import jax, jax.numpy as jnp
from jax import lax
import numpy as np

D_MODEL = 2048
BATCH = 4
SEQ = 4096
DEPTH = 1

A_WIDTH = D_MODEL // 2
A_HEAD_DIM = 128
A_HEADS = A_WIDTH // A_HEAD_DIM
CHUNK = 128
B_WIDTH = D_MODEL - A_WIDTH
B_HEADS = 8
B_V_DIM = B_WIDTH // B_HEADS
B_QK_DIM = B_V_DIM // 2
B_QK_WIDTH = B_HEADS * 2 * B_QK_DIM
ROT_DIM = B_QK_DIM // 4
ROPE_THETA = 500000.0
Q_BLOCK = 128
IN_WIDTH = 2 * A_WIDTH + 2 * B_QK_WIDTH + B_WIDTH
MIX_WIDTH = A_WIDTH + B_WIDTH
N_GROUPS = 4
EXPERTS_PER_GROUP = 8
N_EXPERTS = N_GROUPS * EXPERTS_PER_GROUP
TOP_K = 2
D_EXPERT = D_MODEL // 2
EXPERT_BLOCK = 256
EPS = 1e-6

kernel_name = "hybrid_gmlp_diffattn_hiermoe_encoder"


def rms_norm(x, g, eps=EPS):
    xf = x.astype(jnp.float32)
    y = xf * lax.rsqrt(jnp.mean(xf * xf, axis=-1, keepdims=True) + eps)
    return (y * g.astype(jnp.float32)).astype(x.dtype)


def layer_norm(x, g, b, eps=EPS):
    xf = x.astype(jnp.float32)
    mu = jnp.mean(xf, axis=-1, keepdims=True)
    xc = xf - mu
    y = xc * lax.rsqrt(jnp.mean(xc * xc, axis=-1, keepdims=True) + eps)
    return (y * g.astype(jnp.float32) + b.astype(jnp.float32)).astype(x.dtype)


def rotary_tables(seq_len):
    pos = jnp.arange(seq_len, dtype=jnp.float32)
    inv_freq = 1.0 / (jnp.float32(ROPE_THETA) ** (jnp.arange(0, ROT_DIM, 2, dtype=jnp.float32) / ROT_DIM))
    ang = pos[:, None] * inv_freq[None, :]
    return jnp.cos(ang), jnp.sin(ang)


def partial_rotary(x, cos, sin):
    half = ROT_DIM // 2
    c = cos[None, :, None, None, :].astype(x.dtype)
    s = sin[None, :, None, None, :].astype(x.dtype)
    x1 = x[..., :half]
    x2 = x[..., half:ROT_DIM]
    return jnp.concatenate([x1 * c - x2 * s, x2 * c + x1 * s, x[..., ROT_DIM:]], axis=-1)


def gmlp_spatial_gating(zu, zv, ln_g, ln_b, w_s, b_s):
    bsz, seq, _ = zu.shape
    u = jax.nn.gelu(zu).reshape(bsz, seq, A_HEADS, A_HEAD_DIM)
    v = layer_norm(jax.nn.gelu(zv).reshape(bsz, seq, A_HEADS, A_HEAD_DIM), ln_g, ln_b)
    vc = v.reshape(bsz, seq // CHUNK, CHUNK, A_HEADS, A_HEAD_DIM)
    s = jnp.einsum('hij,bcjhd->bcihd', w_s, vc) + jnp.transpose(b_s)[:, :, None]
    return (u * s.reshape(bsz, seq, A_HEADS, A_HEAD_DIM)).reshape(bsz, seq, A_WIDTH)


def diff_attention_core(q, k, v, lam):
    bsz, seq, nh, _, dk = q.shape
    nqb = seq // Q_BLOCK
    qb = jnp.moveaxis(q.reshape(bsz, nqb, Q_BLOCK, nh, 2, dk), 1, 0)
    scale = 1.0 / math.sqrt(dk)

    def one_block(q_blk):
        s = jnp.einsum('bqhcd,bkhcd->bhcqk', q_blk, k).astype(jnp.float32) * scale
        p = jax.nn.softmax(s, axis=-1)
        a = p[:, :, 0] - lam * p[:, :, 1]
        return jnp.einsum('bhqk,bkhe->bqhe', a.astype(v.dtype), v)

    o = lax.map(one_block, qb)
    return jnp.moveaxis(o, 0, 1).reshape(bsz, seq, nh, -1)


def hierarchical_moe(h, w_group, b_group, w_router, b_router, w_gate, w_up, w_down):
    bsz, seq, d = h.shape
    T = bsz * seq
    hf = h.reshape(T, d)
    g_prob = jax.nn.softmax((hf @ w_group).astype(jnp.float32) + b_group.astype(jnp.float32), axis=-1)
    g_w, g_idx = lax.top_k(g_prob, 1)
    e_logits = ((hf @ w_router).astype(jnp.float32) + b_router.astype(jnp.float32)).reshape(T, N_GROUPS, EXPERTS_PER_GROUP)
    sel = jnp.take_along_axis(e_logits, g_idx[:, :, None], axis=1)[:, 0]
    e_w, e_loc = lax.top_k(jax.nn.softmax(sel, axis=-1), TOP_K)
    e_w = e_w / jnp.sum(e_w, axis=-1, keepdims=True)
    combine = g_w * e_w
    expert_id = g_idx * EXPERTS_PER_GROUP + e_loc

    A = T * TOP_K
    flat_e = expert_id.reshape(A)
    flat_t = jnp.repeat(jnp.arange(T, dtype=jnp.int32), TOP_K)
    flat_w = combine.reshape(A)
    order = jnp.argsort(flat_e)
    e_s = flat_e[order]
    counts = jnp.bincount(flat_e, length=N_EXPERTS)
    starts = jnp.cumsum(counts) - counts
    padded = (counts + EXPERT_BLOCK - 1) // EXPERT_BLOCK * EXPERT_BLOCK
    pend = jnp.cumsum(padded)
    pstart = pend - padded
    dest = pstart[e_s] + (jnp.arange(A) - starts[e_s])
    n_blocks = -(-A // EXPERT_BLOCK) + N_EXPERTS
    R = n_blocks * EXPERT_BLOCK
    row_tok = jnp.zeros((R,), jnp.int32).at[dest].set(flat_t[order])
    row_w = jnp.zeros((R,), jnp.float32).at[dest].set(flat_w[order])
    block_e = jnp.minimum(jnp.searchsorted(pend, jnp.arange(n_blocks) * EXPERT_BLOCK, side='right'), N_EXPERTS - 1)

    def expert_block(args):
        tok, e = args
        xb = hf[tok]
        return (jax.nn.silu(xb @ w_gate[e]) * (xb @ w_up[e])) @ w_down[e]

    y = lax.map(expert_block, (row_tok.reshape(n_blocks, EXPERT_BLOCK), block_e))
    y = y.reshape(R, d) * row_w[:, None].astype(y.dtype)
    out = jnp.zeros((T, d), h.dtype).at[row_tok].add(y)
    return out.reshape(bsz, seq, d)


def setup_inputs(seed: int = 0) -> dict:
    key = jax.random.key(seed)
    ks = jax.random.split(key, 24)
    f32 = jnp.float32
    L = DEPTH
    nrm = lambda k, shape, scale: jax.random.normal(k, shape, f32) * scale
    return {
        "x": jax.random.normal(ks[0], (BATCH, SEQ, D_MODEL), f32),
        "attn_norm_g": 1.0 + nrm(ks[1], (L, D_MODEL), 0.02),
        "w_in": nrm(ks[2], (L, D_MODEL, IN_WIDTH), D_MODEL ** -0.5),
        "gmlp_ln_g": 1.0 + nrm(ks[3], (L, A_HEADS, A_HEAD_DIM), 0.02),
        "gmlp_ln_b": nrm(ks[4], (L, A_HEADS, A_HEAD_DIM), 0.02),
        "gmlp_ws": nrm(ks[5], (L, A_HEADS, CHUNK, CHUNK), CHUNK ** -0.5),
        "gmlp_bs": 1.0 + nrm(ks[6], (L, A_HEADS, CHUNK), 0.02),
        "q_norm_g": 1.0 + nrm(ks[7], (L, B_QK_DIM), 0.02),
        "k_norm_g": 1.0 + nrm(ks[8], (L, B_QK_DIM), 0.02),
        "lambda_q1": nrm(ks[9], (L, B_QK_DIM), 0.1),
        "lambda_k1": nrm(ks[10], (L, B_QK_DIM), 0.1),
        "lambda_q2": nrm(ks[11], (L, B_QK_DIM), 0.1),
        "lambda_k2": nrm(ks[12], (L, B_QK_DIM), 0.1),
        "subln_g": 1.0 + nrm(ks[13], (L, B_V_DIM), 0.02),
        "w_out": nrm(ks[14], (L, MIX_WIDTH, D_MODEL), MIX_WIDTH ** -0.5),
        "ffn_norm_g": 1.0 + nrm(ks[15], (L, D_MODEL), 0.02),
        "w_group": nrm(ks[16], (L, D_MODEL, N_GROUPS), D_MODEL ** -0.5),
        "b_group": nrm(ks[17], (L, N_GROUPS), 0.01),
        "w_router": nrm(ks[18], (L, D_MODEL, N_EXPERTS), D_MODEL ** -0.5),
        "b_router": nrm(ks[19], (L, N_EXPERTS), 0.01),
        "w_gate": nrm(ks[20], (L, N_EXPERTS, D_MODEL, D_EXPERT), D_MODEL ** -0.5),
        "w_up": nrm(ks[21], (L, N_EXPERTS, D_MODEL, D_EXPERT), D_MODEL ** -0.5),
        "w_down": nrm(ks[22], (L, N_EXPERTS, D_EXPERT, D_MODEL), D_EXPERT ** -0.5),
    }


def reference(x, attn_norm_g, w_in, gmlp_ln_g, gmlp_ln_b, gmlp_ws, gmlp_bs, q_norm_g, k_norm_g,
              lambda_q1, lambda_k1, lambda_q2, lambda_k2, subln_g, w_out, ffn_norm_g,
              w_group, b_group, w_router, b_router, w_gate, w_up, w_down):
    bsz, seq, _ = x.shape
    cos, sin = rotary_tables(seq)
    splits = [A_WIDTH, 2 * A_WIDTH, 2 * A_WIDTH + B_QK_WIDTH, 2 * A_WIDTH + 2 * B_QK_WIDTH]
    for l in range(DEPTH):
        lambda_init = 0.8 - 0.6 * math.exp(-0.3 * l)
        h = rms_norm(x, attn_norm_g[l])
        z = h @ w_in[l]
        zu, zv, zq, zk, zvb = jnp.split(z, splits, axis=-1)
        out_a = gmlp_spatial_gating(zu, zv, gmlp_ln_g[l], gmlp_ln_b[l], gmlp_ws[l], gmlp_bs[l])
        q = rms_norm(zq.reshape(bsz, seq, B_HEADS, 2, B_QK_DIM), q_norm_g[l])
        k = rms_norm(zk.reshape(bsz, seq, B_HEADS, 2, B_QK_DIM), k_norm_g[l])
        q = partial_rotary(q, cos, sin)
        k = partial_rotary(k, cos, sin)
        v = zvb.reshape(bsz, seq, B_HEADS, B_V_DIM)
        lam = (jnp.exp(jnp.sum(lambda_q1[l].astype(jnp.float32) * lambda_k1[l].astype(jnp.float32)))
               - jnp.exp(jnp.sum(lambda_q2[l].astype(jnp.float32) * lambda_k2[l].astype(jnp.float32)))
               + lambda_init)
        o = diff_attention_core(q, k, v, lam)
        out_b = (rms_norm(o, subln_g[l]) * (1.0 - lambda_init)).reshape(bsz, seq, B_WIDTH)
        x = x + jnp.concatenate([out_a, out_b], axis=-1) @ w_out[l]
        hm = rms_norm(x, ffn_norm_g[l])
        x = x + hierarchical_moe(hm, w_group[l], b_group[l], w_router[l], b_router[l],
                                 w_gate[l], w_up[l], w_down[l])
    return x
```

```python
import functools
import math

import jax
import jax.numpy as jnp
from jax import lax
from jax.experimental import pallas as pl
from jax.experimental.pallas import tpu as pltpu

F32 = jnp.float32
BF16 = jnp.bfloat16

D_MODEL = 2048
A_WIDTH = 1024
A_HEADS = 8
HEAD = 128
CHUNK = 128
B_HEADS = 8
QK_DIM = 64
ROT_DIM = 16
ROPE_THETA = 500000.0
IN_WIDTH = 5120
N_GROUPS = 4
EPG = 8
N_EXPERTS = 32
TOP_K = 2
D_EXPERT = 1024
EPS = 1e-6
LAMBDA_INIT = 0.8 - 0.6 * math.exp(-0.3 * 0)

LANES = 128
ROW_SUB = D_MODEL // LANES

IN_TM = 512
IN_TN = 1024
GM_TM = 512
ATT_TQ = 512
ATT_TK = 512
OUT_TM = 256
EXP_BLK = 256
DSP_TC = 512
CMB_TC = 256
VMEM_LIMIT = 48 * 1024 * 1024


def _cparams(sem):
    return pltpu.CompilerParams(dimension_semantics=sem, vmem_limit_bytes=VMEM_LIMIT)


def _in_proj_kernel(x_ref, g_ref, w_ref, p_ref, bd_ref, c_ref, s1_ref, s2_ref, o_ref, hn_ref):
    j = pl.program_id(1)

    @pl.when(j == 0)
    def _():
        x = x_ref[...]
        ms = jnp.mean(x * x, axis=-1, keepdims=True)
        hn_ref[...] = (x * lax.rsqrt(ms + EPS) * g_ref[...]).astype(BF16)

    z = jnp.dot(hn_ref[...], w_ref[...], preferred_element_type=F32)
    nslab = IN_TN // LANES

    @pl.when(j == 0)
    def _():
        o_ref[...] = jax.nn.gelu(z).astype(BF16)

    @pl.when(j == 1)
    def _():
        for s in range(nslab):
            sl = slice(s * LANES, (s + 1) * LANES)
            gz = jax.nn.gelu(z[:, sl])
            mu = jnp.mean(gz, axis=-1, keepdims=True)
            xc = gz - mu
            var = jnp.mean(xc * xc, axis=-1, keepdims=True)
            y = xc * lax.rsqrt(var + EPS)
            o_ref[:, sl] = (y * p_ref[0:1, sl] + p_ref[1:2, sl]).astype(BF16)

    def qk_section(scale):
        bd = bd_ref[...]
        cos = c_ref[...]
        s_up = s1_ref[...]
        s_dn = s2_ref[...]
        for s in range(nslab):
            sl = slice(s * LANES, (s + 1) * LANES)
            zs = z[:, sl]
            sq = zs * zs
            hi = sq.astype(BF16)
            lo = (sq - hi.astype(F32)).astype(BF16)
            ms = (jnp.dot(hi, bd, preferred_element_type=F32)
                  + jnp.dot(lo, bd, preferred_element_type=F32))
            y = zs * lax.rsqrt(ms + EPS) * p_ref[0:1, sl]
            r = y * cos + pltpu.roll(y, LANES - ROT_DIM // 2, 1) * s_up + pltpu.roll(y, ROT_DIM // 2, 1) * s_dn
            if scale != 1.0:
                r = r * scale
            o_ref[:, sl] = r.astype(BF16)

    @pl.when(j == 2)
    def _():
        qk_section(1.0 / math.sqrt(QK_DIM))

    @pl.when(j == 3)
    def _():
        qk_section(1.0)

    @pl.when(j == 4)
    def _():
        o_ref[...] = z.astype(BF16)


def _in_proj(x2, g, w_bf, params, bd, cos_t, sup_t, sdn_t, seq):
    T = x2.shape[0]
    spt = seq // IN_TM
    return pl.pallas_call(
        _in_proj_kernel,
        grid=(T // IN_TM, IN_WIDTH // IN_TN),
        in_specs=[
            pl.BlockSpec((IN_TM, D_MODEL), lambda i, j: (i, 0)),
            pl.BlockSpec((1, D_MODEL), lambda i, j: (0, 0)),
            pl.BlockSpec((D_MODEL, IN_TN), lambda i, j: (0, j)),
            pl.BlockSpec((8, IN_TN), lambda i, j: (0, j)),
            pl.BlockSpec((LANES, LANES), lambda i, j: (0, 0)),
            pl.BlockSpec((IN_TM, LANES), lambda i, j: (i % spt, 0)),
            pl.BlockSpec((IN_TM, LANES), lambda i, j: (i % spt, 0)),
            pl.BlockSpec((IN_TM, LANES), lambda i, j: (i % spt, 0)),
        ],
        out_specs=pl.BlockSpec((IN_TM, IN_TN), lambda i, j: (i, j)),
        out_shape=jax.ShapeDtypeStruct((T, IN_WIDTH), BF16),
        scratch_shapes=[pltpu.VMEM((IN_TM, D_MODEL), BF16)],
        compiler_params=_cparams(("parallel", "arbitrary")),
        name="in_proj",
    )(x2, g, w_bf, params, bd, cos_t, sup_t, sdn_t)


def _gmlp_kernel(u_ref, v_ref, ws_ref, bs_ref, o_ref):
    for c in range(GM_TM // CHUNK):
        rows = slice(c * CHUNK, (c + 1) * CHUNK)
        for h in range(A_HEADS):
            cols = slice(h * HEAD, (h + 1) * HEAD)
            s = jnp.dot(ws_ref[h], v_ref[rows, cols], preferred_element_type=F32) + bs_ref[:, cols]
            o_ref[rows, cols] = (u_ref[rows, cols].astype(F32) * s).astype(BF16)


def _gmlp(zp, ws_bf, bs_full):
    T = zp.shape[0]
    return pl.pallas_call(
        _gmlp_kernel,
        grid=(T // GM_TM,),
        in_specs=[
            pl.BlockSpec((GM_TM, A_WIDTH), lambda i: (i, 0)),
            pl.BlockSpec((GM_TM, A_WIDTH), lambda i: (i, 1)),
            pl.BlockSpec((A_HEADS, CHUNK, CHUNK), lambda i: (0, 0, 0)),
            pl.BlockSpec((CHUNK, A_WIDTH), lambda i: (0, 0)),
        ],
        out_specs=pl.BlockSpec((GM_TM, A_WIDTH), lambda i: (i, 0)),
        out_shape=jax.ShapeDtypeStruct((T, A_WIDTH), BF16),
        compiler_params=_cparams(("parallel",)),
        name="gmlp",
    )(zp, zp, ws_bf, bs_full)


def _attn_kernel(lam_ref, q_ref, k_ref, v_ref, g_ref, o_ref):
    seq = k_ref.shape[0]
    q = q_ref[...]
    lane = lax.broadcasted_iota(jnp.int32, q.shape, 1)
    zero = jnp.zeros_like(q)
    q1 = jnp.where(lane < QK_DIM, q, zero)
    q2 = jnp.where(lane >= QK_DIM, q, zero)

    def half(kc, vc, qh, m, l, acc):
        s_t = lax.dot_general(kc, qh, (((1,), (1,)), ((), ())), preferred_element_type=F32)
        m_new = jnp.maximum(m, jnp.max(s_t, axis=0, keepdims=True))
        alpha = jnp.exp(m - m_new)
        p = jnp.exp(s_t - m_new)
        l = alpha * l + jnp.sum(p, axis=0, keepdims=True)
        pv = lax.dot_general(vc, p.astype(BF16), (((0,), (0,)), ((), ())), preferred_element_type=F32)
        return m_new, l, alpha * acc + pv

    def body(c, carry):
        m1, l1, a1, m2, l2, a2 = carry
        off = pl.multiple_of(c * ATT_TK, ATT_TK)
        kc = k_ref[pl.ds(off, ATT_TK), :]
        vc = v_ref[pl.ds(off, ATT_TK), :]
        m1, l1, a1 = half(kc, vc, q1, m1, l1, a1)
        m2, l2, a2 = half(kc, vc, q2, m2, l2, a2)
        return m1, l1, a1, m2, l2, a2

    tq = q.shape[0]
    m0 = jnp.full((1, tq), -jnp.inf, F32)
    l0 = jnp.zeros((1, tq), F32)
    a0 = jnp.zeros((HEAD, tq), F32)
    m1, l1, a1, m2, l2, a2 = lax.fori_loop(0, seq // ATT_TK, body, (m0, l0, a0, m0, l0, a0))
    o_t = a1 / l1 - lam_ref[0] * (a2 / l2)
    o = o_t.T
    ms = jnp.mean(o * o, axis=-1, keepdims=True)
    o_ref[...] = (o * lax.rsqrt(ms + EPS) * g_ref[...] * (1.0 - LAMBDA_INIT)).astype(BF16)


def _attention(lam, zp3, subln_g):
    bsz, seq, _ = zp3.shape
    qb, kb, vb = (2 * A_WIDTH) // HEAD, (2 * A_WIDTH + 1024) // HEAD, (2 * A_WIDTH + 2048) // HEAD
    grid_spec = pltpu.PrefetchScalarGridSpec(
        num_scalar_prefetch=1,
        grid=(bsz, B_HEADS, seq // ATT_TQ),
        in_specs=[
            pl.BlockSpec((None, ATT_TQ, HEAD), lambda b, h, i, lam: (b, i, qb + h)),
            pl.BlockSpec((None, seq, HEAD), lambda b, h, i, lam: (b, 0, kb + h)),
            pl.BlockSpec((None, seq, HEAD), lambda b, h, i, lam: (b, 0, vb + h)),
            pl.BlockSpec((1, HEAD), lambda b, h, i, lam: (0, 0)),
        ],
        out_specs=pl.BlockSpec((None, ATT_TQ, HEAD), lambda b, h, i, lam: (b, i, h)),
    )
    return pl.pallas_call(
        _attn_kernel,
        grid_spec=grid_spec,
        out_shape=jax.ShapeDtypeStruct((bsz, seq, B_HEADS * HEAD), BF16),
        compiler_params=_cparams(("parallel", "parallel", "arbitrary")),
        name="attn",
    )(lam, zp3, zp3, zp3, subln_g)


def _out_proj_kernel(a_ref, b_ref, x_ref, wa_ref, wb_ref, g_ref, wr_ref, br_ref, x1_ref, hm_ref, lg_ref):
    acc = (jnp.dot(a_ref[...], wa_ref[...], preferred_element_type=F32)
           + jnp.dot(b_ref[...], wb_ref[...], preferred_element_type=F32))
    x1 = x_ref[...] + acc
    x1_ref[...] = x1
    ms = jnp.mean(x1 * x1, axis=-1, keepdims=True)
    hm = (x1 * lax.rsqrt(ms + EPS) * g_ref[...]).astype(BF16)
    hm_ref[...] = hm
    lg_ref[...] = jnp.dot(hm, wr_ref[...], preferred_element_type=F32) + br_ref[...]


def _out_proj(out_a, out_b, x2, w_out_bf, g, w_rt, b_rt):
    T = x2.shape[0]
    return pl.pallas_call(
        _out_proj_kernel,
        grid=(T // OUT_TM,),
        in_specs=[
            pl.BlockSpec((OUT_TM, A_WIDTH), lambda i: (i, 0)),
            pl.BlockSpec((OUT_TM, A_WIDTH), lambda i: (i, 0)),
            pl.BlockSpec((OUT_TM, D_MODEL), lambda i: (i, 0)),
            pl.BlockSpec((A_WIDTH, D_MODEL), lambda i: (0, 0)),
            pl.BlockSpec((A_WIDTH, D_MODEL), lambda i: (1, 0)),
            pl.BlockSpec((1, D_MODEL), lambda i: (0, 0)),
            pl.BlockSpec((D_MODEL, LANES), lambda i: (0, 0)),
            pl.BlockSpec((1, LANES), lambda i: (0, 0)),
        ],
        out_specs=[
            pl.BlockSpec((OUT_TM, D_MODEL), lambda i: (i, 0)),
            pl.BlockSpec((OUT_TM, D_MODEL), lambda i: (i, 0)),
            pl.BlockSpec((OUT_TM, LANES), lambda i: (i, 0)),
        ],
        out_shape=[
            jax.ShapeDtypeStruct((T, D_MODEL), F32),
            jax.ShapeDtypeStruct((T, D_MODEL), BF16),
            jax.ShapeDtypeStruct((T, LANES), F32),
        ],
        compiler_params=_cparams(("parallel",)),
        name="out_proj",
    )(out_a, out_b, x2, w_out_bf, w_out_bf, g, w_rt, b_rt)


def _dispatch_kernel(zt_ref, pos_ref, hm_ref, xs_ref, zbuf, sem):
    i = pl.program_id(0)

    @pl.when(i == 0)
    def _():
        zbuf[...] = jnp.zeros_like(zbuf)
        for e in range(2 * N_EXPERTS):
            @pl.when(zt_ref[e] >= 0)
            def _():
                pltpu.make_async_copy(zbuf, xs_ref.at[pl.ds(zt_ref[e], EXP_BLK)], sem).start()
        for e in range(2 * N_EXPERTS):
            @pl.when(zt_ref[e] >= 0)
            def _():
                pltpu.make_async_copy(zbuf, xs_ref.at[pl.ds(zt_ref[e], EXP_BLK)], sem).wait()

    def issue(t, carry):
        tok = i * DSP_TC + t
        for k in range(TOP_K):
            pltpu.make_async_copy(hm_ref.at[tok], xs_ref.at[pos_ref[0, TOP_K * t + k]], sem).start()
        return carry

    lax.fori_loop(0, DSP_TC, issue, 0)

    def drain(t, carry):
        for k in range(TOP_K):
            pltpu.make_async_copy(hm_ref.at[0], xs_ref.at[0], sem).wait()
        return carry

    lax.fori_loop(0, DSP_TC, drain, 0)


def _dispatch(zt, pos3, hm3, n_rows):
    T = hm3.shape[0]
    grid_spec = pltpu.PrefetchScalarGridSpec(
        num_scalar_prefetch=1,
        grid=(T // DSP_TC,),
        in_specs=[
            pl.BlockSpec((None, 1, TOP_K * DSP_TC), lambda i, zt: (i, 0, 0), memory_space=pltpu.SMEM),
            pl.BlockSpec(memory_space=pl.ANY),
        ],
        out_specs=pl.BlockSpec(memory_space=pl.ANY),
        scratch_shapes=[pltpu.VMEM((EXP_BLK, ROW_SUB, LANES), BF16), pltpu.SemaphoreType.DMA(())],
    )
    return pl.pallas_call(
        _dispatch_kernel,
        grid_spec=grid_spec,
        out_shape=jax.ShapeDtypeStruct((n_rows, ROW_SUB, LANES), BF16),
        compiler_params=_cparams(("arbitrary",)),
        name="dispatch",
    )(zt, pos3, hm3)


def _expert_kernel(be_ref, nu_ref, xs_ref, wg_ref, wu_ref, wd_ref, y_ref):
    b = pl.program_id(0)

    @pl.when(b < nu_ref[0])
    def _():
        x = xs_ref[...]
        g = jnp.dot(x, wg_ref[...], preferred_element_type=F32)
        u = jnp.dot(x, wu_ref[...], preferred_element_type=F32)
        a = (jax.nn.silu(g) * u).astype(BF16)
        y_ref[...] = jnp.dot(a, wd_ref[...], preferred_element_type=F32).astype(BF16)

    @pl.when(b >= nu_ref[0])
    def _():
        y_ref[...] = jnp.zeros_like(y_ref)


def _experts(block_e, n_used, xs2, wg, wu, wd):
    n_rows = xs2.shape[0]
    n_blocks = n_rows // EXP_BLK

    def xs_map(b, be, nu):
        return (jnp.minimum(b, nu[0] - 1), 0)

    def w_map(b, be, nu):
        return (be[b], 0, 0)

    grid_spec = pltpu.PrefetchScalarGridSpec(
        num_scalar_prefetch=2,
        grid=(n_blocks,),
        in_specs=[
            pl.BlockSpec((EXP_BLK, D_MODEL), xs_map),
            pl.BlockSpec((None, D_MODEL, D_EXPERT), w_map),
            pl.BlockSpec((None, D_MODEL, D_EXPERT), w_map),
            pl.BlockSpec((None, D_EXPERT, D_MODEL), w_map),
        ],
        out_specs=pl.BlockSpec((EXP_BLK, D_MODEL), lambda b, be, nu: (b, 0)),
    )
    return pl.pallas_call(
        _expert_kernel,
        grid_spec=grid_spec,
        out_shape=jax.ShapeDtypeStruct((n_rows, D_MODEL), BF16),
        compiler_params=_cparams(("arbitrary",)),
        name="experts",
    )(block_e, n_used, xs2, wg, wu, wd)


def _combine_kernel(pos_ref, x1_ref, w_ref, y_ref, o_ref, yb0, yb1, sem):
    def issue(t, carry):
        pltpu.make_async_copy(y_ref.at[pos_ref[0, TOP_K * t]], yb0.at[t], sem).start()
        pltpu.make_async_copy(y_ref.at[pos_ref[0, TOP_K * t + 1]], yb1.at[t], sem).start()
        return carry

    lax.fori_loop(0, CMB_TC, issue, 0)

    def drain(t, carry):
        pltpu.make_async_copy(y_ref.at[0], yb0.at[0], sem).wait()
        pltpu.make_async_copy(y_ref.at[0], yb1.at[0], sem).wait()
        return carry

    lax.fori_loop(0, CMB_TC, drain, 0)
    w0 = w_ref[:, 0:1, :]
    w1 = w_ref[:, 1:2, :]
    o_ref[...] = x1_ref[...] + w0 * yb0[...].astype(F32) + w1 * yb1[...].astype(F32)


def _combine(pos3, x13, wb, y3):
    T = x13.shape[0]
    return pl.pallas_call(
        _combine_kernel,
        grid=(T // CMB_TC,),
        in_specs=[
            pl.BlockSpec((None, 1, TOP_K * CMB_TC), lambda i: (i, 0, 0), memory_space=pltpu.SMEM),
            pl.BlockSpec((CMB_TC, ROW_SUB, LANES), lambda i: (i, 0, 0)),
            pl.BlockSpec((CMB_TC, TOP_K, LANES), lambda i: (i, 0, 0)),
            pl.BlockSpec(memory_space=pl.ANY),
        ],
        out_specs=pl.BlockSpec((CMB_TC, ROW_SUB, LANES), lambda i: (i, 0, 0)),
        out_shape=jax.ShapeDtypeStruct((T, ROW_SUB, LANES), F32),
        scratch_shapes=[
            pltpu.VMEM((CMB_TC, ROW_SUB, LANES), BF16),
            pltpu.VMEM((CMB_TC, ROW_SUB, LANES), BF16),
            pltpu.SemaphoreType.DMA(()),
        ],
        compiler_params=_cparams(("arbitrary",)),
        name="combine",
    )(pos3, x13, wb, y3)


def _rotary_lane_tables(seq):
    pos = jnp.arange(seq, dtype=F32)
    inv_freq = 1.0 / (jnp.float32(ROPE_THETA) ** (jnp.arange(0, ROT_DIM, 2, dtype=F32) / ROT_DIM))
    ang = pos[:, None] * inv_freq[None, :]
    cos, sin = jnp.cos(ang), jnp.sin(ang)
    half = ROT_DIM // 2
    lane = jnp.arange(LANES) % QK_DIM
    fidx = lane % half
    first = (lane < half)[None, :]
    second = ((lane >= half) & (lane < ROT_DIM))[None, :]
    cos_t = jnp.where(first | second, cos[:, fidx], 1.0)
    s_up = jnp.where(first, -sin[:, fidx], 0.0)
    s_dn = jnp.where(second, sin[:, fidx], 0.0)
    return cos_t, s_up, s_dn


def _route(logits):
    T = logits.shape[0]
    g_prob = jax.nn.softmax(logits[:, :N_GROUPS], axis=-1)
    g_w, g_idx = lax.top_k(g_prob, 1)
    e_logits = logits[:, N_GROUPS:N_GROUPS + N_EXPERTS].reshape(T, N_GROUPS, EPG)
    sel = jnp.take_along_axis(e_logits, g_idx[:, :, None], axis=1)[:, 0]
    e_w, e_loc = lax.top_k(jax.nn.softmax(sel, axis=-1), TOP_K)
    e_w = e_w / jnp.sum(e_w, axis=-1, keepdims=True)
    return g_idx * EPG + e_loc, g_w * e_w


def kernel(x, attn_norm_g, w_in, gmlp_ln_g, gmlp_ln_b, gmlp_ws, gmlp_bs, q_norm_g, k_norm_g,
           lambda_q1, lambda_k1, lambda_q2, lambda_k2, subln_g, w_out, ffn_norm_g,
           w_group, b_group, w_router, b_router, w_gate, w_up, w_down):
    bsz, seq, d = x.shape
    T = bsz * seq
    assert d == D_MODEL and seq % IN_TM == 0 and seq % ATT_TQ == 0 and T % DSP_TC == 0
    l = 0
    x2 = x.reshape(T, d)

    params = jnp.zeros((8, IN_WIDTH), F32)
    params = params.at[0, A_WIDTH:2 * A_WIDTH].set(gmlp_ln_g[l].reshape(-1))
    params = params.at[1, A_WIDTH:2 * A_WIDTH].set(gmlp_ln_b[l].reshape(-1))
    params = params.at[0, 2 * A_WIDTH:3 * A_WIDTH].set(jnp.tile(q_norm_g[l], 2 * B_HEADS))
    params = params.at[0, 3 * A_WIDTH:4 * A_WIDTH].set(jnp.tile(k_norm_g[l], 2 * B_HEADS))
    blk = jnp.arange(LANES) // QK_DIM
    bd = jnp.where(blk[:, None] == blk[None, :], 1.0 / QK_DIM, 0.0).astype(BF16)
    cos_t, s_up, s_dn = _rotary_lane_tables(seq)
    bs_full = jnp.repeat(jnp.transpose(gmlp_bs[l]), HEAD, axis=1)
    lam = (jnp.exp(jnp.sum(lambda_q1[l] * lambda_k1[l])) - jnp.exp(jnp.sum(lambda_q2[l] * lambda_k2[l]))
           + LAMBDA_INIT).reshape(1).astype(F32)
    w_rt = jnp.zeros((D_MODEL, LANES), F32)
    w_rt = w_rt.at[:, :N_GROUPS].set(w_group[l]).at[:, N_GROUPS:N_GROUPS + N_EXPERTS].set(w_router[l])
    b_rt = jnp.zeros((1, LANES), F32)
    b_rt = b_rt.at[0, :N_GROUPS].set(b_group[l]).at[0, N_GROUPS:N_GROUPS + N_EXPERTS].set(b_router[l])

    zp = _in_proj(x2, attn_norm_g[l].reshape(1, d), w_in[l].astype(BF16), params, bd,
                  cos_t, s_up, s_dn, seq)
    out_a = _gmlp(zp, gmlp_ws[l].astype(BF16), bs_full)
    out_b = _attention(lam, zp.reshape(bsz, seq, IN_WIDTH), subln_g[l].reshape(1, HEAD))
    x1, hm, logits = _out_proj(out_a, out_b.reshape(T, A_WIDTH), x2, w_out[l].astype(BF16),
                               ffn_norm_g[l].reshape(1, d), w_rt.astype(BF16), b_rt)

    expert_id, combine_w = _route(logits)
    flat_e = expert_id.reshape(T * TOP_K)
    onehot = (flat_e[:, None] == jnp.arange(N_EXPERTS)[None, :]).astype(jnp.int32)
    csum = jnp.cumsum(onehot, axis=0)
    rank = jnp.sum(csum * onehot, axis=1) - 1
    counts = csum[-1]
    padded = (counts + EXP_BLK - 1) // EXP_BLK * EXP_BLK
    pend = jnp.cumsum(padded)
    pstart = pend - padded
    pos = (pstart[flat_e] + rank).astype(jnp.int32)
    n_blocks = (T * TOP_K) // EXP_BLK + N_EXPERTS
    n_rows = n_blocks * EXP_BLK
    block_e = jnp.minimum(jnp.searchsorted(pend, jnp.arange(n_blocks) * EXP_BLK, side='right'),
                          N_EXPERTS - 1).astype(jnp.int32)
    n_used = (pend[-1] // EXP_BLK).reshape(1).astype(jnp.int32)
    trail = pend[-1] + jnp.arange(N_EXPERTS) * EXP_BLK
    zt = jnp.concatenate([jnp.where(counts > 0, pend - EXP_BLK, -1),
                          jnp.where(trail < n_rows, trail, -1)]).astype(jnp.int32)

    xs = _dispatch(zt, pos.reshape(T // DSP_TC, 1, TOP_K * DSP_TC), hm.reshape(T, ROW_SUB, LANES), n_rows)
    y = _experts(block_e, n_used, xs.reshape(n_rows, D_MODEL),
                 w_gate[l].astype(BF16), w_up[l].astype(BF16), w_down[l].astype(BF16))
    wb = jnp.broadcast_to(combine_w.astype(F32)[:, :, None], (T, TOP_K, LANES))
    out = _combine(pos.reshape(T // CMB_TC, 1, TOP_K * CMB_TC), x1.reshape(T, ROW_SUB, LANES), wb,
                   y.reshape(n_rows, ROW_SUB, LANES))
    return out.reshape(bsz, seq, d)
```

```python
import math

import jax
import jax.numpy as jnp
from jax import lax
from jax.experimental import pallas as pl
from jax.experimental.pallas import tpu as pltpu

F32 = jnp.float32
BF16 = jnp.bfloat16
I32 = jnp.int32

D_MODEL = 2048
A_WIDTH = 1024
A_HEADS = 8
HEAD = 128
CHUNK = 128
B_HEADS = 8
QK_DIM = 64
ROT_DIM = 16
ROPE_THETA = 500000.0
IN_WIDTH = 5120
N_GROUPS = 4
EPG = 8
N_EXPERTS = 32
TOP_K = 2
D_EXPERT = 1024
EPS = 1e-6
LAMBDA_INIT = 0.8 - 0.6 * math.exp(-0.3 * 0)

LANES = 128
SUBLANES = 8
ROW_SUB = D_MODEL // LANES
LG_ROWS = SUBLANES * (1 + N_GROUPS)

IN_TM = 512
IN_TN = 1024
GM_TM = 512
ATT_TQ = 512
ATT_TK = 512
OUT_TM = 256
RT_TN = 512
EXP_BLK = 256
DSP_TC = 256
CMB_TC = 256
VMEM_LIMIT = 48 * 1024 * 1024


def _cparams(sem):
    return pltpu.CompilerParams(dimension_semantics=sem, vmem_limit_bytes=VMEM_LIMIT)


def _in_proj_kernel(x_ref, g_ref, w_ref, p_ref, bd_ref, c_ref, s1_ref, s2_ref, o_ref, hn_ref):
    j = pl.program_id(1)

    @pl.when(j == 0)
    def _():
        x = x_ref[...]
        ms = jnp.mean(x * x, axis=-1, keepdims=True)
        hn_ref[...] = (x * lax.rsqrt(ms + EPS) * g_ref[...]).astype(BF16)

    z = jnp.dot(hn_ref[...], w_ref[...], preferred_element_type=F32)
    nslab = IN_TN // LANES

    @pl.when(j == 0)
    def _():
        o_ref[...] = jax.nn.gelu(z).astype(BF16)

    @pl.when(j == 1)
    def _():
        for s in range(nslab):
            sl = slice(s * LANES, (s + 1) * LANES)
            gz = jax.nn.gelu(z[:, sl])
            mu = jnp.mean(gz, axis=-1, keepdims=True)
            xc = gz - mu
            var = jnp.mean(xc * xc, axis=-1, keepdims=True)
            y = xc * lax.rsqrt(var + EPS)
            o_ref[:, sl] = (y * p_ref[0:1, sl] + p_ref[1:2, sl]).astype(BF16)

    def qk_section(scale):
        bd = bd_ref[...]
        cos = c_ref[...]
        s_up = s1_ref[...]
        s_dn = s2_ref[...]
        for s in range(nslab):
            sl = slice(s * LANES, (s + 1) * LANES)
            zs = z[:, sl]
            sq = zs * zs
            hi = sq.astype(BF16)
            lo = (sq - hi.astype(F32)).astype(BF16)
            ms = (jnp.dot(hi, bd, preferred_element_type=F32)
                  + jnp.dot(lo, bd, preferred_element_type=F32))
            y = zs * lax.rsqrt(ms + EPS) * p_ref[0:1, sl]
            r = y * cos + pltpu.roll(y, LANES - ROT_DIM // 2, 1) * s_up + pltpu.roll(y, ROT_DIM // 2, 1) * s_dn
            if scale != 1.0:
                r = r * scale
            o_ref[:, sl] = r.astype(BF16)

    @pl.when(j == 2)
    def _():
        qk_section(1.0 / math.sqrt(QK_DIM))

    @pl.when(j == 3)
    def _():
        qk_section(1.0)

    @pl.when(j == 4)
    def _():
        o_ref[...] = z.astype(BF16)


def _in_proj(x2, g, w_bf, params, bd, cos_t, sup_t, sdn_t, seq):
    T = x2.shape[0]
    spt = seq // IN_TM
    return pl.pallas_call(
        _in_proj_kernel,
        grid=(T // IN_TM, IN_WIDTH // IN_TN),
        in_specs=[
            pl.BlockSpec((IN_TM, D_MODEL), lambda i, j: (i, 0)),
            pl.BlockSpec((1, D_MODEL), lambda i, j: (0, 0)),
            pl.BlockSpec((D_MODEL, IN_TN), lambda i, j: (0, j)),
            pl.BlockSpec((8, IN_TN), lambda i, j: (0, j)),
            pl.BlockSpec((LANES, LANES), lambda i, j: (0, 0)),
            pl.BlockSpec((IN_TM, LANES), lambda i, j: (i % spt, 0)),
            pl.BlockSpec((IN_TM, LANES), lambda i, j: (i % spt, 0)),
            pl.BlockSpec((IN_TM, LANES), lambda i, j: (i % spt, 0)),
        ],
        out_specs=pl.BlockSpec((IN_TM, IN_TN), lambda i, j: (i, j)),
        out_shape=jax.ShapeDtypeStruct((T, IN_WIDTH), BF16),
        scratch_shapes=[pltpu.VMEM((IN_TM, D_MODEL), BF16)],
        compiler_params=_cparams(("parallel", "arbitrary")),
        name="in_proj",
    )(x2, g, w_bf, params, bd, cos_t, sup_t, sdn_t)


def _gmlp_kernel(u_ref, v_ref, ws_ref, bs_ref, o_ref):
    for c in range(GM_TM // CHUNK):
        rows = slice(c * CHUNK, (c + 1) * CHUNK)
        for h in range(A_HEADS):
            cols = slice(h * HEAD, (h + 1) * HEAD)
            s = jnp.dot(ws_ref[h], v_ref[rows, cols], preferred_element_type=F32) + bs_ref[:, cols]
            o_ref[rows, cols] = (u_ref[rows, cols].astype(F32) * s).astype(BF16)


def _gmlp(zp, ws_bf, bs_full):
    T = zp.shape[0]
    return pl.pallas_call(
        _gmlp_kernel,
        grid=(T // GM_TM,),
        in_specs=[
            pl.BlockSpec((GM_TM, A_WIDTH), lambda i: (i, 0)),
            pl.BlockSpec((GM_TM, A_WIDTH), lambda i: (i, 1)),
            pl.BlockSpec((A_HEADS, CHUNK, CHUNK), lambda i: (0, 0, 0)),
            pl.BlockSpec((CHUNK, A_WIDTH), lambda i: (0, 0)),
        ],
        out_specs=pl.BlockSpec((GM_TM, A_WIDTH), lambda i: (i, 0)),
        out_shape=jax.ShapeDtypeStruct((T, A_WIDTH), BF16),
        compiler_params=_cparams(("parallel",)),
        name="gmlp",
    )(zp, zp, ws_bf, bs_full)


def _attn_kernel(lam_ref, q_ref, k_ref, v_ref, g_ref, o_ref):
    seq = k_ref.shape[0]
    q = q_ref[...]
    lane = lax.broadcasted_iota(jnp.int32, q.shape, 1)
    zero = jnp.zeros_like(q)
    q1 = jnp.where(lane < QK_DIM, q, zero)
    q2 = jnp.where(lane >= QK_DIM, q, zero)

    def half(kc, vc, qh, m, l, acc):
        s_t = lax.dot_general(kc, qh, (((1,), (1,)), ((), ())), preferred_element_type=F32)
        m_new = jnp.maximum(m, jnp.max(s_t, axis=0, keepdims=True))
        alpha = jnp.exp(m - m_new)
        p = jnp.exp(s_t - m_new)
        l = alpha * l + jnp.sum(p, axis=0, keepdims=True)
        pv = lax.dot_general(vc, p.astype(BF16), (((0,), (0,)), ((), ())), preferred_element_type=F32)
        return m_new, l, alpha * acc + pv

    def body(c, carry):
        m1, l1, a1, m2, l2, a2 = carry
        off = pl.multiple_of(c * ATT_TK, ATT_TK)
        kc = k_ref[pl.ds(off, ATT_TK), :]
        vc = v_ref[pl.ds(off, ATT_TK), :]
        m1, l1, a1 = half(kc, vc, q1, m1, l1, a1)
        m2, l2, a2 = half(kc, vc, q2, m2, l2, a2)
        return m1, l1, a1, m2, l2, a2

    tq = q.shape[0]
    m0 = jnp.full((1, tq), -jnp.inf, F32)
    l0 = jnp.zeros((1, tq), F32)
    a0 = jnp.zeros((HEAD, tq), F32)
    m1, l1, a1, m2, l2, a2 = lax.fori_loop(0, seq // ATT_TK, body, (m0, l0, a0, m0, l0, a0))
    o_t = a1 / l1 - lam_ref[0] * (a2 / l2)
    o = o_t.T
    ms = jnp.mean(o * o, axis=-1, keepdims=True)
    o_ref[...] = (o * lax.rsqrt(ms + EPS) * g_ref[...] * (1.0 - LAMBDA_INIT)).astype(BF16)


def _attention(lam, zp3, subln_g):
    bsz, seq, _ = zp3.shape
    qb, kb, vb = (2 * A_WIDTH) // HEAD, (2 * A_WIDTH + 1024) // HEAD, (2 * A_WIDTH + 2048) // HEAD
    grid_spec = pltpu.PrefetchScalarGridSpec(
        num_scalar_prefetch=1,
        grid=(bsz, B_HEADS, seq // ATT_TQ),
        in_specs=[
            pl.BlockSpec((None, ATT_TQ, HEAD), lambda b, h, i, lam: (b, i, qb + h)),
            pl.BlockSpec((None, seq, HEAD), lambda b, h, i, lam: (b, 0, kb + h)),
            pl.BlockSpec((None, seq, HEAD), lambda b, h, i, lam: (b, 0, vb + h)),
            pl.BlockSpec((1, HEAD), lambda b, h, i, lam: (0, 0)),
        ],
        out_specs=pl.BlockSpec((None, ATT_TQ, HEAD), lambda b, h, i, lam: (b, i, h)),
    )
    return pl.pallas_call(
        _attn_kernel,
        grid_spec=grid_spec,
        out_shape=jax.ShapeDtypeStruct((bsz, seq, B_HEADS * HEAD), BF16),
        compiler_params=_cparams(("parallel", "parallel", "arbitrary")),
        name="attn",
    )(lam, zp3, zp3, zp3, subln_g)


def _out_proj_kernel(a_ref, b_ref, x_ref, wa_ref, wb_ref, g_ref, wr_ref, br_ref, x1_ref, hm_ref, lg_ref):
    acc = (jnp.dot(a_ref[...], wa_ref[...], preferred_element_type=F32)
           + jnp.dot(b_ref[...], wb_ref[...], preferred_element_type=F32))
    x1 = x_ref[...] + acc
    x1_ref[...] = x1
    ms = jnp.mean(x1 * x1, axis=-1, keepdims=True)
    hm = x1 * lax.rsqrt(ms + EPS) * g_ref[...]
    for s in range(ROW_SUB):
        hm_ref[pl.ds(s, OUT_TM, stride=ROW_SUB), :] = hm[:, s * LANES:(s + 1) * LANES]
    lg_ref[...] = lax.dot_general(wr_ref[...], hm.astype(BF16), (((1,), (1,)), ((), ())),
                                  preferred_element_type=F32) + br_ref[...]


def _out_proj(out_a, out_b, x2, w_out_bf, g, w_rt, b_rt):
    T = x2.shape[0]
    return pl.pallas_call(
        _out_proj_kernel,
        grid=(T // OUT_TM,),
        in_specs=[
            pl.BlockSpec((OUT_TM, A_WIDTH), lambda i: (i, 0)),
            pl.BlockSpec((OUT_TM, A_WIDTH), lambda i: (i, 0)),
            pl.BlockSpec((OUT_TM, D_MODEL), lambda i: (i, 0)),
            pl.BlockSpec((A_WIDTH, D_MODEL), lambda i: (0, 0)),
            pl.BlockSpec((A_WIDTH, D_MODEL), lambda i: (1, 0)),
            pl.BlockSpec((1, D_MODEL), lambda i: (0, 0)),
            pl.BlockSpec((LG_ROWS, D_MODEL), lambda i: (0, 0)),
            pl.BlockSpec((LG_ROWS, 1), lambda i: (0, 0)),
        ],
        out_specs=[
            pl.BlockSpec((OUT_TM, D_MODEL), lambda i: (i, 0)),
            pl.BlockSpec((OUT_TM * ROW_SUB, LANES), lambda i: (i, 0)),
            pl.BlockSpec((LG_ROWS, OUT_TM), lambda i: (0, i)),
        ],
        out_shape=[
            jax.ShapeDtypeStruct((T, D_MODEL), F32),
            jax.ShapeDtypeStruct((T * ROW_SUB, LANES), F32),
            jax.ShapeDtypeStruct((LG_ROWS, T), F32),
        ],
        compiler_params=_cparams(("parallel",)),
        name="out_proj",
    )(out_a, out_b, x2, w_out_bf, w_out_bf, g, w_rt, b_rt)


def _route_kernel(lg_ref, tri_ref, idr_ref, cw_ref, cnt_ref, carry_ref):
    i = pl.program_id(0)

    @pl.when(i == 0)
    def _():
        carry_ref[...] = jnp.zeros_like(carry_ref)

    tn = lg_ref.shape[1]
    row = lax.broadcasted_iota(I32, (SUBLANES, tn), 0)
    neg = jnp.float32(-jnp.inf)

    def top1(v):
        mx = jnp.max(v, axis=0, keepdims=True)
        idx = jnp.min(jnp.where(v == mx, row, SUBLANES), axis=0, keepdims=True)
        return mx, idx

    g = jnp.where(row < N_GROUPS, lg_ref[0:SUBLANES, :], neg)
    gmax, gidx = top1(g)
    g_w = 1.0 / jnp.sum(jnp.exp(g - gmax), axis=0, keepdims=True)
    sel = lg_ref[SUBLANES * N_GROUPS:SUBLANES * (N_GROUPS + 1), :]
    for grp in range(N_GROUPS - 2, -1, -1):
        sel = jnp.where(gidx == grp, lg_ref[SUBLANES * (grp + 1):SUBLANES * (grp + 2), :], sel)
    e1, i1 = top1(sel)
    e2, i2 = top1(jnp.where(row == i1, neg, sel))
    d = jnp.exp(e2 - e1)
    w1 = 1.0 / (1.0 + d)
    w2 = d * w1
    id0 = gidx * EPG + i1
    id1 = gidx * EPG + i2

    erow = lax.broadcasted_iota(I32, (N_EXPERTS, tn), 0)
    o0 = erow == id0
    o1 = erow == id1
    occ = jnp.where(o0 | o1, 1.0, 0.0).astype(F32)
    before = jnp.dot(occ.astype(BF16), tri_ref[...], preferred_element_type=F32) + carry_ref[:, 0:1]
    r0 = jnp.sum(jnp.where(o0, before, 0.0), axis=0, keepdims=True)
    r1 = jnp.sum(jnp.where(o1, before, 0.0), axis=0, keepdims=True)
    total = carry_ref[...] + jnp.sum(occ, axis=1, keepdims=True)
    carry_ref[...] = total
    cnt_ref[...] = total.astype(I32)

    zi = jnp.zeros((SUBLANES - 4, tn), I32)
    idr_ref[...] = jnp.concatenate([id0, id1, r0.astype(I32), r1.astype(I32), zi], axis=0)
    zf = jnp.zeros((SUBLANES - 2, tn), F32)
    cw_ref[...] = jnp.concatenate([g_w * w1, g_w * w2, zf], axis=0)


def _route(lg_t, tri):
    T = lg_t.shape[1]
    return pl.pallas_call(
        _route_kernel,
        grid=(T // RT_TN,),
        in_specs=[
            pl.BlockSpec((LG_ROWS, RT_TN), lambda i: (0, i)),
            pl.BlockSpec((RT_TN, RT_TN), lambda i: (0, 0)),
        ],
        out_specs=[
            pl.BlockSpec((SUBLANES, RT_TN), lambda i: (0, i)),
            pl.BlockSpec((SUBLANES, RT_TN), lambda i: (0, i)),
            pl.BlockSpec((N_EXPERTS, LANES), lambda i: (0, 0)),
        ],
        out_shape=[
            jax.ShapeDtypeStruct((SUBLANES, T), I32),
            jax.ShapeDtypeStruct((SUBLANES, T), F32),
            jax.ShapeDtypeStruct((N_EXPERTS, LANES), I32),
        ],
        scratch_shapes=[pltpu.VMEM((N_EXPERTS, LANES), F32)],
        compiler_params=_cparams(("arbitrary",)),
        name="route",
    )(lg_t, tri)


def _row_copy(src_ref, src_row, dst_ref, dst_row, sem):
    return pltpu.make_async_copy(src_ref.at[pl.ds(src_row * ROW_SUB, ROW_SUB)],
                                 dst_ref.at[pl.ds(dst_row * ROW_SUB, ROW_SUB)], sem)


def _dispatch_kernel(zt_ref, ps_ref, idr_ref, hm_ref, xs_ref, zbuf, sem):
    i = pl.program_id(0)

    def zero_copy(e):
        return pltpu.make_async_copy(
            zbuf, xs_ref.at[pl.ds(pl.multiple_of(zt_ref[e] * ROW_SUB, ROW_SUB), EXP_BLK * ROW_SUB)], sem)

    @pl.when(i == 0)
    def _():
        zbuf[...] = jnp.zeros_like(zbuf)
        for e in range(2 * N_EXPERTS):
            @pl.when(zt_ref[e] >= 0)
            def _():
                zero_copy(e).start()
        for e in range(2 * N_EXPERTS):
            @pl.when(zt_ref[e] >= 0)
            def _():
                zero_copy(e).wait()

    def issue(t, carry):
        for k in range(TOP_K):
            dst = ps_ref[idr_ref[k, t]] + idr_ref[TOP_K + k, t]
            _row_copy(hm_ref, t, xs_ref, dst, sem).start()
        return carry

    lax.fori_loop(0, DSP_TC, issue, 0)

    def drain(t, carry):
        for k in range(TOP_K):
            _row_copy(hm_ref, 0, xs_ref, 0, sem).wait()
        return carry

    lax.fori_loop(0, DSP_TC, drain, 0)


def _dispatch(zt, pstart, idr, hm3, n_rows):
    T = idr.shape[1]
    grid_spec = pltpu.PrefetchScalarGridSpec(
        num_scalar_prefetch=2,
        grid=(T // DSP_TC,),
        in_specs=[
            pl.BlockSpec((SUBLANES, DSP_TC), lambda i, zt, ps: (0, i), memory_space=pltpu.SMEM),
            pl.BlockSpec((DSP_TC * ROW_SUB, LANES), lambda i, zt, ps: (i, 0)),
        ],
        out_specs=pl.BlockSpec(memory_space=pl.ANY),
        scratch_shapes=[pltpu.VMEM((EXP_BLK * ROW_SUB, LANES), F32), pltpu.SemaphoreType.DMA(())],
    )
    return pl.pallas_call(
        _dispatch_kernel,
        grid_spec=grid_spec,
        out_shape=jax.ShapeDtypeStruct((n_rows * ROW_SUB, LANES), F32),
        compiler_params=_cparams(("arbitrary",)),
        name="dispatch",
    )(zt, pstart, idr, hm3)


def _expert_kernel(be_ref, nu_ref, xs_ref, wg_ref, wu_ref, wd_ref, y_ref, xb_ref):
    b = pl.program_id(0)

    @pl.when(b < nu_ref[0])
    def _():
        for s in range(ROW_SUB):
            xb_ref[:, s * LANES:(s + 1) * LANES] = xs_ref[pl.ds(s, EXP_BLK, stride=ROW_SUB), :].astype(BF16)
        x = xb_ref[...]
        g = jnp.dot(x, wg_ref[...], preferred_element_type=F32)
        u = jnp.dot(x, wu_ref[...], preferred_element_type=F32)
        a = (jax.nn.silu(g) * u).astype(BF16)
        y = jnp.dot(a, wd_ref[...], preferred_element_type=F32)
        for s in range(ROW_SUB):
            y_ref[pl.ds(s, EXP_BLK, stride=ROW_SUB), :] = y[:, s * LANES:(s + 1) * LANES]

    @pl.when(b >= nu_ref[0])
    def _():
        y_ref[...] = jnp.zeros_like(y_ref)


def _experts(block_e, n_used, xs3, wg, wu, wd):
    n_blocks = xs3.shape[0] // (EXP_BLK * ROW_SUB)

    def xs_map(b, be, nu):
        return (jnp.minimum(b, nu[0] - 1), 0)

    def w_map(b, be, nu):
        return (be[b], 0, 0)

    grid_spec = pltpu.PrefetchScalarGridSpec(
        num_scalar_prefetch=2,
        grid=(n_blocks,),
        in_specs=[
            pl.BlockSpec((EXP_BLK * ROW_SUB, LANES), xs_map),
            pl.BlockSpec((None, D_MODEL, D_EXPERT), w_map),
            pl.BlockSpec((None, D_MODEL, D_EXPERT), w_map),
            pl.BlockSpec((None, D_EXPERT, D_MODEL), w_map),
        ],
        out_specs=pl.BlockSpec((EXP_BLK * ROW_SUB, LANES), lambda b, be, nu: (b, 0)),
        scratch_shapes=[pltpu.VMEM((EXP_BLK, D_MODEL), BF16)],
    )
    return pl.pallas_call(
        _expert_kernel,
        grid_spec=grid_spec,
        out_shape=jax.ShapeDtypeStruct(xs3.shape, F32),
        compiler_params=_cparams(("arbitrary",)),
        name="experts",
    )(block_e, n_used, xs3, wg, wu, wd)


def _combine_kernel(ps_ref, idr_ref, x1_ref, cw_ref, y_ref, o_ref, yb0, yb1, sem):
    def issue(t, carry):
        for k, yb in enumerate((yb0, yb1)):
            src = ps_ref[idr_ref[k, t]] + idr_ref[TOP_K + k, t]
            _row_copy(y_ref, src, yb, t, sem).start()
        return carry

    lax.fori_loop(0, CMB_TC, issue, 0)

    def drain(t, carry):
        for yb in (yb0, yb1):
            _row_copy(y_ref, 0, yb, 0, sem).wait()
        return carry

    lax.fori_loop(0, CMB_TC, drain, 0)
    w_t = cw_ref[...].T
    w0 = w_t[:, 0:1]
    w1 = w_t[:, 1:2]
    for s in range(ROW_SUB):
        sl = slice(s * LANES, (s + 1) * LANES)
        o_ref[:, sl] = (x1_ref[:, sl] + w0 * yb0[pl.ds(s, CMB_TC, stride=ROW_SUB), :]
                        + w1 * yb1[pl.ds(s, CMB_TC, stride=ROW_SUB), :])


def _combine(pstart, idr, x1, cw, y3):
    T = x1.shape[0]
    grid_spec = pltpu.PrefetchScalarGridSpec(
        num_scalar_prefetch=1,
        grid=(T // CMB_TC,),
        in_specs=[
            pl.BlockSpec((SUBLANES, CMB_TC), lambda i, ps: (0, i), memory_space=pltpu.SMEM),
            pl.BlockSpec((CMB_TC, D_MODEL), lambda i, ps: (i, 0)),
            pl.BlockSpec((SUBLANES, CMB_TC), lambda i, ps: (0, i)),
            pl.BlockSpec(memory_space=pl.ANY),
        ],
        out_specs=pl.BlockSpec((CMB_TC, D_MODEL), lambda i, ps: (i, 0)),
        scratch_shapes=[
            pltpu.VMEM((CMB_TC * ROW_SUB, LANES), F32),
            pltpu.VMEM((CMB_TC * ROW_SUB, LANES), F32),
            pltpu.SemaphoreType.DMA(()),
        ],
    )
    return pl.pallas_call(
        _combine_kernel,
        grid_spec=grid_spec,
        out_shape=jax.ShapeDtypeStruct((T, D_MODEL), F32),
        compiler_params=_cparams(("arbitrary",)),
        name="combine",
    )(pstart, idr, x1, cw, y3)


def _rotary_lane_tables(seq):
    pos = jnp.arange(seq, dtype=F32)
    inv_freq = 1.0 / (jnp.float32(ROPE_THETA) ** (jnp.arange(0, ROT_DIM, 2, dtype=F32) / ROT_DIM))
    ang = pos[:, None] * inv_freq[None, :]
    cos, sin = jnp.cos(ang), jnp.sin(ang)
    half = ROT_DIM // 2
    lane = jnp.arange(LANES) % QK_DIM
    fidx = lane % half
    first = (lane < half)[None, :]
    second = ((lane >= half) & (lane < ROT_DIM))[None, :]
    cos_t = jnp.where(first | second, cos[:, fidx], 1.0)
    s_up = jnp.where(first, -sin[:, fidx], 0.0)
    s_dn = jnp.where(second, sin[:, fidx], 0.0)
    return cos_t, s_up, s_dn


def kernel(x, attn_norm_g, w_in, gmlp_ln_g, gmlp_ln_b, gmlp_ws, gmlp_bs, q_norm_g, k_norm_g,
           lambda_q1, lambda_k1, lambda_q2, lambda_k2, subln_g, w_out, ffn_norm_g,
           w_group, b_group, w_router, b_router, w_gate, w_up, w_down):
    bsz, seq, d = x.shape
    T = bsz * seq
    assert d == D_MODEL and seq % IN_TM == 0 and seq % ATT_TQ == 0 and T % RT_TN == 0
    l = 0
    x2 = x.reshape(T, d)

    params = jnp.zeros((8, IN_WIDTH), F32)
    params = params.at[0, A_WIDTH:2 * A_WIDTH].set(gmlp_ln_g[l].reshape(-1))
    params = params.at[1, A_WIDTH:2 * A_WIDTH].set(gmlp_ln_b[l].reshape(-1))
    params = params.at[0, 2 * A_WIDTH:3 * A_WIDTH].set(jnp.tile(q_norm_g[l], 2 * B_HEADS))
    params = params.at[0, 3 * A_WIDTH:4 * A_WIDTH].set(jnp.tile(k_norm_g[l], 2 * B_HEADS))
    blk = jnp.arange(LANES) // QK_DIM
    bd = jnp.where(blk[:, None] == blk[None, :], 1.0 / QK_DIM, 0.0).astype(BF16)
    cos_t, s_up, s_dn = _rotary_lane_tables(seq)
    bs_full = jnp.repeat(jnp.transpose(gmlp_bs[l]), HEAD, axis=1)
    lam = (jnp.exp(jnp.sum(lambda_q1[l] * lambda_k1[l])) - jnp.exp(jnp.sum(lambda_q2[l] * lambda_k2[l]))
           + LAMBDA_INIT).reshape(1).astype(F32)
    w_rt = jnp.zeros((LG_ROWS, D_MODEL), F32)
    w_rt = w_rt.at[:N_GROUPS].set(w_group[l].T).at[SUBLANES:].set(w_router[l].T)
    b_rt = jnp.zeros((LG_ROWS, 1), F32)
    b_rt = b_rt.at[:N_GROUPS, 0].set(b_group[l]).at[SUBLANES:, 0].set(b_router[l])
    tri = (jnp.arange(RT_TN)[:, None] < jnp.arange(RT_TN)[None, :]).astype(BF16)

    zp = _in_proj(x2, attn_norm_g[l].reshape(1, d), w_in[l].astype(BF16), params, bd,
                  cos_t, s_up, s_dn, seq)
    out_a = _gmlp(zp, gmlp_ws[l].astype(BF16), bs_full)
    out_b = _attention(lam, zp.reshape(bsz, seq, IN_WIDTH), subln_g[l].reshape(1, HEAD))
    x1, hm3, lg_t = _out_proj(out_a, out_b.reshape(T, A_WIDTH), x2, w_out[l].astype(BF16),
                              ffn_norm_g[l].reshape(1, d), w_rt.astype(BF16), b_rt)

    idr, cw, cnt = _route(lg_t, tri)
    counts = cnt[:, 0]
    padded = (counts + EXP_BLK - 1) // EXP_BLK * EXP_BLK
    pend = jnp.cumsum(padded)
    pstart = (pend - padded).astype(I32)
    n_blocks = (T * TOP_K) // EXP_BLK + N_EXPERTS
    n_rows = n_blocks * EXP_BLK
    block_e = jnp.minimum(jnp.searchsorted(pend, jnp.arange(n_blocks) * EXP_BLK, side='right'),
                          N_EXPERTS - 1).astype(I32)
    n_used = (pend[-1] // EXP_BLK).reshape(1).astype(I32)
    trail = pend[-1] + jnp.arange(N_EXPERTS) * EXP_BLK
    zt = jnp.concatenate([jnp.where(counts > 0, pend - EXP_BLK, -1),
                          jnp.where(trail < n_rows, trail, -1)]).astype(I32)

    xs3 = _dispatch(zt, pstart, idr, hm3, n_rows)
    y3 = _experts(block_e, n_used, xs3, w_gate[l].astype(BF16), w_up[l].astype(BF16), w_down[l].astype(BF16))
    out = _combine(pstart, idr, x1, cw, y3)
    return out.reshape(bsz, seq, d)
```

```python
import math

import jax
import jax.numpy as jnp
from jax import lax
from jax.experimental import pallas as pl
from jax.experimental.pallas import tpu as pltpu

F32 = jnp.float32
BF16 = jnp.bfloat16
I32 = jnp.int32

D_MODEL = 2048
A_WIDTH = 1024
A_HEADS = 8
HEAD = 128
CHUNK = 128
B_HEADS = 8
QK_DIM = 64
ROT_DIM = 16
ROPE_THETA = 500000.0
IN_WIDTH = 5120
N_GROUPS = 4
EPG = 8
N_EXPERTS = 32
TOP_K = 2
D_EXPERT = 1024
EPS = 1e-6
LAMBDA_INIT = 0.8 - 0.6 * math.exp(-0.3 * 0)

LANES = 128
SUBLANES = 8
ROW_SUB = D_MODEL // LANES
LG_ROWS = SUBLANES * (1 + N_GROUPS)

IN_TM = 512
IN_TN = 1024
IN_GRP = 256
GM_TM = 512
ATT_TQ = 512
ATT_TK = 512
ONES_ROWS = 16
OUT_TM = 256
RT_TN = 512
EXP_BLK = 256
DSP_TC = 256
CMB_TC = 256
VMEM_LIMIT = 48 * 1024 * 1024


def _cparams(sem):
    return pltpu.CompilerParams(dimension_semantics=sem, vmem_limit_bytes=VMEM_LIMIT)


def _in_proj_kernel(x_ref, g_ref, w_ref, p_ref, bd_ref, c_ref, s1_ref, s2_ref, o_ref, hn_ref):
    j = pl.program_id(1)

    @pl.when(j == 0)
    def _():
        x = x_ref[...]
        ms = jnp.mean(x * x, axis=-1, keepdims=True)
        hn_ref[...] = (x * lax.rsqrt(ms + EPS) * g_ref[...]).astype(BF16)

    z = jnp.dot(hn_ref[...], w_ref[...], preferred_element_type=F32)
    tm = z.shape[0]
    ngrp = IN_TN // IN_GRP

    @pl.when(j == 0)
    def _():
        o_ref[...] = jax.nn.gelu(z).astype(BF16)

    @pl.when(j == 1)
    def _():
        for s in range(IN_TN // LANES):
            sl = slice(s * LANES, (s + 1) * LANES)
            gz = jax.nn.gelu(z[:, sl])
            mu = jnp.mean(gz, axis=-1, keepdims=True)
            xc = gz - mu
            var = jnp.mean(xc * xc, axis=-1, keepdims=True)
            o_ref[:, sl] = (xc * lax.rsqrt(var + EPS) * p_ref[0:1, sl] + p_ref[1:2, sl]).astype(BF16)

    def qk_norm_rotary(scale):
        sq = jnp.concatenate([z[:, g * IN_GRP:(g + 1) * IN_GRP] for g in range(ngrp)], axis=0)
        ms = jnp.dot((sq * sq).astype(BF16), bd_ref[...], preferred_element_type=F32)
        for s in range(IN_TN // LANES):
            sl = slice(s * LANES, (s + 1) * LANES)
            g, c = divmod(s * LANES, IN_GRP)
            ms_s = ms[g * tm:(g + 1) * tm, c:c + LANES]
            y = z[:, sl] * lax.rsqrt(ms_s + EPS) * p_ref[0:1, sl]
            r = (y * c_ref[...] + pltpu.roll(y, LANES - ROT_DIM // 2, 1) * s1_ref[...]
                 + pltpu.roll(y, ROT_DIM // 2, 1) * s2_ref[...])
            o_ref[:, sl] = (r * scale if scale != 1.0 else r).astype(BF16)

    @pl.when(j == 2)
    def _():
        qk_norm_rotary(math.log2(math.e) / math.sqrt(QK_DIM))

    @pl.when(j == 3)
    def _():
        qk_norm_rotary(1.0)

    @pl.when(j == 4)
    def _():
        o_ref[...] = z.astype(BF16)


def _in_proj(x2, g, w_bf, params, bd, cos_t, sup_t, sdn_t, seq):
    T = x2.shape[0]
    spt = seq // IN_TM
    return pl.pallas_call(
        _in_proj_kernel,
        grid=(T // IN_TM, IN_WIDTH // IN_TN),
        in_specs=[
            pl.BlockSpec((IN_TM, D_MODEL), lambda i, j: (i, 0)),
            pl.BlockSpec((1, D_MODEL), lambda i, j: (0, 0)),
            pl.BlockSpec((D_MODEL, IN_TN), lambda i, j: (0, j)),
            pl.BlockSpec((8, IN_TN), lambda i, j: (0, j)),
            pl.BlockSpec((IN_GRP, IN_GRP), lambda i, j: (0, 0)),
            pl.BlockSpec((IN_TM, LANES), lambda i, j: (i % spt, 0)),
            pl.BlockSpec((IN_TM, LANES), lambda i, j: (i % spt, 0)),
            pl.BlockSpec((IN_TM, LANES), lambda i, j: (i % spt, 0)),
        ],
        out_specs=pl.BlockSpec((IN_TM, IN_TN), lambda i, j: (i, j)),
        out_shape=jax.ShapeDtypeStruct((T, IN_WIDTH), BF16),
        scratch_shapes=[pltpu.VMEM((IN_TM, D_MODEL), BF16)],
        compiler_params=_cparams(("parallel", "arbitrary")),
        name="in_proj",
    )(x2, g, w_bf, params, bd, cos_t, sup_t, sdn_t)


def _gmlp_kernel(u_ref, v_ref, ws_ref, bs_ref, o_ref):
    for c in range(GM_TM // CHUNK):
        rows = slice(c * CHUNK, (c + 1) * CHUNK)
        for h in range(A_HEADS):
            cols = slice(h * HEAD, (h + 1) * HEAD)
            s = jnp.dot(ws_ref[h], v_ref[rows, cols], preferred_element_type=F32) + bs_ref[:, cols]
            o_ref[rows, cols] = (u_ref[rows, cols].astype(F32) * s).astype(BF16)


def _gmlp(zp, ws_bf, bs_full):
    T = zp.shape[0]
    return pl.pallas_call(
        _gmlp_kernel,
        grid=(T // GM_TM,),
        in_specs=[
            pl.BlockSpec((GM_TM, A_WIDTH), lambda i: (i, 0)),
            pl.BlockSpec((GM_TM, A_WIDTH), lambda i: (i, 1)),
            pl.BlockSpec((A_HEADS, CHUNK, CHUNK), lambda i: (0, 0, 0)),
            pl.BlockSpec((CHUNK, A_WIDTH), lambda i: (0, 0)),
        ],
        out_specs=pl.BlockSpec((GM_TM, A_WIDTH), lambda i: (i, 0)),
        out_shape=jax.ShapeDtypeStruct((T, A_WIDTH), BF16),
        compiler_params=_cparams(("parallel",)),
        name="gmlp",
    )(zp, zp, ws_bf, bs_full)


def _attn_kernel(lam_ref, q_ref, k_ref, v_ref, g_ref, wg_ref, wu_ref, wd_ref,
                 o_ref, wgb_ref, wub_ref, wdb_ref, vt_ref):
    seq = k_ref.shape[0]
    nc = seq // ATT_TK

    wgb_ref[...] = wg_ref[...].astype(BF16)
    wub_ref[...] = wu_ref[...].astype(BF16)
    wdb_ref[...] = wd_ref[...].astype(BF16)

    @pl.when(pl.program_id(2) == 0)
    def _():
        for c in range(nc):
            cols = slice(c * ATT_TK, (c + 1) * ATT_TK)
            vt_ref[0:HEAD, cols] = v_ref[cols, :].astype(F32).T.astype(BF16)
        vt_ref[HEAD:, :] = jnp.ones((ONES_ROWS, seq), BF16)

    q_t = q_ref[...].astype(F32).T
    row = lax.broadcasted_iota(I32, q_t.shape, 0)
    q1_t = jnp.where(row < QK_DIM, q_t, 0.0).astype(BF16)
    q2_t = jnp.where(row >= QK_DIM, q_t, 0.0).astype(BF16)

    def scores(c, qh_t):
        return jnp.dot(k_ref[c * ATT_TK:(c + 1) * ATT_TK, :], qh_t, preferred_element_type=F32)

    def update(c, s_t, m, acc):
        m_new = jnp.maximum(m, jnp.max(s_t, axis=0, keepdims=True))
        p = jnp.exp2(s_t - m_new).astype(BF16)
        pv = jnp.dot(vt_ref[:, c * ATT_TK:(c + 1) * ATT_TK], p, preferred_element_type=F32)
        return m_new, jnp.exp2(m - m_new) * acc + pv

    tq = q_t.shape[1]
    m1 = m2 = jnp.full((1, tq), -jnp.inf, F32)
    a1 = a2 = jnp.zeros((HEAD + ONES_ROWS, tq), F32)
    s1, s2 = scores(0, q1_t), scores(0, q2_t)
    for c in range(nc):
        if c + 1 < nc:
            n1, n2 = scores(c + 1, q1_t), scores(c + 1, q2_t)
        m1, a1 = update(c, s1, m1, a1)
        m2, a2 = update(c, s2, m2, a2)
        if c + 1 < nc:
            s1, s2 = n1, n2
    o_t = (a1[0:HEAD] / a1[HEAD:HEAD + 1]
           - lam_ref[0] * (a2[0:HEAD] / a2[HEAD:HEAD + 1]))
    o = o_t.T
    ms = jnp.mean(o * o, axis=-1, keepdims=True)
    o_ref[...] = (o * lax.rsqrt(ms + EPS) * g_ref[...] * (1.0 - LAMBDA_INIT)).astype(BF16)


def _attention(lam, zp3, subln_g, wg2, wu2, wd2):
    bsz, seq, _ = zp3.shape
    nq = seq // ATT_TQ
    steps = bsz * B_HEADS * nq
    qb, kb, vb = (2 * A_WIDTH) // HEAD, (2 * A_WIDTH + 1024) // HEAD, (2 * A_WIDTH + 2048) // HEAD

    def w_spec(w):
        rows = w.shape[0] // steps
        return pl.BlockSpec((rows, w.shape[1]), lambda b, h, i, lam: ((b * B_HEADS + h) * nq + i, 0))

    w_specs = [w_spec(wg2), w_spec(wu2), w_spec(wd2)]
    grid_spec = pltpu.PrefetchScalarGridSpec(
        num_scalar_prefetch=1,
        grid=(bsz, B_HEADS, nq),
        in_specs=[
            pl.BlockSpec((None, ATT_TQ, HEAD), lambda b, h, i, lam: (b, i, qb + h)),
            pl.BlockSpec((None, seq, HEAD), lambda b, h, i, lam: (b, 0, kb + h)),
            pl.BlockSpec((None, seq, HEAD), lambda b, h, i, lam: (b, 0, vb + h)),
            pl.BlockSpec((1, HEAD), lambda b, h, i, lam: (0, 0)),
        ] + w_specs,
        out_specs=[pl.BlockSpec((None, ATT_TQ, HEAD), lambda b, h, i, lam: (b, i, h))] + w_specs,
        scratch_shapes=[pltpu.VMEM((HEAD + ONES_ROWS, seq), BF16)],
    )
    return pl.pallas_call(
        _attn_kernel,
        grid_spec=grid_spec,
        out_shape=[jax.ShapeDtypeStruct((bsz, seq, B_HEADS * HEAD), BF16)]
                  + [jax.ShapeDtypeStruct(w.shape, BF16) for w in (wg2, wu2, wd2)],
        compiler_params=_cparams(("parallel", "parallel", "arbitrary")),
        name="attn",
    )(lam, zp3, zp3, zp3, subln_g, wg2, wu2, wd2)


def _out_proj_kernel(a_ref, b_ref, x_ref, wa_ref, wb_ref, g_ref, wr_ref, br_ref, x1_ref, hm_ref, lg_ref):
    acc = (jnp.dot(a_ref[...], wa_ref[...], preferred_element_type=F32)
           + jnp.dot(b_ref[...], wb_ref[...], preferred_element_type=F32))
    x1 = x_ref[...] + acc
    x1_ref[...] = x1
    ms = jnp.mean(x1 * x1, axis=-1, keepdims=True)
    hm = x1 * lax.rsqrt(ms + EPS) * g_ref[...]
    for s in range(ROW_SUB):
        hm_ref[pl.ds(s, OUT_TM, stride=ROW_SUB), :] = hm[:, s * LANES:(s + 1) * LANES]
    lg_ref[...] = lax.dot_general(wr_ref[...], hm.astype(BF16), (((1,), (1,)), ((), ())),
                                  preferred_element_type=F32) + br_ref[...]


def _out_proj(out_a, out_b, x2, w_out_bf, g, w_rt, b_rt):
    T = x2.shape[0]
    return pl.pallas_call(
        _out_proj_kernel,
        grid=(T // OUT_TM,),
        in_specs=[
            pl.BlockSpec((OUT_TM, A_WIDTH), lambda i: (i, 0)),
            pl.BlockSpec((OUT_TM, A_WIDTH), lambda i: (i, 0)),
            pl.BlockSpec((OUT_TM, D_MODEL), lambda i: (i, 0)),
            pl.BlockSpec((A_WIDTH, D_MODEL), lambda i: (0, 0)),
            pl.BlockSpec((A_WIDTH, D_MODEL), lambda i: (1, 0)),
            pl.BlockSpec((1, D_MODEL), lambda i: (0, 0)),
            pl.BlockSpec((LG_ROWS, D_MODEL), lambda i: (0, 0)),
            pl.BlockSpec((LG_ROWS, 1), lambda i: (0, 0)),
        ],
        out_specs=[
            pl.BlockSpec((OUT_TM, D_MODEL), lambda i: (i, 0)),
            pl.BlockSpec((OUT_TM * ROW_SUB, LANES), lambda i: (i, 0)),
            pl.BlockSpec((LG_ROWS, OUT_TM), lambda i: (0, i)),
        ],
        out_shape=[
            jax.ShapeDtypeStruct((T, D_MODEL), F32),
            jax.ShapeDtypeStruct((T * ROW_SUB, LANES), F32),
            jax.ShapeDtypeStruct((LG_ROWS, T), F32),
        ],
        compiler_params=_cparams(("parallel",)),
        name="out_proj",
    )(out_a, out_b, x2, w_out_bf, w_out_bf, g, w_rt, b_rt)


def _route_kernel(lg_ref, tri_ref, idr_ref, cw_ref, cnt_ref, carry_ref):
    i = pl.program_id(0)

    @pl.when(i == 0)
    def _():
        carry_ref[...] = jnp.zeros_like(carry_ref)

    tn = lg_ref.shape[1]
    row = lax.broadcasted_iota(I32, (SUBLANES, tn), 0)
    neg = jnp.float32(-jnp.inf)

    def top1(v):
        mx = jnp.max(v, axis=0, keepdims=True)
        idx = jnp.min(jnp.where(v == mx, row, SUBLANES), axis=0, keepdims=True)
        return mx, idx

    g = jnp.where(row < N_GROUPS, lg_ref[0:SUBLANES, :], neg)
    gmax, gidx = top1(g)
    g_w = 1.0 / jnp.sum(jnp.exp(g - gmax), axis=0, keepdims=True)
    sel = lg_ref[SUBLANES * N_GROUPS:SUBLANES * (N_GROUPS + 1), :]
    for grp in range(N_GROUPS - 2, -1, -1):
        sel = jnp.where(gidx == grp, lg_ref[SUBLANES * (grp + 1):SUBLANES * (grp + 2), :], sel)
    e1, i1 = top1(sel)
    e2, i2 = top1(jnp.where(row == i1, neg, sel))
    d = jnp.exp(e2 - e1)
    w1 = 1.0 / (1.0 + d)
    w2 = d * w1
    id0 = gidx * EPG + i1
    id1 = gidx * EPG + i2

    erow = lax.broadcasted_iota(I32, (N_EXPERTS, tn), 0)
    o0 = erow == id0
    o1 = erow == id1
    occ = jnp.where(o0 | o1, 1.0, 0.0).astype(F32)
    before = jnp.dot(occ.astype(BF16), tri_ref[...], preferred_element_type=F32) + carry_ref[:, 0:1]
    r0 = jnp.sum(jnp.where(o0, before, 0.0), axis=0, keepdims=True)
    r1 = jnp.sum(jnp.where(o1, before, 0.0), axis=0, keepdims=True)
    total = carry_ref[...] + jnp.sum(occ, axis=1, keepdims=True)
    carry_ref[...] = total
    cnt_ref[...] = total.astype(I32)

    zi = jnp.zeros((SUBLANES - 4, tn), I32)
    idr_ref[...] = jnp.concatenate([id0, id1, r0.astype(I32), r1.astype(I32), zi], axis=0)
    zf = jnp.zeros((SUBLANES - 2, tn), F32)
    cw_ref[...] = jnp.concatenate([g_w * w1, g_w * w2, zf], axis=0)


def _route(lg_t, tri):
    T = lg_t.shape[1]
    return pl.pallas_call(
        _route_kernel,
        grid=(T // RT_TN,),
        in_specs=[
            pl.BlockSpec((LG_ROWS, RT_TN), lambda i: (0, i)),
            pl.BlockSpec((RT_TN, RT_TN), lambda i: (0, 0)),
        ],
        out_specs=[
            pl.BlockSpec((SUBLANES, RT_TN), lambda i: (0, i)),
            pl.BlockSpec((SUBLANES, RT_TN), lambda i: (0, i)),
            pl.BlockSpec((N_EXPERTS, LANES), lambda i: (0, 0)),
        ],
        out_shape=[
            jax.ShapeDtypeStruct((SUBLANES, T), I32),
            jax.ShapeDtypeStruct((SUBLANES, T), F32),
            jax.ShapeDtypeStruct((N_EXPERTS, LANES), I32),
        ],
        scratch_shapes=[pltpu.VMEM((N_EXPERTS, LANES), F32)],
        compiler_params=_cparams(("arbitrary",)),
        name="route",
    )(lg_t, tri)


def _row_copy(src_ref, src_row, dst_ref, dst_row, sem):
    return pltpu.make_async_copy(src_ref.at[pl.ds(src_row * ROW_SUB, ROW_SUB)],
                                 dst_ref.at[pl.ds(dst_row * ROW_SUB, ROW_SUB)], sem)


def _dispatch_kernel(zt_ref, ps_ref, idr_ref, hm_ref, xs_ref, zbuf, sem):
    i = pl.program_id(0)

    def zero_copy(e):
        return pltpu.make_async_copy(
            zbuf, xs_ref.at[pl.ds(pl.multiple_of(zt_ref[e] * ROW_SUB, ROW_SUB), EXP_BLK * ROW_SUB)], sem)

    @pl.when(i == 0)
    def _():
        zbuf[...] = jnp.zeros_like(zbuf)
        for e in range(2 * N_EXPERTS):
            @pl.when(zt_ref[e] >= 0)
            def _():
                zero_copy(e).start()
        for e in range(2 * N_EXPERTS):
            @pl.when(zt_ref[e] >= 0)
            def _():
                zero_copy(e).wait()

    def issue(t, carry):
        for k in range(TOP_K):
            dst = ps_ref[idr_ref[k, t]] + idr_ref[TOP_K + k, t]
            _row_copy(hm_ref, t, xs_ref, dst, sem).start()
        return carry

    lax.fori_loop(0, DSP_TC, issue, 0)

    def drain(t, carry):
        for k in range(TOP_K):
            _row_copy(hm_ref, 0, xs_ref, 0, sem).wait()
        return carry

    lax.fori_loop(0, DSP_TC, drain, 0)


def _dispatch(zt, pstart, idr, hm3, n_rows):
    T = idr.shape[1]
    grid_spec = pltpu.PrefetchScalarGridSpec(
        num_scalar_prefetch=2,
        grid=(T // DSP_TC,),
        in_specs=[
            pl.BlockSpec((SUBLANES, DSP_TC), lambda i, zt, ps: (0, i), memory_space=pltpu.SMEM),
            pl.BlockSpec((DSP_TC * ROW_SUB, LANES), lambda i, zt, ps: (i, 0)),
        ],
        out_specs=pl.BlockSpec(memory_space=pl.ANY),
        scratch_shapes=[pltpu.VMEM((EXP_BLK * ROW_SUB, LANES), F32), pltpu.SemaphoreType.DMA(())],
    )
    return pl.pallas_call(
        _dispatch_kernel,
        grid_spec=grid_spec,
        out_shape=jax.ShapeDtypeStruct((n_rows * ROW_SUB, LANES), F32),
        compiler_params=_cparams(("arbitrary",)),
        name="dispatch",
    )(zt, pstart, idr, hm3)


def _expert_kernel(be_ref, nu_ref, xs_ref, wg_ref, wu_ref, wd_ref, y_ref, xb_ref):
    b = pl.program_id(0)

    @pl.when(b < nu_ref[0])
    def _():
        for s in range(ROW_SUB):
            xb_ref[:, s * LANES:(s + 1) * LANES] = xs_ref[pl.ds(s, EXP_BLK, stride=ROW_SUB), :].astype(BF16)
        x = xb_ref[...]
        g = jnp.dot(x, wg_ref[...], preferred_element_type=F32)
        u = jnp.dot(x, wu_ref[...], preferred_element_type=F32)
        a = (jax.nn.silu(g) * u).astype(BF16)
        y = jnp.dot(a, wd_ref[...], preferred_element_type=F32)
        for s in range(ROW_SUB):
            y_ref[pl.ds(s, EXP_BLK, stride=ROW_SUB), :] = y[:, s * LANES:(s + 1) * LANES]

    @pl.when(b >= nu_ref[0])
    def _():
        y_ref[...] = jnp.zeros_like(y_ref)


def _experts(block_e, n_used, xs3, wg, wu, wd):
    n_blocks = xs3.shape[0] // (EXP_BLK * ROW_SUB)

    def xs_map(b, be, nu):
        return (jnp.minimum(b, nu[0] - 1), 0)

    def w_map(b, be, nu):
        return (be[b], 0, 0)

    grid_spec = pltpu.PrefetchScalarGridSpec(
        num_scalar_prefetch=2,
        grid=(n_blocks,),
        in_specs=[
            pl.BlockSpec((EXP_BLK * ROW_SUB, LANES), xs_map),
            pl.BlockSpec((None, D_MODEL, D_EXPERT), w_map),
            pl.BlockSpec((None, D_MODEL, D_EXPERT), w_map),
            pl.BlockSpec((None, D_EXPERT, D_MODEL), w_map),
        ],
        out_specs=pl.BlockSpec((EXP_BLK * ROW_SUB, LANES), lambda b, be, nu: (b, 0)),
        scratch_shapes=[pltpu.VMEM((EXP_BLK, D_MODEL), BF16)],
    )
    return pl.pallas_call(
        _expert_kernel,
        grid_spec=grid_spec,
        out_shape=jax.ShapeDtypeStruct(xs3.shape, F32),
        compiler_params=_cparams(("arbitrary",)),
        name="experts",
    )(block_e, n_used, xs3, wg, wu, wd)


def _combine_kernel(ps_ref, idr_ref, x1_ref, cw_ref, y_ref, o_ref, yb0, yb1, sem):
    def issue(t, carry):
        for k, yb in enumerate((yb0, yb1)):
            src = ps_ref[idr_ref[k, t]] + idr_ref[TOP_K + k, t]
            _row_copy(y_ref, src, yb, t, sem).start()
        return carry

    lax.fori_loop(0, CMB_TC, issue, 0)

    def drain(t, carry):
        for yb in (yb0, yb1):
            _row_copy(y_ref, 0, yb, 0, sem).wait()
        return carry

    lax.fori_loop(0, CMB_TC, drain, 0)
    w_t = cw_ref[...].T
    w0 = w_t[:, 0:1]
    w1 = w_t[:, 1:2]
    for s in range(ROW_SUB):
        sl = slice(s * LANES, (s + 1) * LANES)
        o_ref[:, sl] = (x1_ref[:, sl] + w0 * yb0[pl.ds(s, CMB_TC, stride=ROW_SUB), :]
                        + w1 * yb1[pl.ds(s, CMB_TC, stride=ROW_SUB), :])


def _combine(pstart, idr, x1, cw, y3):
    T = x1.shape[0]
    grid_spec = pltpu.PrefetchScalarGridSpec(
        num_scalar_prefetch=1,
        grid=(T // CMB_TC,),
        in_specs=[
            pl.BlockSpec((SUBLANES, CMB_TC), lambda i, ps: (0, i), memory_space=pltpu.SMEM),
            pl.BlockSpec((CMB_TC, D_MODEL), lambda i, ps: (i, 0)),
            pl.BlockSpec((SUBLANES, CMB_TC), lambda i, ps: (0, i)),
            pl.BlockSpec(memory_space=pl.ANY),
        ],
        out_specs=pl.BlockSpec((CMB_TC, D_MODEL), lambda i, ps: (i, 0)),
        scratch_shapes=[
            pltpu.VMEM((CMB_TC * ROW_SUB, LANES), F32),
            pltpu.VMEM((CMB_TC * ROW_SUB, LANES), F32),
            pltpu.SemaphoreType.DMA(()),
        ],
    )
    return pl.pallas_call(
        _combine_kernel,
        grid_spec=grid_spec,
        out_shape=jax.ShapeDtypeStruct((T, D_MODEL), F32),
        compiler_params=_cparams(("arbitrary",)),
        name="combine",
    )(pstart, idr, x1, cw, y3)


def _rotary_lane_tables(seq):
    pos = jnp.arange(seq, dtype=F32)
    inv_freq = 1.0 / (jnp.float32(ROPE_THETA) ** (jnp.arange(0, ROT_DIM, 2, dtype=F32) / ROT_DIM))
    ang = pos[:, None] * inv_freq[None, :]
    cos, sin = jnp.cos(ang), jnp.sin(ang)
    half = ROT_DIM // 2
    lane = jnp.arange(LANES) % QK_DIM
    fidx = lane % half
    first = (lane < half)[None, :]
    second = ((lane >= half) & (lane < ROT_DIM))[None, :]
    cos_t = jnp.where(first | second, cos[:, fidx], 1.0)
    s_up = jnp.where(first, -sin[:, fidx], 0.0)
    s_dn = jnp.where(second, sin[:, fidx], 0.0)
    return cos_t, s_up, s_dn


def kernel(x, attn_norm_g, w_in, gmlp_ln_g, gmlp_ln_b, gmlp_ws, gmlp_bs, q_norm_g, k_norm_g,
           lambda_q1, lambda_k1, lambda_q2, lambda_k2, subln_g, w_out, ffn_norm_g,
           w_group, b_group, w_router, b_router, w_gate, w_up, w_down):
    bsz, seq, d = x.shape
    T = bsz * seq
    assert d == D_MODEL and seq % IN_TM == 0 and seq % ATT_TQ == 0 and T % RT_TN == 0
    l = 0
    x2 = x.reshape(T, d)

    params = jnp.zeros((8, IN_WIDTH), F32)
    params = params.at[0, A_WIDTH:2 * A_WIDTH].set(gmlp_ln_g[l].reshape(-1))
    params = params.at[1, A_WIDTH:2 * A_WIDTH].set(gmlp_ln_b[l].reshape(-1))
    params = params.at[0, 2 * A_WIDTH:3 * A_WIDTH].set(jnp.tile(q_norm_g[l], 2 * B_HEADS))
    params = params.at[0, 3 * A_WIDTH:4 * A_WIDTH].set(jnp.tile(k_norm_g[l], 2 * B_HEADS))
    blk = jnp.arange(IN_GRP) // QK_DIM
    bd = jnp.where(blk[:, None] == blk[None, :], 1.0 / QK_DIM, 0.0).astype(BF16)
    cos_t, s_up, s_dn = _rotary_lane_tables(seq)
    bs_full = jnp.repeat(jnp.transpose(gmlp_bs[l]), HEAD, axis=1)
    lam = (jnp.exp(jnp.sum(lambda_q1[l] * lambda_k1[l])) - jnp.exp(jnp.sum(lambda_q2[l] * lambda_k2[l]))
           + LAMBDA_INIT).reshape(1).astype(F32)
    w_rt = jnp.zeros((LG_ROWS, D_MODEL), F32)
    w_rt = w_rt.at[:N_GROUPS].set(w_group[l].T).at[SUBLANES:].set(w_router[l].T)
    b_rt = jnp.zeros((LG_ROWS, 1), F32)
    b_rt = b_rt.at[:N_GROUPS, 0].set(b_group[l]).at[SUBLANES:, 0].set(b_router[l])
    tri = (jnp.arange(RT_TN)[:, None] < jnp.arange(RT_TN)[None, :]).astype(BF16)

    zp = _in_proj(x2, attn_norm_g[l].reshape(1, d), w_in[l].astype(BF16), params, bd,
                  cos_t, s_up, s_dn, seq)
    out_a = _gmlp(zp, gmlp_ws[l].astype(BF16), bs_full)
    att_steps = bsz * B_HEADS * (seq // ATT_TQ)
    assert (N_EXPERTS * D_MODEL) % att_steps == 0 and (N_EXPERTS * D_EXPERT) % att_steps == 0
    out_b, wg_bf, wu_bf, wd_bf = _attention(
        lam, zp.reshape(bsz, seq, IN_WIDTH), subln_g[l].reshape(1, HEAD),
        w_gate[l].reshape(N_EXPERTS * D_MODEL, D_EXPERT), w_up[l].reshape(N_EXPERTS * D_MODEL, D_EXPERT),
        w_down[l].reshape(N_EXPERTS * D_EXPERT, D_MODEL))
    x1, hm3, lg_t = _out_proj(out_a, out_b.reshape(T, A_WIDTH), x2, w_out[l].astype(BF16),
                              ffn_norm_g[l].reshape(1, d), w_rt.astype(BF16), b_rt)

    idr, cw, cnt = _route(lg_t, tri)
    counts = cnt[:, 0]
    padded = (counts + EXP_BLK - 1) // EXP_BLK * EXP_BLK
    pend = jnp.cumsum(padded)
    pstart = (pend - padded).astype(I32)
    n_blocks = (T * TOP_K) // EXP_BLK + N_EXPERTS
    n_rows = n_blocks * EXP_BLK
    block_e = jnp.minimum(jnp.sum(pend[None, :] <= (jnp.arange(n_blocks) * EXP_BLK)[:, None], axis=1),
                          N_EXPERTS - 1).astype(I32)
    n_used = (pend[-1] // EXP_BLK).reshape(1).astype(I32)
    trail = pend[-1] + jnp.arange(N_EXPERTS) * EXP_BLK
    zt = jnp.concatenate([jnp.where(counts > 0, pend - EXP_BLK, -1),
                          jnp.where(trail < n_rows, trail, -1)]).astype(I32)

    xs3 = _dispatch(zt, pstart, idr, hm3, n_rows)
    y3 = _experts(block_e, n_used, xs3, wg_bf.reshape(N_EXPERTS, D_MODEL, D_EXPERT),
                  wu_bf.reshape(N_EXPERTS, D_MODEL, D_EXPERT), wd_bf.reshape(N_EXPERTS, D_EXPERT, D_MODEL))
    out = _combine(pstart, idr, x1, cw, y3)
    return out.reshape(bsz, seq, d)
```

```python
import math

import jax
import jax.numpy as jnp
from jax import lax
from jax.experimental import pallas as pl
from jax.experimental.pallas import tpu as pltpu

F32 = jnp.float32
BF16 = jnp.bfloat16
I32 = jnp.int32

D_MODEL = 2048
A_WIDTH = 1024
A_HEADS = 8
HEAD = 128
CHUNK = 128
B_HEADS = 8
QK_DIM = 64
ROT_DIM = 16
ROPE_THETA = 500000.0
IN_WIDTH = 5120
N_GROUPS = 4
EPG = 8
N_EXPERTS = 32
TOP_K = 2
D_EXPERT = 1024
EPS = 1e-6
LAMBDA_INIT = 0.8 - 0.6 * math.exp(-0.3 * 0)

LANES = 128
SUBLANES = 8
ROW_SUB = D_MODEL // LANES
LG_ROWS = SUBLANES * (1 + N_GROUPS)

IN_TM = 512
IN_TN = 1024
IN_GRP = 256
GM_TM = 512
ATT_TQ = 512
ATT_TK = 512
ONES_ROWS = 16
OUT_TM = 256
RT_TN = 512
EXP_BLK = 256
DSP_TC = 256
CMB_TC = 256
DMA_UNROLL = 8
VMEM_LIMIT = 48 * 1024 * 1024


def _cparams(sem):
    return pltpu.CompilerParams(dimension_semantics=sem, vmem_limit_bytes=VMEM_LIMIT)


def _to_token_rows(x):
    rows = x.shape[0]
    slabs = jnp.stack([x[:, s * LANES:(s + 1) * LANES] for s in range(ROW_SUB)], axis=0)
    return pltpu.einshape("stl->tsl", slabs).reshape(rows * ROW_SUB, LANES)


def _from_token_rows(x3):
    rows = x3.shape[0] // ROW_SUB
    return pltpu.einshape("tsl->stl", x3.reshape(rows, ROW_SUB, LANES))


def _in_proj_kernel(x_ref, g_ref, w_ref, p_ref, bd_ref, c_ref, s1_ref, s2_ref, o_ref, hn_ref):
    j = pl.program_id(1)

    @pl.when(j == 0)
    def _():
        x = x_ref[...]
        ms = jnp.mean(x * x, axis=-1, keepdims=True)
        hn_ref[...] = (x * lax.rsqrt(ms + EPS) * g_ref[...]).astype(BF16)

    z = jnp.dot(hn_ref[...], w_ref[...], preferred_element_type=F32)
    tm = z.shape[0]
    ngrp = IN_TN // IN_GRP

    @pl.when(j == 0)
    def _():
        o_ref[...] = jax.nn.gelu(z).astype(BF16)

    @pl.when(j == 1)
    def _():
        for s in range(IN_TN // LANES):
            sl = slice(s * LANES, (s + 1) * LANES)
            gz = jax.nn.gelu(z[:, sl])
            mu = jnp.mean(gz, axis=-1, keepdims=True)
            xc = gz - mu
            var = jnp.mean(xc * xc, axis=-1, keepdims=True)
            o_ref[:, sl] = (xc * lax.rsqrt(var + EPS) * p_ref[0:1, sl] + p_ref[1:2, sl]).astype(BF16)

    def qk_norm_rotary(scale):
        sq = jnp.concatenate([z[:, g * IN_GRP:(g + 1) * IN_GRP] for g in range(ngrp)], axis=0)
        ms = jnp.dot((sq * sq).astype(BF16), bd_ref[...], preferred_element_type=F32)
        for s in range(IN_TN // LANES):
            sl = slice(s * LANES, (s + 1) * LANES)
            g, c = divmod(s * LANES, IN_GRP)
            ms_s = ms[g * tm:(g + 1) * tm, c:c + LANES]
            y = z[:, sl] * lax.rsqrt(ms_s + EPS) * p_ref[0:1, sl]
            r = (y * c_ref[...] + pltpu.roll(y, LANES - ROT_DIM // 2, 1) * s1_ref[...]
                 + pltpu.roll(y, ROT_DIM // 2, 1) * s2_ref[...])
            o_ref[:, sl] = (r * scale if scale != 1.0 else r).astype(BF16)

    @pl.when(j == 2)
    def _():
        qk_norm_rotary(math.log2(math.e) / math.sqrt(QK_DIM))

    @pl.when(j == 3)
    def _():
        qk_norm_rotary(1.0)

    @pl.when(j == 4)
    def _():
        o_ref[...] = z.astype(BF16)


def _in_proj(x2, g, w_bf, params, bd, cos_t, sup_t, sdn_t, seq):
    T = x2.shape[0]
    spt = seq // IN_TM
    return pl.pallas_call(
        _in_proj_kernel,
        grid=(T // IN_TM, IN_WIDTH // IN_TN),
        in_specs=[
            pl.BlockSpec((IN_TM, D_MODEL), lambda i, j: (i, 0)),
            pl.BlockSpec((1, D_MODEL), lambda i, j: (0, 0)),
            pl.BlockSpec((D_MODEL, IN_TN), lambda i, j: (0, j)),
            pl.BlockSpec((8, IN_TN), lambda i, j: (0, j)),
            pl.BlockSpec((IN_GRP, IN_GRP), lambda i, j: (0, 0)),
            pl.BlockSpec((IN_TM, LANES), lambda i, j: (i % spt, 0)),
            pl.BlockSpec((IN_TM, LANES), lambda i, j: (i % spt, 0)),
            pl.BlockSpec((IN_TM, LANES), lambda i, j: (i % spt, 0)),
        ],
        out_specs=pl.BlockSpec((IN_TM, IN_TN), lambda i, j: (i, j)),
        out_shape=jax.ShapeDtypeStruct((T, IN_WIDTH), BF16),
        scratch_shapes=[pltpu.VMEM((IN_TM, D_MODEL), BF16)],
        compiler_params=_cparams(("parallel", "arbitrary")),
        name="in_proj",
    )(x2, g, w_bf, params, bd, cos_t, sup_t, sdn_t)


def _gmlp_kernel(u_ref, v_ref, ws_ref, bs_ref, o_ref):
    for c in range(GM_TM // CHUNK):
        rows = slice(c * CHUNK, (c + 1) * CHUNK)
        for h in range(A_HEADS):
            cols = slice(h * HEAD, (h + 1) * HEAD)
            s = jnp.dot(ws_ref[h], v_ref[rows, cols], preferred_element_type=F32) + bs_ref[:, cols]
            o_ref[rows, cols] = (u_ref[rows, cols].astype(F32) * s).astype(BF16)


def _gmlp(zp, ws_bf, bs_full):
    T = zp.shape[0]
    return pl.pallas_call(
        _gmlp_kernel,
        grid=(T // GM_TM,),
        in_specs=[
            pl.BlockSpec((GM_TM, A_WIDTH), lambda i: (i, 0)),
            pl.BlockSpec((GM_TM, A_WIDTH), lambda i: (i, 1)),
            pl.BlockSpec((A_HEADS, CHUNK, CHUNK), lambda i: (0, 0, 0)),
            pl.BlockSpec((CHUNK, A_WIDTH), lambda i: (0, 0)),
        ],
        out_specs=pl.BlockSpec((GM_TM, A_WIDTH), lambda i: (i, 0)),
        out_shape=jax.ShapeDtypeStruct((T, A_WIDTH), BF16),
        compiler_params=_cparams(("parallel",)),
        name="gmlp",
    )(zp, zp, ws_bf, bs_full)


def _attn_kernel(lam_ref, q_ref, k_ref, v_ref, g_ref, wg_ref, wu_ref, wd_ref,
                 o_ref, wgb_ref, wub_ref, wdb_ref, vt_ref):
    seq = k_ref.shape[0]
    nc = seq // ATT_TK

    wgb_ref[...] = wg_ref[...].astype(BF16)
    wub_ref[...] = wu_ref[...].astype(BF16)
    wdb_ref[...] = wd_ref[...].astype(BF16)

    @pl.when(pl.program_id(2) == 0)
    def _():
        for c in range(nc):
            cols = slice(c * ATT_TK, (c + 1) * ATT_TK)
            vt_ref[0:HEAD, cols] = v_ref[cols, :].astype(F32).T.astype(BF16)
        vt_ref[HEAD:, :] = jnp.ones((ONES_ROWS, seq), BF16)

    q_t = q_ref[...].astype(F32).T
    row = lax.broadcasted_iota(I32, q_t.shape, 0)
    q1_t = jnp.where(row < QK_DIM, q_t, 0.0).astype(BF16)
    q2_t = jnp.where(row >= QK_DIM, q_t, 0.0).astype(BF16)

    def scores(c, qh_t):
        return jnp.dot(k_ref[c * ATT_TK:(c + 1) * ATT_TK, :], qh_t, preferred_element_type=F32)

    def update(c, s_t, m, acc):
        m_new = jnp.maximum(m, jnp.max(s_t, axis=0, keepdims=True))
        p = jnp.exp2(s_t - m_new).astype(BF16)
        pv = jnp.dot(vt_ref[:, c * ATT_TK:(c + 1) * ATT_TK], p, preferred_element_type=F32)
        return m_new, jnp.exp2(m - m_new) * acc + pv

    tq = q_t.shape[1]
    m1 = m2 = jnp.full((1, tq), -jnp.inf, F32)
    a1 = a2 = jnp.zeros((HEAD + ONES_ROWS, tq), F32)
    s1, s2 = scores(0, q1_t), scores(0, q2_t)
    for c in range(nc):
        if c + 1 < nc:
            n1, n2 = scores(c + 1, q1_t), scores(c + 1, q2_t)
        m1, a1 = update(c, s1, m1, a1)
        m2, a2 = update(c, s2, m2, a2)
        if c + 1 < nc:
            s1, s2 = n1, n2
    o_t = (a1[0:HEAD] / a1[HEAD:HEAD + 1]
           - lam_ref[0] * (a2[0:HEAD] / a2[HEAD:HEAD + 1]))
    o = o_t.T
    ms = jnp.mean(o * o, axis=-1, keepdims=True)
    o_ref[...] = (o * lax.rsqrt(ms + EPS) * g_ref[...] * (1.0 - LAMBDA_INIT)).astype(BF16)


def _attention(lam, zp3, subln_g, wg2, wu2, wd2):
    bsz, seq, _ = zp3.shape
    nq = seq // ATT_TQ
    steps = bsz * B_HEADS * nq
    qb, kb, vb = (2 * A_WIDTH) // HEAD, (2 * A_WIDTH + 1024) // HEAD, (2 * A_WIDTH + 2048) // HEAD

    def w_spec(w):
        rows = w.shape[0] // steps
        return pl.BlockSpec((rows, w.shape[1]), lambda b, h, i, lam: ((b * B_HEADS + h) * nq + i, 0))

    w_specs = [w_spec(wg2), w_spec(wu2), w_spec(wd2)]
    grid_spec = pltpu.PrefetchScalarGridSpec(
        num_scalar_prefetch=1,
        grid=(bsz, B_HEADS, nq),
        in_specs=[
            pl.BlockSpec((None, ATT_TQ, HEAD), lambda b, h, i, lam: (b, i, qb + h)),
            pl.BlockSpec((None, seq, HEAD), lambda b, h, i, lam: (b, 0, kb + h)),
            pl.BlockSpec((None, seq, HEAD), lambda b, h, i, lam: (b, 0, vb + h)),
            pl.BlockSpec((1, HEAD), lambda b, h, i, lam: (0, 0)),
        ] + w_specs,
        out_specs=[pl.BlockSpec((None, ATT_TQ, HEAD), lambda b, h, i, lam: (b, i, h))] + w_specs,
        scratch_shapes=[pltpu.VMEM((HEAD + ONES_ROWS, seq), BF16)],
    )
    return pl.pallas_call(
        _attn_kernel,
        grid_spec=grid_spec,
        out_shape=[jax.ShapeDtypeStruct((bsz, seq, B_HEADS * HEAD), BF16)]
                  + [jax.ShapeDtypeStruct(w.shape, BF16) for w in (wg2, wu2, wd2)],
        compiler_params=_cparams(("parallel", "parallel", "arbitrary")),
        name="attn",
    )(lam, zp3, zp3, zp3, subln_g, wg2, wu2, wd2)


def _out_proj_kernel(a_ref, b_ref, x_ref, wa_ref, wb_ref, g_ref, wr_ref, br_ref, x1_ref, hm_ref, lg_ref):
    acc = (jnp.dot(a_ref[...], wa_ref[...], preferred_element_type=F32)
           + jnp.dot(b_ref[...], wb_ref[...], preferred_element_type=F32))
    x1 = x_ref[...] + acc
    x1_ref[...] = x1
    ms = jnp.mean(x1 * x1, axis=-1, keepdims=True)
    hm = x1 * lax.rsqrt(ms + EPS) * g_ref[...]
    hm_ref[...] = _to_token_rows(hm)
    lg_ref[...] = lax.dot_general(wr_ref[...], hm.astype(BF16), (((1,), (1,)), ((), ())),
                                  preferred_element_type=F32) + br_ref[...]


def _out_proj(out_a, out_b, x2, w_out_bf, g, w_rt, b_rt):
    T = x2.shape[0]
    return pl.pallas_call(
        _out_proj_kernel,
        grid=(T // OUT_TM,),
        in_specs=[
            pl.BlockSpec((OUT_TM, A_WIDTH), lambda i: (i, 0)),
            pl.BlockSpec((OUT_TM, A_WIDTH), lambda i: (i, 0)),
            pl.BlockSpec((OUT_TM, D_MODEL), lambda i: (i, 0)),
            pl.BlockSpec((A_WIDTH, D_MODEL), lambda i: (0, 0)),
            pl.BlockSpec((A_WIDTH, D_MODEL), lambda i: (1, 0)),
            pl.BlockSpec((1, D_MODEL), lambda i: (0, 0)),
            pl.BlockSpec((LG_ROWS, D_MODEL), lambda i: (0, 0)),
            pl.BlockSpec((LG_ROWS, 1), lambda i: (0, 0)),
        ],
        out_specs=[
            pl.BlockSpec((OUT_TM, D_MODEL), lambda i: (i, 0)),
            pl.BlockSpec((OUT_TM * ROW_SUB, LANES), lambda i: (i, 0)),
            pl.BlockSpec((LG_ROWS, OUT_TM), lambda i: (0, i)),
        ],
        out_shape=[
            jax.ShapeDtypeStruct((T, D_MODEL), F32),
            jax.ShapeDtypeStruct((T * ROW_SUB, LANES), F32),
            jax.ShapeDtypeStruct((LG_ROWS, T), F32),
        ],
        compiler_params=_cparams(("parallel",)),
        name="out_proj",
    )(out_a, out_b, x2, w_out_bf, w_out_bf, g, w_rt, b_rt)


def _route_kernel(lg_ref, tri_ref, idr_ref, cw_ref, cnt_ref, carry_ref):
    i = pl.program_id(0)

    @pl.when(i == 0)
    def _():
        carry_ref[...] = jnp.zeros_like(carry_ref)

    tn = lg_ref.shape[1]
    row = lax.broadcasted_iota(I32, (SUBLANES, tn), 0)
    neg = jnp.float32(-jnp.inf)

    def top1(v):
        mx = jnp.max(v, axis=0, keepdims=True)
        idx = jnp.min(jnp.where(v == mx, row, SUBLANES), axis=0, keepdims=True)
        return mx, idx

    g = jnp.where(row < N_GROUPS, lg_ref[0:SUBLANES, :], neg)
    gmax, gidx = top1(g)
    g_w = 1.0 / jnp.sum(jnp.exp(g - gmax), axis=0, keepdims=True)
    sel = lg_ref[SUBLANES * N_GROUPS:SUBLANES * (N_GROUPS + 1), :]
    for grp in range(N_GROUPS - 2, -1, -1):
        sel = jnp.where(gidx == grp, lg_ref[SUBLANES * (grp + 1):SUBLANES * (grp + 2), :], sel)
    e1, i1 = top1(sel)
    e2, i2 = top1(jnp.where(row == i1, neg, sel))
    d = jnp.exp(e2 - e1)
    w1 = 1.0 / (1.0 + d)
    w2 = d * w1
    id0 = gidx * EPG + i1
    id1 = gidx * EPG + i2

    erow = lax.broadcasted_iota(I32, (N_EXPERTS, tn), 0)
    o0 = erow == id0
    o1 = erow == id1
    occ = jnp.where(o0 | o1, 1.0, 0.0).astype(F32)
    before = jnp.dot(occ.astype(BF16), tri_ref[...], preferred_element_type=F32) + carry_ref[:, 0:1]
    r0 = jnp.sum(jnp.where(o0, before, 0.0), axis=0, keepdims=True)
    r1 = jnp.sum(jnp.where(o1, before, 0.0), axis=0, keepdims=True)
    total = carry_ref[...] + jnp.sum(occ, axis=1, keepdims=True)
    carry_ref[...] = total
    cnt_ref[...] = total.astype(I32)

    zi = jnp.zeros((SUBLANES - 4, tn), I32)
    idr_ref[...] = jnp.concatenate([id0, id1, r0.astype(I32), r1.astype(I32), zi], axis=0)
    zf = jnp.zeros((SUBLANES - 2, tn), F32)
    cw_ref[...] = jnp.concatenate([g_w * w1, g_w * w2, zf], axis=0)


def _route(lg_t, tri):
    T = lg_t.shape[1]
    return pl.pallas_call(
        _route_kernel,
        grid=(T // RT_TN,),
        in_specs=[
            pl.BlockSpec((LG_ROWS, RT_TN), lambda i: (0, i)),
            pl.BlockSpec((RT_TN, RT_TN), lambda i: (0, 0)),
        ],
        out_specs=[
            pl.BlockSpec((SUBLANES, RT_TN), lambda i: (0, i)),
            pl.BlockSpec((SUBLANES, RT_TN), lambda i: (0, i)),
            pl.BlockSpec((N_EXPERTS, LANES), lambda i: (0, 0)),
        ],
        out_shape=[
            jax.ShapeDtypeStruct((SUBLANES, T), I32),
            jax.ShapeDtypeStruct((SUBLANES, T), F32),
            jax.ShapeDtypeStruct((N_EXPERTS, LANES), I32),
        ],
        scratch_shapes=[pltpu.VMEM((N_EXPERTS, LANES), F32)],
        compiler_params=_cparams(("arbitrary",)),
        name="route",
    )(lg_t, tri)


def _row_copy(src_ref, src_row, dst_ref, dst_row, sem):
    return pltpu.make_async_copy(src_ref.at[pl.ds(src_row * ROW_SUB, ROW_SUB)],
                                 dst_ref.at[pl.ds(dst_row * ROW_SUB, ROW_SUB)], sem)


def _dispatch_kernel(zt_ref, pos_ref, hm_ref, xs_ref, zbuf, sem):
    i = pl.program_id(0)

    def zero_copy(e):
        return pltpu.make_async_copy(
            zbuf, xs_ref.at[pl.ds(pl.multiple_of(zt_ref[e] * ROW_SUB, ROW_SUB), EXP_BLK * ROW_SUB)], sem)

    @pl.when(i == 0)
    def _():
        zbuf[...] = jnp.zeros_like(zbuf)
        for e in range(2 * N_EXPERTS):
            @pl.when(zt_ref[e] >= 0)
            def _():
                zero_copy(e).start()
        for e in range(2 * N_EXPERTS):
            @pl.when(zt_ref[e] >= 0)
            def _():
                zero_copy(e).wait()

    def issue(grp, carry):
        for u in range(DMA_UNROLL):
            t = grp * DMA_UNROLL + u
            for k in range(TOP_K):
                _row_copy(hm_ref, t, xs_ref, pos_ref[k, t], sem).start()
        return carry

    lax.fori_loop(0, DSP_TC // DMA_UNROLL, issue, 0)

    def drain(grp, carry):
        for _ in range(DMA_UNROLL * TOP_K):
            _row_copy(hm_ref, 0, xs_ref, 0, sem).wait()
        return carry

    lax.fori_loop(0, DSP_TC // DMA_UNROLL, drain, 0)


def _dispatch(zt, pos, hm3, n_rows):
    T = pos.shape[1]
    grid_spec = pltpu.PrefetchScalarGridSpec(
        num_scalar_prefetch=1,
        grid=(T // DSP_TC,),
        in_specs=[
            pl.BlockSpec((TOP_K, DSP_TC), lambda i, zt: (0, i), memory_space=pltpu.SMEM),
            pl.BlockSpec((DSP_TC * ROW_SUB, LANES), lambda i, zt: (i, 0)),
        ],
        out_specs=pl.BlockSpec(memory_space=pl.ANY),
        scratch_shapes=[pltpu.VMEM((EXP_BLK * ROW_SUB, LANES), F32), pltpu.SemaphoreType.DMA(())],
    )
    return pl.pallas_call(
        _dispatch_kernel,
        grid_spec=grid_spec,
        out_shape=jax.ShapeDtypeStruct((n_rows * ROW_SUB, LANES), F32),
        compiler_params=_cparams(("arbitrary",)),
        name="dispatch",
    )(zt, pos, hm3)


def _expert_kernel(be_ref, nu_ref, xs_ref, wg_ref, wu_ref, wd_ref, y_ref, xb_ref):
    b = pl.program_id(0)

    @pl.when(b < nu_ref[0])
    def _():
        slabs = _from_token_rows(xs_ref[...])
        for s in range(ROW_SUB):
            xb_ref[:, s * LANES:(s + 1) * LANES] = slabs[s].astype(BF16)
        x = xb_ref[...]
        g = jnp.dot(x, wg_ref[...], preferred_element_type=F32)
        u = jnp.dot(x, wu_ref[...], preferred_element_type=F32)
        a = (jax.nn.silu(g) * u).astype(BF16)
        y = jnp.dot(a, wd_ref[...], preferred_element_type=F32)
        y_ref[...] = _to_token_rows(y)

    @pl.when(b >= nu_ref[0])
    def _():
        y_ref[...] = jnp.zeros_like(y_ref)


def _experts(block_e, n_used, xs3, wg, wu, wd):
    n_blocks = xs3.shape[0] // (EXP_BLK * ROW_SUB)

    def xs_map(b, be, nu):
        return (jnp.minimum(b, nu[0] - 1), 0)

    def w_map(b, be, nu):
        return (be[b], 0, 0)

    grid_spec = pltpu.PrefetchScalarGridSpec(
        num_scalar_prefetch=2,
        grid=(n_blocks,),
        in_specs=[
            pl.BlockSpec((EXP_BLK * ROW_SUB, LANES), xs_map),
            pl.BlockSpec((None, D_MODEL, D_EXPERT), w_map),
            pl.BlockSpec((None, D_MODEL, D_EXPERT), w_map),
            pl.BlockSpec((None, D_EXPERT, D_MODEL), w_map),
        ],
        out_specs=pl.BlockSpec((EXP_BLK * ROW_SUB, LANES), lambda b, be, nu: (b, 0)),
        scratch_shapes=[pltpu.VMEM((EXP_BLK, D_MODEL), BF16)],
    )
    return pl.pallas_call(
        _expert_kernel,
        grid_spec=grid_spec,
        out_shape=jax.ShapeDtypeStruct(xs3.shape, F32),
        compiler_params=_cparams(("arbitrary",)),
        name="experts",
    )(block_e, n_used, xs3, wg, wu, wd)


def _combine_kernel(pos_ref, posn_ref, x1_ref, cw_ref, y_ref, o_ref, yb, sems):
    i = pl.program_id(0)
    slot = i % 2

    def gather(ids_ref, slot_):
        def issue(grp, carry):
            for u in range(DMA_UNROLL):
                t = grp * DMA_UNROLL + u
                for k in range(TOP_K):
                    _row_copy(y_ref, ids_ref[k, t], yb.at[slot_, k], t, sems.at[slot_]).start()
            return carry

        lax.fori_loop(0, CMB_TC // DMA_UNROLL, issue, 0)

    @pl.when(i == 0)
    def _():
        gather(pos_ref, 0)

    @pl.when(i + 1 < pl.num_programs(0))
    def _():
        gather(posn_ref, 1 - slot)

    def drain(grp, carry):
        for _ in range(DMA_UNROLL * TOP_K):
            _row_copy(y_ref, 0, yb.at[slot, 0], 0, sems.at[slot]).wait()
        return carry

    lax.fori_loop(0, CMB_TC // DMA_UNROLL, drain, 0)
    w_t = cw_ref[...].T
    w0 = w_t[:, 0:1]
    w1 = w_t[:, 1:2]
    y0 = _from_token_rows(yb[slot, 0])
    y1 = _from_token_rows(yb[slot, 1])
    for s in range(ROW_SUB):
        sl = slice(s * LANES, (s + 1) * LANES)
        o_ref[:, sl] = x1_ref[:, sl] + w0 * y0[s] + w1 * y1[s]


def _combine(pos, x1, cw, y3):
    T = x1.shape[0]
    steps = T // CMB_TC
    return pl.pallas_call(
        _combine_kernel,
        grid=(steps,),
        in_specs=[
            pl.BlockSpec((TOP_K, CMB_TC), lambda i: (0, i), memory_space=pltpu.SMEM),
            pl.BlockSpec((TOP_K, CMB_TC), lambda i: (0, jnp.minimum(i + 1, steps - 1)),
                         memory_space=pltpu.SMEM),
            pl.BlockSpec((CMB_TC, D_MODEL), lambda i: (i, 0)),
            pl.BlockSpec((SUBLANES, CMB_TC), lambda i: (0, i)),
            pl.BlockSpec(memory_space=pl.ANY),
        ],
        out_specs=pl.BlockSpec((CMB_TC, D_MODEL), lambda i: (i, 0)),
        out_shape=jax.ShapeDtypeStruct((T, D_MODEL), F32),
        scratch_shapes=[
            pltpu.VMEM((2, TOP_K, CMB_TC * ROW_SUB, LANES), F32),
            pltpu.SemaphoreType.DMA((2,)),
        ],
        compiler_params=_cparams(("arbitrary",)),
        name="combine",
    )(pos, pos, x1, cw, y3)


def _rotary_lane_tables(seq):
    pos = jnp.arange(seq, dtype=F32)
    inv_freq = 1.0 / (jnp.float32(ROPE_THETA) ** (jnp.arange(0, ROT_DIM, 2, dtype=F32) / ROT_DIM))
    ang = pos[:, None] * inv_freq[None, :]
    cos, sin = jnp.cos(ang), jnp.sin(ang)
    half = ROT_DIM // 2
    lane = jnp.arange(LANES) % QK_DIM
    fidx = lane % half
    first = (lane < half)[None, :]
    second = ((lane >= half) & (lane < ROT_DIM))[None, :]
    cos_t = jnp.where(first | second, cos[:, fidx], 1.0)
    s_up = jnp.where(first, -sin[:, fidx], 0.0)
    s_dn = jnp.where(second, sin[:, fidx], 0.0)
    return cos_t, s_up, s_dn


def kernel(x, attn_norm_g, w_in, gmlp_ln_g, gmlp_ln_b, gmlp_ws, gmlp_bs, q_norm_g, k_norm_g,
           lambda_q1, lambda_k1, lambda_q2, lambda_k2, subln_g, w_out, ffn_norm_g,
           w_group, b_group, w_router, b_router, w_gate, w_up, w_down):
    bsz, seq, d = x.shape
    T = bsz * seq
    assert d == D_MODEL and seq % IN_TM == 0 and seq % ATT_TQ == 0 and T % RT_TN == 0
    l = 0
    x2 = x.reshape(T, d)

    params = jnp.zeros((8, IN_WIDTH), F32)
    params = params.at[0, A_WIDTH:2 * A_WIDTH].set(gmlp_ln_g[l].reshape(-1))
    params = params.at[1, A_WIDTH:2 * A_WIDTH].set(gmlp_ln_b[l].reshape(-1))
    params = params.at[0, 2 * A_WIDTH:3 * A_WIDTH].set(jnp.tile(q_norm_g[l], 2 * B_HEADS))
    params = params.at[0, 3 * A_WIDTH:4 * A_WIDTH].set(jnp.tile(k_norm_g[l], 2 * B_HEADS))
    blk = jnp.arange(IN_GRP) // QK_DIM
    bd = jnp.where(blk[:, None] == blk[None, :], 1.0 / QK_DIM, 0.0).astype(BF16)
    cos_t, s_up, s_dn = _rotary_lane_tables(seq)
    bs_full = jnp.repeat(jnp.transpose(gmlp_bs[l]), HEAD, axis=1)
    lam = (jnp.exp(jnp.sum(lambda_q1[l] * lambda_k1[l])) - jnp.exp(jnp.sum(lambda_q2[l] * lambda_k2[l]))
           + LAMBDA_INIT).reshape(1).astype(F32)
    w_rt = jnp.zeros((LG_ROWS, D_MODEL), F32)
    w_rt = w_rt.at[:N_GROUPS].set(w_group[l].T).at[SUBLANES:].set(w_router[l].T)
    b_rt = jnp.zeros((LG_ROWS, 1), F32)
    b_rt = b_rt.at[:N_GROUPS, 0].set(b_group[l]).at[SUBLANES:, 0].set(b_router[l])
    tri = (jnp.arange(RT_TN)[:, None] < jnp.arange(RT_TN)[None, :]).astype(BF16)

    zp = _in_proj(x2, attn_norm_g[l].reshape(1, d), w_in[l].astype(BF16), params, bd,
                  cos_t, s_up, s_dn, seq)
    out_a = _gmlp(zp, gmlp_ws[l].astype(BF16), bs_full)
    att_steps = bsz * B_HEADS * (seq // ATT_TQ)
    assert (N_EXPERTS * D_MODEL) % att_steps == 0 and (N_EXPERTS * D_EXPERT) % att_steps == 0
    out_b, wg_bf, wu_bf, wd_bf = _attention(
        lam, zp.reshape(bsz, seq, IN_WIDTH), subln_g[l].reshape(1, HEAD),
        w_gate[l].reshape(N_EXPERTS * D_MODEL, D_EXPERT), w_up[l].reshape(N_EXPERTS * D_MODEL, D_EXPERT),
        w_down[l].reshape(N_EXPERTS * D_EXPERT, D_MODEL))
    x1, hm3, lg_t = _out_proj(out_a, out_b.reshape(T, A_WIDTH), x2, w_out[l].astype(BF16),
                              ffn_norm_g[l].reshape(1, d), w_rt.astype(BF16), b_rt)

    idr, cw, cnt = _route(lg_t, tri)
    counts = cnt[:, 0]
    padded = (counts + EXP_BLK - 1) // EXP_BLK * EXP_BLK
    pend = jnp.cumsum(padded)
    pstart = (pend - padded).astype(I32)
    n_blocks = (T * TOP_K) // EXP_BLK + N_EXPERTS
    n_rows = n_blocks * EXP_BLK
    block_e = jnp.minimum(jnp.sum(pend[None, :] <= (jnp.arange(n_blocks) * EXP_BLK)[:, None], axis=1),
                          N_EXPERTS - 1).astype(I32)
    n_used = (pend[-1] // EXP_BLK).reshape(1).astype(I32)
    trail = pend[-1] + jnp.arange(N_EXPERTS) * EXP_BLK
    zt = jnp.concatenate([jnp.where(counts > 0, pend - EXP_BLK, -1),
                          jnp.where(trail < n_rows, trail, -1)]).astype(I32)

    pos = jnp.take(pstart, idr[0:TOP_K], axis=0) + idr[TOP_K:2 * TOP_K]
    xs3 = _dispatch(zt, pos, hm3, n_rows)
    y3 = _experts(block_e, n_used, xs3, wg_bf.reshape(N_EXPERTS, D_MODEL, D_EXPERT),
                  wu_bf.reshape(N_EXPERTS, D_MODEL, D_EXPERT), wd_bf.reshape(N_EXPERTS, D_EXPERT, D_MODEL))
    out = _combine(pos, x1, cw, y3)
    return out.reshape(bsz, seq, d)
```

```python
import math

import jax
import jax.numpy as jnp
from jax import lax
from jax.experimental import pallas as pl
from jax.experimental.pallas import tpu as pltpu

F32 = jnp.float32
BF16 = jnp.bfloat16
I32 = jnp.int32

D_MODEL = 2048
A_WIDTH = 1024
A_HEADS = 8
HEAD = 128
CHUNK = 128
B_HEADS = 8
QK_DIM = 64
ROT_DIM = 16
ROPE_THETA = 500000.0
IN_WIDTH = 5120
N_GROUPS = 4
EPG = 8
N_EXPERTS = 32
TOP_K = 2
D_EXPERT = 1024
EPS = 1e-6
LAMBDA_INIT = 0.8 - 0.6 * math.exp(-0.3 * 0)

LANES = 128
SUBLANES = 8
ROW_SUB = D_MODEL // LANES
LG_ROWS = SUBLANES * (1 + N_GROUPS)

IN_TM = 512
IN_TN = 1024
IN_GRP = 256
IN_ROWS = 64
GM_TM = 512
ATT_TQ = 512
ATT_TK = 512
ONES_ROWS = 16
OUT_TM = 256
RT_TN = 512
EXP_BLK = 256
DSP_TC = 512
CMB_TC = 256
DMA_UNROLL = 8
VMEM_LIMIT = 48 * 1024 * 1024


def _cparams(sem):
    return pltpu.CompilerParams(dimension_semantics=sem, vmem_limit_bytes=VMEM_LIMIT)


def _to_token_rows(x):
    rows = x.shape[0]
    slabs = jnp.stack([x[:, s * LANES:(s + 1) * LANES] for s in range(ROW_SUB)], axis=0)
    return pltpu.einshape("stl->tsl", slabs).reshape(rows * ROW_SUB, LANES)


def _from_token_rows(x3):
    rows = x3.shape[0] // ROW_SUB
    return pltpu.einshape("tsl->stl", x3.reshape(rows, ROW_SUB, LANES))


def _in_proj_kernel(x_ref, g_ref, w_ref, p_ref, bd_ref, c_ref, s1_ref, s2_ref, o_ref, hn_ref, z_ref):
    j = pl.program_id(1)
    tm = hn_ref.shape[0]
    ngrp = IN_TN // IN_GRP

    @pl.when(j == 0)
    def _():
        x = x_ref[...]
        ms = jnp.mean(x * x, axis=-1, keepdims=True)
        hn_ref[...] = (x * lax.rsqrt(ms + EPS) * g_ref[...]).astype(BF16)

    def project(slot):
        z_ref[slot] = jnp.dot(hn_ref[...], w_ref[...], preferred_element_type=F32)

    def gelu_section(slot):
        for r in range(tm // IN_ROWS):
            rows = slice(r * IN_ROWS, (r + 1) * IN_ROWS)
            o_ref[rows, :] = jax.nn.gelu(z_ref[slot, rows, :]).astype(BF16)

    def gelu_ln_section(slot):
        for s in range(IN_TN // LANES):
            sl = slice(s * LANES, (s + 1) * LANES)
            gz = jax.nn.gelu(z_ref[slot, :, sl])
            mu = jnp.mean(gz, axis=-1, keepdims=True)
            xc = gz - mu
            var = jnp.mean(xc * xc, axis=-1, keepdims=True)
            o_ref[:, sl] = (xc * lax.rsqrt(var + EPS) * p_ref[0:1, sl] + p_ref[1:2, sl]).astype(BF16)

    def qk_section(slot, scale):
        sq = jnp.concatenate([z_ref[slot, :, g * IN_GRP:(g + 1) * IN_GRP] for g in range(ngrp)], axis=0)
        ms = jnp.dot((sq * sq).astype(BF16), bd_ref[...], preferred_element_type=F32)
        for s in range(IN_TN // LANES):
            sl = slice(s * LANES, (s + 1) * LANES)
            g, c = divmod(s * LANES, IN_GRP)
            ms_s = ms[g * tm:(g + 1) * tm, c:c + LANES]
            y = z_ref[slot, :, sl] * lax.rsqrt(ms_s + EPS) * p_ref[0:1, sl]
            r = (y * c_ref[...] + pltpu.roll(y, LANES - ROT_DIM // 2, 1) * s1_ref[...]
                 + pltpu.roll(y, ROT_DIM // 2, 1) * s2_ref[...])
            o_ref[:, sl] = (r * scale if scale != 1.0 else r).astype(BF16)

    @pl.when(j == 0)
    def _():
        project(0)

    @pl.when(j == 1)
    def _():
        gelu_section(0)
        project(1)

    @pl.when(j == 2)
    def _():
        gelu_ln_section(1)
        project(0)

    @pl.when(j == 3)
    def _():
        qk_section(0, math.log2(math.e) / math.sqrt(QK_DIM))
        project(1)

    @pl.when(j == 4)
    def _():
        qk_section(1, 1.0)
        project(0)

    @pl.when(j == 5)
    def _():
        o_ref[...] = z_ref[0].astype(BF16)


def _in_proj(x2, g, w_bf, params, bd, cos_t, sup_t, sdn_t, seq):
    T = x2.shape[0]
    spt = seq // IN_TM
    nsec = IN_WIDTH // IN_TN
    assert nsec == 5
    return pl.pallas_call(
        _in_proj_kernel,
        grid=(T // IN_TM, nsec + 1),
        in_specs=[
            pl.BlockSpec((IN_TM, D_MODEL), lambda i, j: (i, 0)),
            pl.BlockSpec((1, D_MODEL), lambda i, j: (0, 0)),
            pl.BlockSpec((D_MODEL, IN_TN), lambda i, j: (0, jnp.minimum(j, nsec - 1))),
            pl.BlockSpec((8, IN_TN), lambda i, j: (0, jnp.maximum(j - 1, 0))),
            pl.BlockSpec((IN_GRP, IN_GRP), lambda i, j: (0, 0)),
            pl.BlockSpec((IN_TM, LANES), lambda i, j: (i % spt, 0)),
            pl.BlockSpec((IN_TM, LANES), lambda i, j: (i % spt, 0)),
            pl.BlockSpec((IN_TM, LANES), lambda i, j: (i % spt, 0)),
        ],
        out_specs=pl.BlockSpec((IN_TM, IN_TN), lambda i, j: (i, jnp.maximum(j - 1, 0))),
        out_shape=jax.ShapeDtypeStruct((T, IN_WIDTH), BF16),
        scratch_shapes=[pltpu.VMEM((IN_TM, D_MODEL), BF16), pltpu.VMEM((2, IN_TM, IN_TN), F32)],
        compiler_params=_cparams(("parallel", "arbitrary")),
        name="in_proj",
    )(x2, g, w_bf, params, bd, cos_t, sup_t, sdn_t)


def _gmlp_kernel(u_ref, v_ref, ws_ref, bs_ref, o_ref):
    for c in range(GM_TM // CHUNK):
        rows = slice(c * CHUNK, (c + 1) * CHUNK)
        for h in range(A_HEADS):
            cols = slice(h * HEAD, (h + 1) * HEAD)
            s = jnp.dot(ws_ref[h], v_ref[rows, cols], preferred_element_type=F32) + bs_ref[:, cols]
            o_ref[rows, cols] = (u_ref[rows, cols].astype(F32) * s).astype(BF16)


def _gmlp(zp, ws_bf, bs_full):
    T = zp.shape[0]
    return pl.pallas_call(
        _gmlp_kernel,
        grid=(T // GM_TM,),
        in_specs=[
            pl.BlockSpec((GM_TM, A_WIDTH), lambda i: (i, 0)),
            pl.BlockSpec((GM_TM, A_WIDTH), lambda i: (i, 1)),
            pl.BlockSpec((A_HEADS, CHUNK, CHUNK), lambda i: (0, 0, 0)),
            pl.BlockSpec((CHUNK, A_WIDTH), lambda i: (0, 0)),
        ],
        out_specs=pl.BlockSpec((GM_TM, A_WIDTH), lambda i: (i, 0)),
        out_shape=jax.ShapeDtypeStruct((T, A_WIDTH), BF16),
        compiler_params=_cparams(("parallel",)),
        name="gmlp",
    )(zp, zp, ws_bf, bs_full)


def _attn_kernel(lam_ref, q_ref, k_ref, v_ref, g_ref, wg_ref, wu_ref, wd_ref,
                 o_ref, wgb_ref, wub_ref, wdb_ref, vt_ref):
    seq = k_ref.shape[0]
    nc = seq // ATT_TK

    wgb_ref[...] = wg_ref[...].astype(BF16)
    wub_ref[...] = wu_ref[...].astype(BF16)
    wdb_ref[...] = wd_ref[...].astype(BF16)

    @pl.when(pl.program_id(2) == 0)
    def _():
        for c in range(nc):
            cols = slice(c * ATT_TK, (c + 1) * ATT_TK)
            vt_ref[0:HEAD, cols] = v_ref[cols, :].astype(F32).T.astype(BF16)
        vt_ref[HEAD:, :] = jnp.ones((ONES_ROWS, seq), BF16)

    q_t = q_ref[...].astype(F32).T
    row = lax.broadcasted_iota(I32, q_t.shape, 0)
    q1_t = jnp.where(row < QK_DIM, q_t, 0.0).astype(BF16)
    q2_t = jnp.where(row >= QK_DIM, q_t, 0.0).astype(BF16)

    def scores(c, qh_t):
        return jnp.dot(k_ref[c * ATT_TK:(c + 1) * ATT_TK, :], qh_t, preferred_element_type=F32)

    def update(c, s_t, m, acc):
        m_new = jnp.maximum(m, jnp.max(s_t, axis=0, keepdims=True))
        p = jnp.exp2(s_t - m_new).astype(BF16)
        pv = jnp.dot(vt_ref[:, c * ATT_TK:(c + 1) * ATT_TK], p, preferred_element_type=F32)
        return m_new, jnp.exp2(m - m_new) * acc + pv

    tq = q_t.shape[1]
    m1 = m2 = jnp.full((1, tq), -jnp.inf, F32)
    a1 = a2 = jnp.zeros((HEAD + ONES_ROWS, tq), F32)
    s1, s2 = scores(0, q1_t), scores(0, q2_t)
    for c in range(nc):
        if c + 1 < nc:
            n1, n2 = scores(c + 1, q1_t), scores(c + 1, q2_t)
        m1, a1 = update(c, s1, m1, a1)
        m2, a2 = update(c, s2, m2, a2)
        if c + 1 < nc:
            s1, s2 = n1, n2
    o_t = (a1[0:HEAD] / a1[HEAD:HEAD + 1]
           - lam_ref[0] * (a2[0:HEAD] / a2[HEAD:HEAD + 1]))
    o = o_t.T
    ms = jnp.mean(o * o, axis=-1, keepdims=True)
    o_ref[...] = (o * lax.rsqrt(ms + EPS) * g_ref[...] * (1.0 - LAMBDA_INIT)).astype(BF16)


def _attention(lam, zp3, subln_g, wg2, wu2, wd2):
    bsz, seq, _ = zp3.shape
    nq = seq // ATT_TQ
    steps = bsz * B_HEADS * nq
    qb, kb, vb = (2 * A_WIDTH) // HEAD, (2 * A_WIDTH + 1024) // HEAD, (2 * A_WIDTH + 2048) // HEAD

    def w_spec(w):
        rows = w.shape[0] // steps
        return pl.BlockSpec((rows, w.shape[1]), lambda b, h, i, lam: ((b * B_HEADS + h) * nq + i, 0))

    w_specs = [w_spec(wg2), w_spec(wu2), w_spec(wd2)]
    grid_spec = pltpu.PrefetchScalarGridSpec(
        num_scalar_prefetch=1,
        grid=(bsz, B_HEADS, nq),
        in_specs=[
            pl.BlockSpec((None, ATT_TQ, HEAD), lambda b, h, i, lam: (b, i, qb + h)),
            pl.BlockSpec((None, seq, HEAD), lambda b, h, i, lam: (b, 0, kb + h)),
            pl.BlockSpec((None, seq, HEAD), lambda b, h, i, lam: (b, 0, vb + h)),
            pl.BlockSpec((1, HEAD), lambda b, h, i, lam: (0, 0)),
        ] + w_specs,
        out_specs=[pl.BlockSpec((None, ATT_TQ, HEAD), lambda b, h, i, lam: (b, i, h))] + w_specs,
        scratch_shapes=[pltpu.VMEM((HEAD + ONES_ROWS, seq), BF16)],
    )
    return pl.pallas_call(
        _attn_kernel,
        grid_spec=grid_spec,
        out_shape=[jax.ShapeDtypeStruct((bsz, seq, B_HEADS * HEAD), BF16)]
                  + [jax.ShapeDtypeStruct(w.shape, BF16) for w in (wg2, wu2, wd2)],
        compiler_params=_cparams(("parallel", "parallel", "arbitrary")),
        name="attn",
    )(lam, zp3, zp3, zp3, subln_g, wg2, wu2, wd2)


def _out_proj_kernel(a_ref, b_ref, x_ref, wa_ref, wb_ref, g_ref, wr_ref, br_ref, x1_ref, hm_ref, lg_ref):
    acc = (jnp.dot(a_ref[...], wa_ref[...], preferred_element_type=F32)
           + jnp.dot(b_ref[...], wb_ref[...], preferred_element_type=F32))
    x1 = x_ref[...] + acc
    x1_ref[...] = x1
    ms = jnp.mean(x1 * x1, axis=-1, keepdims=True)
    hm = x1 * lax.rsqrt(ms + EPS) * g_ref[...]
    hm_ref[...] = _to_token_rows(hm)
    lg_ref[...] = lax.dot_general(wr_ref[...], hm.astype(BF16), (((1,), (1,)), ((), ())),
                                  preferred_element_type=F32) + br_ref[...]


def _out_proj(out_a, out_b, x2, w_out_bf, g, w_rt, b_rt):
    T = x2.shape[0]
    return pl.pallas_call(
        _out_proj_kernel,
        grid=(T // OUT_TM,),
        in_specs=[
            pl.BlockSpec((OUT_TM, A_WIDTH), lambda i: (i, 0)),
            pl.BlockSpec((OUT_TM, A_WIDTH), lambda i: (i, 0)),
            pl.BlockSpec((OUT_TM, D_MODEL), lambda i: (i, 0)),
            pl.BlockSpec((A_WIDTH, D_MODEL), lambda i: (0, 0)),
            pl.BlockSpec((A_WIDTH, D_MODEL), lambda i: (1, 0)),
            pl.BlockSpec((1, D_MODEL), lambda i: (0, 0)),
            pl.BlockSpec((LG_ROWS, D_MODEL), lambda i: (0, 0)),
            pl.BlockSpec((LG_ROWS, 1), lambda i: (0, 0)),
        ],
        out_specs=[
            pl.BlockSpec((OUT_TM, D_MODEL), lambda i: (i, 0)),
            pl.BlockSpec((OUT_TM * ROW_SUB, LANES), lambda i: (i, 0)),
            pl.BlockSpec((LG_ROWS, OUT_TM), lambda i: (0, i)),
        ],
        out_shape=[
            jax.ShapeDtypeStruct((T, D_MODEL), F32),
            jax.ShapeDtypeStruct((T * ROW_SUB, LANES), F32),
            jax.ShapeDtypeStruct((LG_ROWS, T), F32),
        ],
        compiler_params=_cparams(("parallel",)),
        name="out_proj",
    )(out_a, out_b, x2, w_out_bf, w_out_bf, g, w_rt, b_rt)


def _route_kernel(lg_ref, tri_ref, idr_ref, cw_ref, cnt_ref, carry_ref):
    i = pl.program_id(0)

    @pl.when(i == 0)
    def _():
        carry_ref[...] = jnp.zeros_like(carry_ref)

    tn = lg_ref.shape[1]
    row = lax.broadcasted_iota(I32, (SUBLANES, tn), 0)
    neg = jnp.float32(-jnp.inf)

    def top1(v):
        mx = jnp.max(v, axis=0, keepdims=True)
        idx = jnp.min(jnp.where(v == mx, row, SUBLANES), axis=0, keepdims=True)
        return mx, idx

    g = jnp.where(row < N_GROUPS, lg_ref[0:SUBLANES, :], neg)
    gmax, gidx = top1(g)
    g_w = 1.0 / jnp.sum(jnp.exp(g - gmax), axis=0, keepdims=True)
    sel = lg_ref[SUBLANES * N_GROUPS:SUBLANES * (N_GROUPS + 1), :]
    for grp in range(N_GROUPS - 2, -1, -1):
        sel = jnp.where(gidx == grp, lg_ref[SUBLANES * (grp + 1):SUBLANES * (grp + 2), :], sel)
    e1, i1 = top1(sel)
    e2, i2 = top1(jnp.where(row == i1, neg, sel))
    d = jnp.exp(e2 - e1)
    w1 = 1.0 / (1.0 + d)
    w2 = d * w1
    id0 = gidx * EPG + i1
    id1 = gidx * EPG + i2

    erow = lax.broadcasted_iota(I32, (N_EXPERTS, tn), 0)
    o0 = erow == id0
    o1 = erow == id1
    occ = jnp.where(o0 | o1, 1.0, 0.0).astype(F32)
    before = jnp.dot(occ.astype(BF16), tri_ref[...], preferred_element_type=F32) + carry_ref[:, 0:1]
    r0 = jnp.sum(jnp.where(o0, before, 0.0), axis=0, keepdims=True)
    r1 = jnp.sum(jnp.where(o1, before, 0.0), axis=0, keepdims=True)
    total = carry_ref[...] + jnp.sum(occ, axis=1, keepdims=True)
    carry_ref[...] = total
    cnt_ref[...] = total.astype(I32)

    zi = jnp.zeros((SUBLANES - 4, tn), I32)
    idr_ref[...] = jnp.concatenate([id0, id1, r0.astype(I32), r1.astype(I32), zi], axis=0)
    zf = jnp.zeros((SUBLANES - 2, tn), F32)
    cw_ref[...] = jnp.concatenate([g_w * w1, g_w * w2, zf], axis=0)


def _route(lg_t, tri):
    T = lg_t.shape[1]
    return pl.pallas_call(
        _route_kernel,
        grid=(T // RT_TN,),
        in_specs=[
            pl.BlockSpec((LG_ROWS, RT_TN), lambda i: (0, i)),
            pl.BlockSpec((RT_TN, RT_TN), lambda i: (0, 0)),
        ],
        out_specs=[
            pl.BlockSpec((SUBLANES, RT_TN), lambda i: (0, i)),
            pl.BlockSpec((SUBLANES, RT_TN), lambda i: (0, i)),
            pl.BlockSpec((N_EXPERTS, LANES), lambda i: (0, 0)),
        ],
        out_shape=[
            jax.ShapeDtypeStruct((SUBLANES, T), I32),
            jax.ShapeDtypeStruct((SUBLANES, T), F32),
            jax.ShapeDtypeStruct((N_EXPERTS, LANES), I32),
        ],
        scratch_shapes=[pltpu.VMEM((N_EXPERTS, LANES), F32)],
        compiler_params=_cparams(("arbitrary",)),
        name="route",
    )(lg_t, tri)


def _row_copy(src_ref, src_row, dst_ref, dst_row, sem):
    return pltpu.make_async_copy(src_ref.at[pl.ds(src_row * ROW_SUB, ROW_SUB)],
                                 dst_ref.at[pl.ds(dst_row * ROW_SUB, ROW_SUB)], sem)


def _dispatch_kernel(zt_ref, pos_ref, hm_ref, xs_ref, zbuf, sem):
    i = pl.program_id(0)

    def zero_copy(e):
        return pltpu.make_async_copy(
            zbuf, xs_ref.at[pl.ds(pl.multiple_of(zt_ref[e] * ROW_SUB, ROW_SUB), EXP_BLK * ROW_SUB)], sem)

    @pl.when(i == 0)
    def _():
        zbuf[...] = jnp.zeros_like(zbuf)
        for e in range(2 * N_EXPERTS):
            @pl.when(zt_ref[e] >= 0)
            def _():
                zero_copy(e).start()
        for e in range(2 * N_EXPERTS):
            @pl.when(zt_ref[e] >= 0)
            def _():
                zero_copy(e).wait()

    def issue(grp, carry):
        for u in range(DMA_UNROLL):
            t = grp * DMA_UNROLL + u
            for k in range(TOP_K):
                _row_copy(hm_ref, t, xs_ref, pos_ref[k, t], sem).start()
        return carry

    lax.fori_loop(0, DSP_TC // DMA_UNROLL, issue, 0)

    def drain(grp, carry):
        for _ in range(DMA_UNROLL * TOP_K):
            _row_copy(hm_ref, 0, xs_ref, 0, sem).wait()
        return carry

    lax.fori_loop(0, DSP_TC // DMA_UNROLL, drain, 0)


def _dispatch(zt, pos, hm3, n_rows):
    T = pos.shape[1]
    grid_spec = pltpu.PrefetchScalarGridSpec(
        num_scalar_prefetch=1,
        grid=(T // DSP_TC,),
        in_specs=[
            pl.BlockSpec((TOP_K, DSP_TC), lambda i, zt: (0, i), memory_space=pltpu.SMEM),
            pl.BlockSpec((DSP_TC * ROW_SUB, LANES), lambda i, zt: (i, 0)),
        ],
        out_specs=pl.BlockSpec(memory_space=pl.ANY),
        scratch_shapes=[pltpu.VMEM((EXP_BLK * ROW_SUB, LANES), F32), pltpu.SemaphoreType.DMA(())],
    )
    return pl.pallas_call(
        _dispatch_kernel,
        grid_spec=grid_spec,
        out_shape=jax.ShapeDtypeStruct((n_rows * ROW_SUB, LANES), F32),
        compiler_params=_cparams(("arbitrary",)),
        name="dispatch",
    )(zt, pos, hm3)


def _expert_kernel(be_ref, nu_ref, xs_ref, wg_ref, wu_ref, wd_ref, y_ref, xb_ref):
    b = pl.program_id(0)

    @pl.when(b < nu_ref[0])
    def _():
        slabs = _from_token_rows(xs_ref[...])
        for s in range(ROW_SUB):
            xb_ref[:, s * LANES:(s + 1) * LANES] = slabs[s].astype(BF16)
        x = xb_ref[...]
        g = jnp.dot(x, wg_ref[...], preferred_element_type=F32)
        u = jnp.dot(x, wu_ref[...], preferred_element_type=F32)
        a = (jax.nn.silu(g) * u).astype(BF16)
        y = jnp.dot(a, wd_ref[...], preferred_element_type=F32)
        y_ref[...] = _to_token_rows(y)

    @pl.when(b >= nu_ref[0])
    def _():
        y_ref[...] = jnp.zeros_like(y_ref)


def _experts(block_e, n_used, xs3, wg, wu, wd):
    n_blocks = xs3.shape[0] // (EXP_BLK * ROW_SUB)

    def xs_map(b, be, nu):
        return (jnp.minimum(b, nu[0] - 1), 0)

    def w_map(b, be, nu):
        return (be[b], 0, 0)

    grid_spec = pltpu.PrefetchScalarGridSpec(
        num_scalar_prefetch=2,
        grid=(n_blocks,),
        in_specs=[
            pl.BlockSpec((EXP_BLK * ROW_SUB, LANES), xs_map),
            pl.BlockSpec((None, D_MODEL, D_EXPERT), w_map),
            pl.BlockSpec((None, D_MODEL, D_EXPERT), w_map),
            pl.BlockSpec((None, D_EXPERT, D_MODEL), w_map),
        ],
        out_specs=pl.BlockSpec((EXP_BLK * ROW_SUB, LANES), lambda b, be, nu: (b, 0)),
        scratch_shapes=[pltpu.VMEM((EXP_BLK, D_MODEL), BF16)],
    )
    return pl.pallas_call(
        _expert_kernel,
        grid_spec=grid_spec,
        out_shape=jax.ShapeDtypeStruct(xs3.shape, F32),
        compiler_params=_cparams(("arbitrary",)),
        name="experts",
    )(block_e, n_used, xs3, wg, wu, wd)


def _combine_kernel(pos_ref, posn_ref, x1_ref, cw_ref, y_ref, o_ref, yb, sems):
    i = pl.program_id(0)
    slot = i % 2

    def gather(ids_ref, slot_):
        def issue(grp, carry):
            for u in range(DMA_UNROLL):
                t = grp * DMA_UNROLL + u
                for k in range(TOP_K):
                    _row_copy(y_ref, ids_ref[k, t], yb.at[slot_, k], t, sems.at[slot_]).start()
            return carry

        lax.fori_loop(0, CMB_TC // DMA_UNROLL, issue, 0)

    @pl.when(i == 0)
    def _():
        gather(pos_ref, 0)

    @pl.when(i + 1 < pl.num_programs(0))
    def _():
        gather(posn_ref, 1 - slot)

    def drain(grp, carry):
        for _ in range(DMA_UNROLL * TOP_K):
            _row_copy(y_ref, 0, yb.at[slot, 0], 0, sems.at[slot]).wait()
        return carry

    lax.fori_loop(0, CMB_TC // DMA_UNROLL, drain, 0)
    w_t = cw_ref[...].T
    w0 = w_t[:, 0:1]
    w1 = w_t[:, 1:2]
    y0 = _from_token_rows(yb[slot, 0])
    y1 = _from_token_rows(yb[slot, 1])
    for s in range(ROW_SUB):
        sl = slice(s * LANES, (s + 1) * LANES)
        o_ref[:, sl] = x1_ref[:, sl] + w0 * y0[s] + w1 * y1[s]


def _combine(pos, x1, cw, y3):
    T = x1.shape[0]
    steps = T // CMB_TC
    return pl.pallas_call(
        _combine_kernel,
        grid=(steps,),
        in_specs=[
            pl.BlockSpec((TOP_K, CMB_TC), lambda i: (0, i), memory_space=pltpu.SMEM),
            pl.BlockSpec((TOP_K, CMB_TC), lambda i: (0, jnp.minimum(i + 1, steps - 1)),
                         memory_space=pltpu.SMEM),
            pl.BlockSpec((CMB_TC, D_MODEL), lambda i: (i, 0)),
            pl.BlockSpec((SUBLANES, CMB_TC), lambda i: (0, i)),
            pl.BlockSpec(memory_space=pl.ANY),
        ],
        out_specs=pl.BlockSpec((CMB_TC, D_MODEL), lambda i: (i, 0)),
        out_shape=jax.ShapeDtypeStruct((T, D_MODEL), F32),
        scratch_shapes=[
            pltpu.VMEM((2, TOP_K, CMB_TC * ROW_SUB, LANES), F32),
            pltpu.SemaphoreType.DMA((2,)),
        ],
        compiler_params=_cparams(("arbitrary",)),
        name="combine",
    )(pos, pos, x1, cw, y3)


def _rotary_lane_tables(seq):
    pos = jnp.arange(seq, dtype=F32)
    inv_freq = 1.0 / (jnp.float32(ROPE_THETA) ** (jnp.arange(0, ROT_DIM, 2, dtype=F32) / ROT_DIM))
    ang = pos[:, None] * inv_freq[None, :]
    cos, sin = jnp.cos(ang), jnp.sin(ang)
    half = ROT_DIM // 2
    lane = jnp.arange(LANES) % QK_DIM
    fidx = lane % half
    first = (lane < half)[None, :]
    second = ((lane >= half) & (lane < ROT_DIM))[None, :]
    cos_t = jnp.where(first | second, cos[:, fidx], 1.0)
    s_up = jnp.where(first, -sin[:, fidx], 0.0)
    s_dn = jnp.where(second, sin[:, fidx], 0.0)
    return cos_t, s_up, s_dn


def kernel(x, attn_norm_g, w_in, gmlp_ln_g, gmlp_ln_b, gmlp_ws, gmlp_bs, q_norm_g, k_norm_g,
           lambda_q1, lambda_k1, lambda_q2, lambda_k2, subln_g, w_out, ffn_norm_g,
           w_group, b_group, w_router, b_router, w_gate, w_up, w_down):
    bsz, seq, d = x.shape
    T = bsz * seq
    assert d == D_MODEL and seq % IN_TM == 0 and seq % ATT_TQ == 0 and T % RT_TN == 0
    l = 0
    x2 = x.reshape(T, d)

    params = jnp.zeros((8, IN_WIDTH), F32)
    params = params.at[0, A_WIDTH:2 * A_WIDTH].set(gmlp_ln_g[l].reshape(-1))
    params = params.at[1, A_WIDTH:2 * A_WIDTH].set(gmlp_ln_b[l].reshape(-1))
    params = params.at[0, 2 * A_WIDTH:3 * A_WIDTH].set(jnp.tile(q_norm_g[l], 2 * B_HEADS))
    params = params.at[0, 3 * A_WIDTH:4 * A_WIDTH].set(jnp.tile(k_norm_g[l], 2 * B_HEADS))
    blk = jnp.arange(IN_GRP) // QK_DIM
    bd = jnp.where(blk[:, None] == blk[None, :], 1.0 / QK_DIM, 0.0).astype(BF16)
    cos_t, s_up, s_dn = _rotary_lane_tables(seq)
    bs_full = jnp.repeat(jnp.transpose(gmlp_bs[l]), HEAD, axis=1)
    lam = (jnp.exp(jnp.sum(lambda_q1[l] * lambda_k1[l])) - jnp.exp(jnp.sum(lambda_q2[l] * lambda_k2[l]))
           + LAMBDA_INIT).reshape(1).astype(F32)
    w_rt = jnp.zeros((LG_ROWS, D_MODEL), F32)
    w_rt = w_rt.at[:N_GROUPS].set(w_group[l].T).at[SUBLANES:].set(w_router[l].T)
    b_rt = jnp.zeros((LG_ROWS, 1), F32)
    b_rt = b_rt.at[:N_GROUPS, 0].set(b_group[l]).at[SUBLANES:, 0].set(b_router[l])
    tri = (jnp.arange(RT_TN)[:, None] < jnp.arange(RT_TN)[None, :]).astype(BF16)

    zp = _in_proj(x2, attn_norm_g[l].reshape(1, d), w_in[l].astype(BF16), params, bd,
                  cos_t, s_up, s_dn, seq)
    out_a = _gmlp(zp, gmlp_ws[l].astype(BF16), bs_full)
    att_steps = bsz * B_HEADS * (seq // ATT_TQ)
    assert (N_EXPERTS * D_MODEL) % att_steps == 0 and (N_EXPERTS * D_EXPERT) % att_steps == 0
    out_b, wg_bf, wu_bf, wd_bf = _attention(
        lam, zp.reshape(bsz, seq, IN_WIDTH), subln_g[l].reshape(1, HEAD),
        w_gate[l].reshape(N_EXPERTS * D_MODEL, D_EXPERT), w_up[l].reshape(N_EXPERTS * D_MODEL, D_EXPERT),
        w_down[l].reshape(N_EXPERTS * D_EXPERT, D_MODEL))
    x1, hm3, lg_t = _out_proj(out_a, out_b.reshape(T, A_WIDTH), x2, w_out[l].astype(BF16),
                              ffn_norm_g[l].reshape(1, d), w_rt.astype(BF16), b_rt)

    idr, cw, cnt = _route(lg_t, tri)
    counts = cnt[:, 0]
    padded = (counts + EXP_BLK - 1) // EXP_BLK * EXP_BLK
    pend = jnp.cumsum(padded)
    pstart = (pend - padded).astype(I32)
    n_blocks = (T * TOP_K) // EXP_BLK + N_EXPERTS
    n_rows = n_blocks * EXP_BLK
    block_e = jnp.minimum(jnp.sum(pend[None, :] <= (jnp.arange(n_blocks) * EXP_BLK)[:, None], axis=1),
                          N_EXPERTS - 1).astype(I32)
    n_used = (pend[-1] // EXP_BLK).reshape(1).astype(I32)
    trail = pend[-1] + jnp.arange(N_EXPERTS) * EXP_BLK
    zt = jnp.concatenate([jnp.where(counts > 0, pend - EXP_BLK, -1),
                          jnp.where(trail < n_rows, trail, -1)]).astype(I32)

    is_e = idr[0:TOP_K, None, :] == jnp.arange(N_EXPERTS, dtype=I32)[None, :, None]
    pos = jnp.sum(jnp.where(is_e, pstart[None, :, None], 0), axis=1) + idr[TOP_K:2 * TOP_K]
    xs3 = _dispatch(zt, pos, hm3, n_rows)
    y3 = _experts(block_e, n_used, xs3, wg_bf.reshape(N_EXPERTS, D_MODEL, D_EXPERT),
                  wu_bf.reshape(N_EXPERTS, D_MODEL, D_EXPERT), wd_bf.reshape(N_EXPERTS, D_EXPERT, D_MODEL))
    out = _combine(pos, x1, cw, y3)
    return out.reshape(bsz, seq, d)
```

```python
import functools
import math

import jax
import jax.numpy as jnp
from jax import lax
from jax.experimental import pallas as pl
from jax.experimental.pallas import tpu as pltpu

F32 = jnp.float32
BF16 = jnp.bfloat16
I32 = jnp.int32

D_MODEL = 2048
A_WIDTH = 1024
A_HEADS = 8
HEAD = 128
CHUNK = 128
B_HEADS = 8
QK_DIM = 64
ROT_DIM = 16
ROPE_THETA = 500000.0
IN_WIDTH = 5120
N_GROUPS = 4
EPG = 8
N_EXPERTS = 32
TOP_K = 2
D_EXPERT = 1024
EPS = 1e-6
LAMBDA_INIT = 0.8 - 0.6 * math.exp(-0.3 * 0)

LANES = 128
SUBLANES = 8
ROW_SUB = D_MODEL // LANES
LG_ROWS = SUBLANES * (1 + N_GROUPS)

IN_TM = 512
IN_TN = 1024
IN_GRP = 256
IN_ROWS = 64
GM_TM = 512
ATT_TQ = 512
ATT_TK = 512
ONES_ROWS = 16
OUT_TM = 256
RT_TN = 512
EXP_BLK = 256
DSP_TC = 512
CMB_TC = 256
DMA_UNROLL = 8
VMEM_LIMIT = 48 * 1024 * 1024


def _cparams(sem):
    return pltpu.CompilerParams(dimension_semantics=sem, vmem_limit_bytes=VMEM_LIMIT)


def _to_token_rows(x):
    rows = x.shape[0]
    slabs = jnp.stack([x[:, s * LANES:(s + 1) * LANES] for s in range(ROW_SUB)], axis=0)
    return jnp.transpose(slabs, (1, 0, 2)).reshape(rows * ROW_SUB, LANES)


def _from_token_rows(x3):
    rows = x3.shape[0] // ROW_SUB
    return jnp.transpose(x3.reshape(rows, ROW_SUB, LANES), (1, 0, 2))


def _in_proj_kernel(x_ref, g_ref, w_ref, p_ref, bd_ref, c_ref, s1_ref, s2_ref, o_ref, hn_ref, z_ref):
    j = pl.program_id(1)
    tm = hn_ref.shape[0]
    ngrp = IN_TN // IN_GRP

    @pl.when(j == 0)
    def _():
        x = x_ref[...]
        ms = jnp.mean(x * x, axis=-1, keepdims=True)
        hn_ref[...] = (x * lax.rsqrt(ms + EPS) * g_ref[...]).astype(BF16)

    def project(slot):
        z_ref[slot] = jnp.dot(hn_ref[...], w_ref[...], preferred_element_type=F32)

    def gelu_section(slot):
        for r in range(tm // IN_ROWS):
            rows = slice(r * IN_ROWS, (r + 1) * IN_ROWS)
            o_ref[rows, :] = jax.nn.gelu(z_ref[slot, rows, :]).astype(BF16)

    def gelu_ln_section(slot):
        for s in range(IN_TN // LANES):
            sl = slice(s * LANES, (s + 1) * LANES)
            gz = jax.nn.gelu(z_ref[slot, :, sl])
            mu = jnp.mean(gz, axis=-1, keepdims=True)
            xc = gz - mu
            var = jnp.mean(xc * xc, axis=-1, keepdims=True)
            o_ref[:, sl] = (xc * lax.rsqrt(var + EPS) * p_ref[0:1, sl] + p_ref[1:2, sl]).astype(BF16)

    def qk_section(slot, scale):
        sq = jnp.concatenate([z_ref[slot, :, g * IN_GRP:(g + 1) * IN_GRP] for g in range(ngrp)], axis=0)
        ms = jnp.dot((sq * sq).astype(BF16), bd_ref[...], preferred_element_type=F32)
        for s in range(IN_TN // LANES):
            sl = slice(s * LANES, (s + 1) * LANES)
            g, c = divmod(s * LANES, IN_GRP)
            ms_s = ms[g * tm:(g + 1) * tm, c:c + LANES]
            y = z_ref[slot, :, sl] * lax.rsqrt(ms_s + EPS) * p_ref[0:1, sl]
            r = (y * c_ref[...] + pltpu.roll(y, LANES - ROT_DIM // 2, 1) * s1_ref[...]
                 + pltpu.roll(y, ROT_DIM // 2, 1) * s2_ref[...])
            o_ref[:, sl] = (r * scale if scale != 1.0 else r).astype(BF16)

    @pl.when(j == 0)
    def _():
        project(0)

    @pl.when(j == 1)
    def _():
        gelu_section(0)
        project(1)

    @pl.when(j == 2)
    def _():
        gelu_ln_section(1)
        project(0)

    @pl.when(j == 3)
    def _():
        qk_section(0, math.log2(math.e) / math.sqrt(QK_DIM))
        project(1)

    @pl.when(j == 4)
    def _():
        qk_section(1, 1.0)
        project(0)

    @pl.when(j == 5)
    def _():
        o_ref[...] = z_ref[0].astype(BF16)


def _in_proj(x2, g, w_bf, params, bd, cos_t, sup_t, sdn_t, seq):
    T = x2.shape[0]
    spt = seq // IN_TM
    nsec = IN_WIDTH // IN_TN
    assert nsec == 5
    return pl.pallas_call(
        _in_proj_kernel,
        grid=(T // IN_TM, nsec + 1),
        in_specs=[
            pl.BlockSpec((IN_TM, D_MODEL), lambda i, j: (i, 0)),
            pl.BlockSpec((1, D_MODEL), lambda i, j: (0, 0)),
            pl.BlockSpec((None, D_MODEL, IN_TN), lambda i, j: (jnp.minimum(j, nsec - 1), 0, 0)),
            pl.BlockSpec((8, IN_TN), lambda i, j: (0, jnp.maximum(j - 1, 0))),
            pl.BlockSpec((IN_GRP, IN_GRP), lambda i, j: (0, 0)),
            pl.BlockSpec((IN_TM, LANES), lambda i, j: (i % spt, 0)),
            pl.BlockSpec((IN_TM, LANES), lambda i, j: (i % spt, 0)),
            pl.BlockSpec((IN_TM, LANES), lambda i, j: (i % spt, 0)),
        ],
        out_specs=pl.BlockSpec((IN_TM, IN_TN), lambda i, j: (i, jnp.maximum(j - 1, 0))),
        out_shape=jax.ShapeDtypeStruct((T, IN_WIDTH), BF16),
        scratch_shapes=[pltpu.VMEM((IN_TM, D_MODEL), BF16), pltpu.VMEM((2, IN_TM, IN_TN), F32)],
        compiler_params=_cparams(("parallel", "arbitrary")),
        name="in_proj",
    )(x2, g, w_bf, params, bd, cos_t, sup_t, sdn_t)


def _gmlp_kernel(u_ref, v_ref, ws_ref, bs_ref, o_ref):
    for c in range(GM_TM // CHUNK):
        rows = slice(c * CHUNK, (c + 1) * CHUNK)
        for h in range(A_HEADS):
            cols = slice(h * HEAD, (h + 1) * HEAD)
            s = jnp.dot(ws_ref[h], v_ref[rows, cols], preferred_element_type=F32) + bs_ref[:, cols]
            o_ref[rows, cols] = (u_ref[rows, cols].astype(F32) * s).astype(BF16)


def _gmlp(zp, ws_bf, bs_full):
    T = zp.shape[0]
    return pl.pallas_call(
        _gmlp_kernel,
        grid=(T // GM_TM,),
        in_specs=[
            pl.BlockSpec((GM_TM, A_WIDTH), lambda i: (i, 0)),
            pl.BlockSpec((GM_TM, A_WIDTH), lambda i: (i, 1)),
            pl.BlockSpec((A_HEADS, CHUNK, CHUNK), lambda i: (0, 0, 0)),
            pl.BlockSpec((CHUNK, A_WIDTH), lambda i: (0, 0)),
        ],
        out_specs=pl.BlockSpec((GM_TM, A_WIDTH), lambda i: (i, 0)),
        out_shape=jax.ShapeDtypeStruct((T, A_WIDTH), BF16),
        compiler_params=_cparams(("parallel",)),
        name="gmlp",
    )(zp, zp, ws_bf, bs_full)


def _attn_kernel(n_tiles, nq, lam_ref, qc_ref, qn_ref, k_ref, v_ref, g_ref, wg_ref, wu_ref, wd_ref,
                 o_ref, wgb_ref, wub_ref, wdb_ref, vt_ref, qt_ref, acc_ref):
    l = pl.program_id(0)
    slot = l % 2
    seq = k_ref.shape[0]
    nc = seq // ATT_TK
    tq = qc_ref.shape[0]

    def prep_q(q_ref, s):
        q_t = q_ref[...].astype(F32).T
        row = lax.broadcasted_iota(I32, q_t.shape, 0)
        qt_ref[s, 0] = jnp.where(row < QK_DIM, q_t, 0.0).astype(BF16)
        qt_ref[s, 1] = jnp.where(row >= QK_DIM, q_t, 0.0).astype(BF16)

    def finish(s):
        a1, a2 = acc_ref[s, 0], acc_ref[s, 1]
        o_t = (a1[0:HEAD] / a1[HEAD:HEAD + 1]
               - lam_ref[0] * (a2[0:HEAD] / a2[HEAD:HEAD + 1]))
        o = o_t.T
        ms = jnp.mean(o * o, axis=-1, keepdims=True)
        o_ref[...] = (o * lax.rsqrt(ms + EPS) * g_ref[...] * (1.0 - LAMBDA_INIT)).astype(BF16)

    def scores(c, qh_t):
        return jnp.dot(k_ref[c * ATT_TK:(c + 1) * ATT_TK, :], qh_t, preferred_element_type=F32)

    def update(c, s_t, m, acc):
        m_new = jnp.maximum(m, jnp.max(s_t, axis=0, keepdims=True))
        p = jnp.exp2(s_t - m_new).astype(BF16)
        pv = jnp.dot(vt_ref[:, c * ATT_TK:(c + 1) * ATT_TK], p, preferred_element_type=F32)
        return m_new, jnp.exp2(m - m_new) * acc + pv

    def attend(s):
        q1_t, q2_t = qt_ref[s, 0], qt_ref[s, 1]
        m1 = m2 = jnp.full((1, tq), -jnp.inf, F32)
        a1 = a2 = jnp.zeros((HEAD + ONES_ROWS, tq), F32)
        s1, s2 = scores(0, q1_t), scores(0, q2_t)
        for c in range(nc):
            if c + 1 < nc:
                n1, n2 = scores(c + 1, q1_t), scores(c + 1, q2_t)
            m1, a1 = update(c, s1, m1, a1)
            m2, a2 = update(c, s2, m2, a2)
            if c + 1 < nc:
                s1, s2 = n1, n2
        acc_ref[s, 0] = a1
        acc_ref[s, 1] = a2

    @pl.when(l == 0)
    def _():
        prep_q(qc_ref, 0)
        acc_ref[1] = jnp.ones(acc_ref.shape[1:], F32)

    @pl.when((l < n_tiles) & (l % nq == 0))
    def _():
        for c in range(nc):
            cols = slice(c * ATT_TK, (c + 1) * ATT_TK)
            vt_ref[0:HEAD, cols] = v_ref[cols, :].astype(F32).T.astype(BF16)
        vt_ref[HEAD:, :] = jnp.ones((ONES_ROWS, seq), BF16)

    @pl.when(l < n_tiles)
    def _():
        finish(1 - slot)
        prep_q(qn_ref, 1 - slot)
        wgb_ref[...] = wg_ref[...].astype(BF16)
        wub_ref[...] = wu_ref[...].astype(BF16)
        wdb_ref[...] = wd_ref[...].astype(BF16)
        attend(slot)

    @pl.when(l == n_tiles)
    def _():
        finish(1 - slot)


def _attention(lam, zp3, subln_g, wg2, wu2, wd2):
    bsz, seq, _ = zp3.shape
    nq = seq // ATT_TQ
    steps = bsz * B_HEADS * nq
    qb, kb, vb = (2 * A_WIDTH) // HEAD, (2 * A_WIDTH + 1024) // HEAD, (2 * A_WIDTH + 2048) // HEAD

    def tile(t):
        return t // (B_HEADS * nq), (t // nq) % B_HEADS, t % nq

    def cur(l):
        return jnp.minimum(l, steps - 1)

    def q_map(l, lam, ahead):
        b, h, i = tile(jnp.minimum(l + ahead, steps - 1))
        return b, i, qb + h

    def kv_map(l, lam, base):
        b, h, _ = tile(cur(l))
        return b, 0, base + h

    def o_map(l, lam):
        b, h, i = tile(jnp.maximum(l - 1, 0))
        return b, i, h

    def w_spec(w):
        rows = w.shape[0] // steps
        return pl.BlockSpec((rows, w.shape[1]), lambda l, lam: (cur(l), 0))

    w_specs = [w_spec(wg2), w_spec(wu2), w_spec(wd2)]
    grid_spec = pltpu.PrefetchScalarGridSpec(
        num_scalar_prefetch=1,
        grid=(steps + 1,),
        in_specs=[
            pl.BlockSpec((None, ATT_TQ, HEAD), functools.partial(q_map, ahead=0)),
            pl.BlockSpec((None, ATT_TQ, HEAD), functools.partial(q_map, ahead=1)),
            pl.BlockSpec((None, seq, HEAD), functools.partial(kv_map, base=kb)),
            pl.BlockSpec((None, seq, HEAD), functools.partial(kv_map, base=vb)),
            pl.BlockSpec((1, HEAD), lambda l, lam: (0, 0)),
        ] + w_specs,
        out_specs=[pl.BlockSpec((None, ATT_TQ, HEAD), o_map)] + w_specs,
        scratch_shapes=[
            pltpu.VMEM((HEAD + ONES_ROWS, seq), BF16),
            pltpu.VMEM((2, 2, HEAD, ATT_TQ), BF16),
            pltpu.VMEM((2, 2, HEAD + ONES_ROWS, ATT_TQ), F32),
        ],
    )
    return pl.pallas_call(
        functools.partial(_attn_kernel, steps, nq),
        grid_spec=grid_spec,
        out_shape=[jax.ShapeDtypeStruct((bsz, seq, B_HEADS * HEAD), BF16)]
                  + [jax.ShapeDtypeStruct(w.shape, BF16) for w in (wg2, wu2, wd2)],
        compiler_params=_cparams(("arbitrary",)),
        name="attn",
    )(lam, zp3, zp3, zp3, zp3, subln_g, wg2, wu2, wd2)


def _out_proj_kernel(a_ref, b_ref, x_ref, wa_ref, wb_ref, g_ref, wr_ref, br_ref, x1_ref, hm_ref, lg_ref):
    acc = (jnp.dot(a_ref[...], wa_ref[...], preferred_element_type=F32)
           + jnp.dot(b_ref[...], wb_ref[...], preferred_element_type=F32))
    x1 = x_ref[...] + acc
    x1_ref[...] = x1
    ms = jnp.mean(x1 * x1, axis=-1, keepdims=True)
    hm = x1 * lax.rsqrt(ms + EPS) * g_ref[...]
    hm_ref[...] = _to_token_rows(hm)
    lg_ref[...] = lax.dot_general(wr_ref[...], hm.astype(BF16), (((1,), (1,)), ((), ())),
                                  preferred_element_type=F32) + br_ref[...]


def _out_proj(out_a, out_b, x2, w_out_bf, g, w_rt, b_rt):
    T = x2.shape[0]
    return pl.pallas_call(
        _out_proj_kernel,
        grid=(T // OUT_TM,),
        in_specs=[
            pl.BlockSpec((OUT_TM, A_WIDTH), lambda i: (i, 0)),
            pl.BlockSpec((OUT_TM, A_WIDTH), lambda i: (i, 0)),
            pl.BlockSpec((OUT_TM, D_MODEL), lambda i: (i, 0)),
            pl.BlockSpec((A_WIDTH, D_MODEL), lambda i: (0, 0)),
            pl.BlockSpec((A_WIDTH, D_MODEL), lambda i: (1, 0)),
            pl.BlockSpec((1, D_MODEL), lambda i: (0, 0)),
            pl.BlockSpec((LG_ROWS, D_MODEL), lambda i: (0, 0)),
            pl.BlockSpec((LG_ROWS, 1), lambda i: (0, 0)),
        ],
        out_specs=[
            pl.BlockSpec((OUT_TM, D_MODEL), lambda i: (i, 0)),
            pl.BlockSpec((OUT_TM * ROW_SUB, LANES), lambda i: (i, 0)),
            pl.BlockSpec((LG_ROWS, OUT_TM), lambda i: (0, i)),
        ],
        out_shape=[
            jax.ShapeDtypeStruct((T, D_MODEL), F32),
            jax.ShapeDtypeStruct((T * ROW_SUB, LANES), F32),
            jax.ShapeDtypeStruct((LG_ROWS, T), F32),
        ],
        compiler_params=_cparams(("parallel",)),
        name="out_proj",
    )(out_a, out_b, x2, w_out_bf, w_out_bf, g, w_rt, b_rt)


def _route_kernel(lg_ref, tri_ref, idr_ref, cw_ref, cnt_ref, carry_ref):
    i = pl.program_id(0)

    @pl.when(i == 0)
    def _():
        carry_ref[...] = jnp.zeros_like(carry_ref)

    tn = lg_ref.shape[1]
    row = lax.broadcasted_iota(I32, (SUBLANES, tn), 0)
    neg = jnp.float32(-jnp.inf)

    def top1(v):
        mx = jnp.max(v, axis=0, keepdims=True)
        idx = jnp.min(jnp.where(v == mx, row, SUBLANES), axis=0, keepdims=True)
        return mx, idx

    g = jnp.where(row < N_GROUPS, lg_ref[0:SUBLANES, :], neg)
    gmax, gidx = top1(g)
    g_w = 1.0 / jnp.sum(jnp.exp(g - gmax), axis=0, keepdims=True)
    sel = lg_ref[SUBLANES * N_GROUPS:SUBLANES * (N_GROUPS + 1), :]
    for grp in range(N_GROUPS - 2, -1, -1):
        sel = jnp.where(gidx == grp, lg_ref[SUBLANES * (grp + 1):SUBLANES * (grp + 2), :], sel)
    e1, i1 = top1(sel)
    e2, i2 = top1(jnp.where(row == i1, neg, sel))
    d = jnp.exp(e2 - e1)
    w1 = 1.0 / (1.0 + d)
    w2 = d * w1
    id0 = gidx * EPG + i1
    id1 = gidx * EPG + i2

    erow = lax.broadcasted_iota(I32, (N_EXPERTS, tn), 0)
    o0 = erow == id0
    o1 = erow == id1
    occ = jnp.where(o0 | o1, 1.0, 0.0).astype(F32)
    before = jnp.dot(occ.astype(BF16), tri_ref[...], preferred_element_type=F32) + carry_ref[:, 0:1]
    r0 = jnp.sum(jnp.where(o0, before, 0.0), axis=0, keepdims=True)
    r1 = jnp.sum(jnp.where(o1, before, 0.0), axis=0, keepdims=True)
    total = carry_ref[...] + jnp.sum(occ, axis=1, keepdims=True)
    carry_ref[...] = total
    cnt_ref[...] = total.astype(I32)

    zi = jnp.zeros((SUBLANES - 4, tn), I32)
    idr_ref[...] = jnp.concatenate([id0, id1, r0.astype(I32), r1.astype(I32), zi], axis=0)
    zf = jnp.zeros((SUBLANES - 2, tn), F32)
    cw_ref[...] = jnp.concatenate([g_w * w1, g_w * w2, zf], axis=0)


def _route(lg_t, tri):
    T = lg_t.shape[1]
    return pl.pallas_call(
        _route_kernel,
        grid=(T // RT_TN,),
        in_specs=[
            pl.BlockSpec((LG_ROWS, RT_TN), lambda i: (0, i)),
            pl.BlockSpec((RT_TN, RT_TN), lambda i: (0, 0)),
        ],
        out_specs=[
            pl.BlockSpec((SUBLANES, RT_TN), lambda i: (0, i)),
            pl.BlockSpec((SUBLANES, RT_TN), lambda i: (0, i)),
            pl.BlockSpec((N_EXPERTS, LANES), lambda i: (0, 0)),
        ],
        out_shape=[
            jax.ShapeDtypeStruct((SUBLANES, T), I32),
            jax.ShapeDtypeStruct((SUBLANES, T), F32),
            jax.ShapeDtypeStruct((N_EXPERTS, LANES), I32),
        ],
        scratch_shapes=[pltpu.VMEM((N_EXPERTS, LANES), F32)],
        compiler_params=_cparams(("arbitrary",)),
        name="route",
    )(lg_t, tri)


def _row_copy(src_ref, src_row, dst_ref, dst_row, sem):
    return pltpu.make_async_copy(src_ref.at[pl.ds(src_row * ROW_SUB, ROW_SUB)],
                                 dst_ref.at[pl.ds(dst_row * ROW_SUB, ROW_SUB)], sem)


def _dispatch_kernel(zt_ref, pos_ref, hm_ref, xs_ref, zbuf, sem):
    i = pl.program_id(0)

    def zero_copy(e):
        return pltpu.make_async_copy(
            zbuf, xs_ref.at[pl.ds(pl.multiple_of(zt_ref[e] * ROW_SUB, ROW_SUB), EXP_BLK * ROW_SUB)], sem)

    @pl.when(i == 0)
    def _():
        zbuf[...] = jnp.zeros_like(zbuf)
        for e in range(2 * N_EXPERTS):
            @pl.when(zt_ref[e] >= 0)
            def _():
                zero_copy(e).start()
        for e in range(2 * N_EXPERTS):
            @pl.when(zt_ref[e] >= 0)
            def _():
                zero_copy(e).wait()

    def issue(grp, carry):
        for u in range(DMA_UNROLL):
            t = grp * DMA_UNROLL + u
            for k in range(TOP_K):
                _row_copy(hm_ref, t, xs_ref, pos_ref[k, t], sem).start()
        return carry

    lax.fori_loop(0, DSP_TC // DMA_UNROLL, issue, 0)

    def drain(grp, carry):
        for _ in range(DMA_UNROLL * TOP_K):
            _row_copy(hm_ref, 0, xs_ref, 0, sem).wait()
        return carry

    lax.fori_loop(0, DSP_TC // DMA_UNROLL, drain, 0)


def _dispatch(zt, pos, hm3, n_rows):
    T = pos.shape[1]
    grid_spec = pltpu.PrefetchScalarGridSpec(
        num_scalar_prefetch=1,
        grid=(T // DSP_TC,),
        in_specs=[
            pl.BlockSpec((TOP_K, DSP_TC), lambda i, zt: (0, i), memory_space=pltpu.SMEM),
            pl.BlockSpec((DSP_TC * ROW_SUB, LANES), lambda i, zt: (i, 0)),
        ],
        out_specs=pl.BlockSpec(memory_space=pl.ANY),
        scratch_shapes=[pltpu.VMEM((EXP_BLK * ROW_SUB, LANES), F32), pltpu.SemaphoreType.DMA(())],
    )
    return pl.pallas_call(
        _dispatch_kernel,
        grid_spec=grid_spec,
        out_shape=jax.ShapeDtypeStruct((n_rows * ROW_SUB, LANES), F32),
        compiler_params=_cparams(("arbitrary",)),
        name="dispatch",
    )(zt, pos, hm3)


def _expert_kernel(be_ref, nu_ref, xs_ref, wg_ref, wu_ref, wd_ref, y_ref, xb_ref):
    b = pl.program_id(0)

    @pl.when(b < nu_ref[0])
    def _():
        slabs = _from_token_rows(xs_ref[...])
        for s in range(ROW_SUB):
            xb_ref[:, s * LANES:(s + 1) * LANES] = slabs[s].astype(BF16)
        x = xb_ref[...]
        g = jnp.dot(x, wg_ref[...], preferred_element_type=F32)
        u = jnp.dot(x, wu_ref[...], preferred_element_type=F32)
        a = (jax.nn.silu(g) * u).astype(BF16)
        y = jnp.dot(a, wd_ref[...], preferred_element_type=F32)
        y_ref[...] = _to_token_rows(y)

    @pl.when(b >= nu_ref[0])
    def _():
        y_ref[...] = jnp.zeros_like(y_ref)


def _experts(block_e, n_used, xs3, wg, wu, wd):
    n_blocks = xs3.shape[0] // (EXP_BLK * ROW_SUB)

    def xs_map(b, be, nu):
        return (jnp.minimum(b, nu[0] - 1), 0)

    def w_map(b, be, nu):
        return (be[b], 0, 0)

    grid_spec = pltpu.PrefetchScalarGridSpec(
        num_scalar_prefetch=2,
        grid=(n_blocks,),
        in_specs=[
            pl.BlockSpec((EXP_BLK * ROW_SUB, LANES), xs_map),
            pl.BlockSpec((None, D_MODEL, D_EXPERT), w_map),
            pl.BlockSpec((None, D_MODEL, D_EXPERT), w_map),
            pl.BlockSpec((None, D_EXPERT, D_MODEL), w_map),
        ],
        out_specs=pl.BlockSpec((EXP_BLK * ROW_SUB, LANES), lambda b, be, nu: (b, 0)),
        scratch_shapes=[pltpu.VMEM((EXP_BLK, D_MODEL), BF16)],
    )
    return pl.pallas_call(
        _expert_kernel,
        grid_spec=grid_spec,
        out_shape=jax.ShapeDtypeStruct(xs3.shape, F32),
        compiler_params=_cparams(("arbitrary",)),
        name="experts",
    )(block_e, n_used, xs3, wg, wu, wd)


def _combine_kernel(pos_ref, posn_ref, x1_ref, cw_ref, y_ref, o_ref, yb, sems):
    i = pl.program_id(0)
    slot = i % 2

    def gather(ids_ref, slot_):
        def issue(grp, carry):
            for u in range(DMA_UNROLL):
                t = grp * DMA_UNROLL + u
                for k in range(TOP_K):
                    _row_copy(y_ref, ids_ref[k, t], yb.at[slot_, k], t, sems.at[slot_]).start()
            return carry

        lax.fori_loop(0, CMB_TC // DMA_UNROLL, issue, 0)

    @pl.when(i == 0)
    def _():
        gather(pos_ref, 0)

    @pl.when(i + 1 < pl.num_programs(0))
    def _():
        gather(posn_ref, 1 - slot)

    def drain(grp, carry):
        for _ in range(DMA_UNROLL * TOP_K):
            _row_copy(y_ref, 0, yb.at[slot, 0], 0, sems.at[slot]).wait()
        return carry

    lax.fori_loop(0, CMB_TC // DMA_UNROLL, drain, 0)
    w_t = cw_ref[...].T
    w0 = w_t[:, 0:1]
    w1 = w_t[:, 1:2]
    y0 = _from_token_rows(yb[slot, 0])
    y1 = _from_token_rows(yb[slot, 1])
    for s in range(ROW_SUB):
        sl = slice(s * LANES, (s + 1) * LANES)
        o_ref[:, sl] = x1_ref[:, sl] + w0 * y0[s] + w1 * y1[s]


def _combine(pos, x1, cw, y3):
    T = x1.shape[0]
    steps = T // CMB_TC
    return pl.pallas_call(
        _combine_kernel,
        grid=(steps,),
        in_specs=[
            pl.BlockSpec((TOP_K, CMB_TC), lambda i: (0, i), memory_space=pltpu.SMEM),
            pl.BlockSpec((TOP_K, CMB_TC), lambda i: (0, jnp.minimum(i + 1, steps - 1)),
                         memory_space=pltpu.SMEM),
            pl.BlockSpec((CMB_TC, D_MODEL), lambda i: (i, 0)),
            pl.BlockSpec((SUBLANES, CMB_TC), lambda i: (0, i)),
            pl.BlockSpec(memory_space=pl.ANY),
        ],
        out_specs=pl.BlockSpec((CMB_TC, D_MODEL), lambda i: (i, 0)),
        out_shape=jax.ShapeDtypeStruct((T, D_MODEL), F32),
        scratch_shapes=[
            pltpu.VMEM((2, TOP_K, CMB_TC * ROW_SUB, LANES), F32),
            pltpu.SemaphoreType.DMA((2,)),
        ],
        compiler_params=_cparams(("arbitrary",)),
        name="combine",
    )(pos, pos, x1, cw, y3)


def _rotary_lane_tables(seq):
    pos = jnp.arange(seq, dtype=F32)
    inv_freq = 1.0 / (jnp.float32(ROPE_THETA) ** (jnp.arange(0, ROT_DIM, 2, dtype=F32) / ROT_DIM))
    ang = pos[:, None] * inv_freq[None, :]
    cos, sin = jnp.cos(ang), jnp.sin(ang)
    half = ROT_DIM // 2
    lane = jnp.arange(LANES) % QK_DIM
    fidx = lane % half
    first = (lane < half)[None, :]
    second = ((lane >= half) & (lane < ROT_DIM))[None, :]
    cos_t = jnp.where(first | second, cos[:, fidx], 1.0)
    s_up = jnp.where(first, -sin[:, fidx], 0.0)
    s_dn = jnp.where(second, sin[:, fidx], 0.0)
    return cos_t, s_up, s_dn


def kernel(x, attn_norm_g, w_in, gmlp_ln_g, gmlp_ln_b, gmlp_ws, gmlp_bs, q_norm_g, k_norm_g,
           lambda_q1, lambda_k1, lambda_q2, lambda_k2, subln_g, w_out, ffn_norm_g,
           w_group, b_group, w_router, b_router, w_gate, w_up, w_down):
    bsz, seq, d = x.shape
    T = bsz * seq
    assert d == D_MODEL and seq % IN_TM == 0 and seq % ATT_TQ == 0 and T % RT_TN == 0
    l = 0
    x2 = x.reshape(T, d)

    params = jnp.zeros((8, IN_WIDTH), F32)
    params = params.at[0, A_WIDTH:2 * A_WIDTH].set(gmlp_ln_g[l].reshape(-1))
    params = params.at[1, A_WIDTH:2 * A_WIDTH].set(gmlp_ln_b[l].reshape(-1))
    params = params.at[0, 2 * A_WIDTH:3 * A_WIDTH].set(jnp.tile(q_norm_g[l], 2 * B_HEADS))
    params = params.at[0, 3 * A_WIDTH:4 * A_WIDTH].set(jnp.tile(k_norm_g[l], 2 * B_HEADS))
    blk = jnp.arange(IN_GRP) // QK_DIM
    bd = jnp.where(blk[:, None] == blk[None, :], 1.0 / QK_DIM, 0.0).astype(BF16)
    cos_t, s_up, s_dn = _rotary_lane_tables(seq)
    bs_full = jnp.repeat(jnp.transpose(gmlp_bs[l]), HEAD, axis=1)
    lam = (jnp.exp(jnp.sum(lambda_q1[l] * lambda_k1[l])) - jnp.exp(jnp.sum(lambda_q2[l] * lambda_k2[l]))
           + LAMBDA_INIT).reshape(1).astype(F32)
    w_rt = jnp.zeros((LG_ROWS, D_MODEL), F32)
    w_rt = w_rt.at[:N_GROUPS].set(w_group[l].T).at[SUBLANES:].set(w_router[l].T)
    b_rt = jnp.zeros((LG_ROWS, 1), F32)
    b_rt = b_rt.at[:N_GROUPS, 0].set(b_group[l]).at[SUBLANES:, 0].set(b_router[l])
    tri = (jnp.arange(RT_TN)[:, None] < jnp.arange(RT_TN)[None, :]).astype(BF16)

    w_sec = w_in[l].reshape(D_MODEL, IN_WIDTH // IN_TN, IN_TN).transpose(1, 0, 2).astype(BF16)
    zp = _in_proj(x2, attn_norm_g[l].reshape(1, d), w_sec, params, bd,
                  cos_t, s_up, s_dn, seq)
    out_a = _gmlp(zp, gmlp_ws[l].astype(BF16), bs_full)
    att_steps = bsz * B_HEADS * (seq // ATT_TQ)
    assert (N_EXPERTS * D_MODEL) % att_steps == 0 and (N_EXPERTS * D_EXPERT) % att_steps == 0
    out_b, wg_bf, wu_bf, wd_bf = _attention(
        lam, zp.reshape(bsz, seq, IN_WIDTH), subln_g[l].reshape(1, HEAD),
        w_gate[l].reshape(N_EXPERTS * D_MODEL, D_EXPERT), w_up[l].reshape(N_EXPERTS * D_MODEL, D_EXPERT),
        w_down[l].reshape(N_EXPERTS * D_EXPERT, D_MODEL))
    x1, hm3, lg_t = _out_proj(out_a, out_b.reshape(T, A_WIDTH), x2, w_out[l].astype(BF16),
                              ffn_norm_g[l].reshape(1, d), w_rt.astype(BF16), b_rt)

    idr, cw, cnt = _route(lg_t, tri)
    counts = cnt[:, 0]
    padded = (counts + EXP_BLK - 1) // EXP_BLK * EXP_BLK
    pend = jnp.cumsum(padded)
    pstart = (pend - padded).astype(I32)
    n_blocks = (T * TOP_K) // EXP_BLK + N_EXPERTS
    n_rows = n_blocks * EXP_BLK
    block_e = jnp.minimum(jnp.sum(pend[None, :] <= (jnp.arange(n_blocks) * EXP_BLK)[:, None], axis=1),
                          N_EXPERTS - 1).astype(I32)
    n_used = (pend[-1] // EXP_BLK).reshape(1).astype(I32)
    trail = pend[-1] + jnp.arange(N_EXPERTS) * EXP_BLK
    zt = jnp.concatenate([jnp.where(counts > 0, pend - EXP_BLK, -1),
                          jnp.where(trail < n_rows, trail, -1)]).astype(I32)

    is_e = idr[0:TOP_K, None, :] == jnp.arange(N_EXPERTS, dtype=I32)[None, :, None]
    pos = jnp.sum(jnp.where(is_e, pstart[None, :, None], 0), axis=1) + idr[TOP_K:2 * TOP_K]
    xs3 = _dispatch(zt, pos, hm3, n_rows)
    y3 = _experts(block_e, n_used, xs3, wg_bf.reshape(N_EXPERTS, D_MODEL, D_EXPERT),
                  wu_bf.reshape(N_EXPERTS, D_MODEL, D_EXPERT), wd_bf.reshape(N_EXPERTS, D_EXPERT, D_MODEL))
    out = _combine(pos, x1, cw, y3)
    return out.reshape(bsz, seq, d)
```

```python
import functools
import math

import jax
import jax.numpy as jnp
from jax import lax
from jax.experimental import pallas as pl
from jax.experimental.pallas import tpu as pltpu

F32 = jnp.float32
BF16 = jnp.bfloat16
I32 = jnp.int32

D_MODEL = 2048
A_WIDTH = 1024
A_HEADS = 8
HEAD = 128
CHUNK = 128
B_HEADS = 8
QK_DIM = 64
ROT_DIM = 16
ROPE_THETA = 500000.0
IN_WIDTH = 5120
N_GROUPS = 4
EPG = 8
N_EXPERTS = 32
TOP_K = 2
D_EXPERT = 1024
EPS = 1e-6
LAMBDA_INIT = 0.8 - 0.6 * math.exp(-0.3 * 0)

LANES = 128
SUBLANES = 8
ROW_SUB = D_MODEL // LANES
LG_ROWS = SUBLANES * (1 + N_GROUPS)

IN_TM = 512
IN_TN = 1024
IN_GRP = 256
IN_ROWS = 64
GM_TM = 512
ATT_TQ = 512
ATT_TK = 512
ONES_ROWS = 16
OUT_TM = 256
RT_TN = 512
EXP_BLK = 256
DSP_TC = 512
CMB_TC = 256
DMA_UNROLL = 8
VMEM_LIMIT = 48 * 1024 * 1024


def _cparams(sem):
    return pltpu.CompilerParams(dimension_semantics=sem, vmem_limit_bytes=VMEM_LIMIT)


def _to_token_rows(x):
    rows = x.shape[0]
    slabs = jnp.stack([x[:, s * LANES:(s + 1) * LANES] for s in range(ROW_SUB)], axis=0)
    return jnp.transpose(slabs, (1, 0, 2)).reshape(rows * ROW_SUB, LANES)


def _from_token_rows(x3):
    rows = x3.shape[0] // ROW_SUB
    return jnp.transpose(x3.reshape(rows, ROW_SUB, LANES), (1, 0, 2))


def _in_proj_kernel(x_ref, g_ref, w_ref, p_ref, bd_ref, c_ref, s1_ref, s2_ref, o_ref, v_ref, hn_ref, z_ref):
    j = pl.program_id(1)
    tm = hn_ref.shape[0]
    ngrp = IN_TN // IN_GRP

    @pl.when(j == 0)
    def _():
        x = x_ref[...]
        ms = jnp.mean(x * x, axis=-1, keepdims=True)
        hn_ref[...] = (x * lax.rsqrt(ms + EPS) * g_ref[...]).astype(BF16)

    def project(slot):
        z_ref[slot] = jnp.dot(hn_ref[...], w_ref[...], preferred_element_type=F32)

    def gelu_section(slot):
        for r in range(tm // IN_ROWS):
            rows = slice(r * IN_ROWS, (r + 1) * IN_ROWS)
            o_ref[rows, :] = jax.nn.gelu(z_ref[slot, rows, :]).astype(BF16)

    def gelu_ln_section(slot):
        for s in range(IN_TN // LANES):
            sl = slice(s * LANES, (s + 1) * LANES)
            gz = jax.nn.gelu(z_ref[slot, :, sl])
            mu = jnp.mean(gz, axis=-1, keepdims=True)
            xc = gz - mu
            var = jnp.mean(xc * xc, axis=-1, keepdims=True)
            o_ref[:, sl] = (xc * lax.rsqrt(var + EPS) * p_ref[0:1, sl] + p_ref[1:2, sl]).astype(BF16)

    def qk_section(slot, scale):
        sq = jnp.concatenate([z_ref[slot, :, g * IN_GRP:(g + 1) * IN_GRP] for g in range(ngrp)], axis=0)
        ms = jnp.dot((sq * sq).astype(BF16), bd_ref[...], preferred_element_type=F32)
        for s in range(IN_TN // LANES):
            sl = slice(s * LANES, (s + 1) * LANES)
            g, c = divmod(s * LANES, IN_GRP)
            ms_s = ms[g * tm:(g + 1) * tm, c:c + LANES]
            y = z_ref[slot, :, sl] * lax.rsqrt(ms_s + EPS) * p_ref[0:1, sl]
            r = (y * c_ref[...] + pltpu.roll(y, LANES - ROT_DIM // 2, 1) * s1_ref[...]
                 + pltpu.roll(y, ROT_DIM // 2, 1) * s2_ref[...])
            o_ref[:, sl] = (r * scale if scale != 1.0 else r).astype(BF16)

    @pl.when(j == 0)
    def _():
        project(0)

    @pl.when(j == 1)
    def _():
        gelu_section(0)
        project(1)

    @pl.when(j == 2)
    def _():
        gelu_ln_section(1)
        project(0)

    @pl.when(j == 3)
    def _():
        qk_section(0, math.log2(math.e) / math.sqrt(QK_DIM))
        project(1)

    @pl.when(j == 4)
    def _():
        qk_section(1, 1.0)
        v_ref[...] = jnp.dot(hn_ref[...], w_ref[...], preferred_element_type=F32).astype(BF16)


def _in_proj(x2, g, w_bf, params, bd, cos_t, sup_t, sdn_t, seq):
    T = x2.shape[0]
    spt = seq // IN_TM
    nsec = IN_WIDTH // IN_TN
    assert nsec == 5
    return pl.pallas_call(
        _in_proj_kernel,
        grid=(T // IN_TM, nsec),
        in_specs=[
            pl.BlockSpec((IN_TM, D_MODEL), lambda i, j: (i, 0)),
            pl.BlockSpec((1, D_MODEL), lambda i, j: (0, 0)),
            pl.BlockSpec((D_MODEL, IN_TN), lambda i, j: (0, j)),
            pl.BlockSpec((8, IN_TN), lambda i, j: (0, jnp.maximum(j - 1, 0))),
            pl.BlockSpec((IN_GRP, IN_GRP), lambda i, j: (0, 0)),
            pl.BlockSpec((IN_TM, LANES), lambda i, j: (i % spt, 0)),
            pl.BlockSpec((IN_TM, LANES), lambda i, j: (i % spt, 0)),
            pl.BlockSpec((IN_TM, LANES), lambda i, j: (i % spt, 0)),
        ],
        out_specs=[
            pl.BlockSpec((IN_TM, IN_TN), lambda i, j: (i, jnp.maximum(j - 1, 0))),
            pl.BlockSpec((IN_TM, IN_TN), lambda i, j: (i, 0)),
        ],
        out_shape=[
            jax.ShapeDtypeStruct((T, IN_WIDTH - IN_TN), BF16),
            jax.ShapeDtypeStruct((T, IN_TN), BF16),
        ],
        scratch_shapes=[pltpu.VMEM((IN_TM, D_MODEL), BF16), pltpu.VMEM((2, IN_TM, IN_TN), F32)],
        compiler_params=_cparams(("parallel", "arbitrary")),
        name="in_proj",
    )(x2, g, w_bf, params, bd, cos_t, sup_t, sdn_t)


def _gmlp_kernel(u_ref, v_ref, ws_ref, bs_ref, o_ref):
    for c in range(GM_TM // CHUNK):
        rows = slice(c * CHUNK, (c + 1) * CHUNK)
        for h in range(A_HEADS):
            cols = slice(h * HEAD, (h + 1) * HEAD)
            s = jnp.dot(ws_ref[h], v_ref[rows, cols], preferred_element_type=F32) + bs_ref[:, cols]
            o_ref[rows, cols] = (u_ref[rows, cols].astype(F32) * s).astype(BF16)


def _gmlp(zp, ws_bf, bs_full):
    T = zp.shape[0]
    return pl.pallas_call(
        _gmlp_kernel,
        grid=(T // GM_TM,),
        in_specs=[
            pl.BlockSpec((GM_TM, A_WIDTH), lambda i: (i, 0)),
            pl.BlockSpec((GM_TM, A_WIDTH), lambda i: (i, 1)),
            pl.BlockSpec((A_HEADS, CHUNK, CHUNK), lambda i: (0, 0, 0)),
            pl.BlockSpec((CHUNK, A_WIDTH), lambda i: (0, 0)),
        ],
        out_specs=pl.BlockSpec((GM_TM, A_WIDTH), lambda i: (i, 0)),
        out_shape=jax.ShapeDtypeStruct((T, A_WIDTH), BF16),
        compiler_params=_cparams(("parallel",)),
        name="gmlp",
    )(zp, zp, ws_bf, bs_full)


def _attn_kernel(n_tiles, nq, lam_ref, qc_ref, qn_ref, k_ref, v_ref, g_ref, wg_ref, wu_ref, wd_ref,
                 o_ref, wgb_ref, wub_ref, wdb_ref, vt_ref, qt_ref, acc_ref):
    l = pl.program_id(0)
    slot = l % 2
    seq = k_ref.shape[0]
    nc = seq // ATT_TK
    tq = qc_ref.shape[0]

    def prep_q(q_ref, s):
        q_t = q_ref[...].astype(F32).T
        row = lax.broadcasted_iota(I32, q_t.shape, 0)
        qt_ref[s, 0] = jnp.where(row < QK_DIM, q_t, 0.0).astype(BF16)
        qt_ref[s, 1] = jnp.where(row >= QK_DIM, q_t, 0.0).astype(BF16)

    def finish(s):
        a1, a2 = acc_ref[s, 0], acc_ref[s, 1]
        o_t = (a1[0:HEAD] / a1[HEAD:HEAD + 1]
               - lam_ref[0] * (a2[0:HEAD] / a2[HEAD:HEAD + 1]))
        o = o_t.T
        ms = jnp.mean(o * o, axis=-1, keepdims=True)
        o_ref[...] = (o * lax.rsqrt(ms + EPS) * g_ref[...] * (1.0 - LAMBDA_INIT)).astype(BF16)

    def scores(c, qh_t):
        return jnp.dot(k_ref[c * ATT_TK:(c + 1) * ATT_TK, :], qh_t, preferred_element_type=F32)

    def update(c, s_t, m, acc):
        m_new = jnp.maximum(m, jnp.max(s_t, axis=0, keepdims=True))
        p = jnp.exp2(s_t - m_new).astype(BF16)
        pv = jnp.dot(vt_ref[:, c * ATT_TK:(c + 1) * ATT_TK], p, preferred_element_type=F32)
        return m_new, jnp.exp2(m - m_new) * acc + pv

    def attend(s):
        q1_t, q2_t = qt_ref[s, 0], qt_ref[s, 1]
        m1 = m2 = jnp.full((1, tq), -jnp.inf, F32)
        a1 = a2 = jnp.zeros((HEAD + ONES_ROWS, tq), F32)
        s1, s2 = scores(0, q1_t), scores(0, q2_t)
        for c in range(nc):
            if c + 1 < nc:
                n1, n2 = scores(c + 1, q1_t), scores(c + 1, q2_t)
            m1, a1 = update(c, s1, m1, a1)
            m2, a2 = update(c, s2, m2, a2)
            if c + 1 < nc:
                s1, s2 = n1, n2
        acc_ref[s, 0] = a1
        acc_ref[s, 1] = a2

    @pl.when(l == 0)
    def _():
        prep_q(qc_ref, 0)
        acc_ref[1] = jnp.ones(acc_ref.shape[1:], F32)

    @pl.when((l < n_tiles) & (l % nq == 0))
    def _():
        for c in range(nc):
            cols = slice(c * ATT_TK, (c + 1) * ATT_TK)
            vt_ref[0:HEAD, cols] = v_ref[cols, :].astype(F32).T.astype(BF16)
        vt_ref[HEAD:, :] = jnp.ones((ONES_ROWS, seq), BF16)

    @pl.when(l < n_tiles)
    def _():
        finish(1 - slot)
        prep_q(qn_ref, 1 - slot)
        wgb_ref[...] = wg_ref[...].astype(BF16)
        wub_ref[...] = wu_ref[...].astype(BF16)
        wdb_ref[...] = wd_ref[...].astype(BF16)
        attend(slot)

    @pl.when(l == n_tiles)
    def _():
        finish(1 - slot)


def _attention(lam, zp3, v3, subln_g, wg2, wu2, wd2):
    bsz, seq, _ = zp3.shape
    nq = seq // ATT_TQ
    steps = bsz * B_HEADS * nq
    qb, kb, vb = (2 * A_WIDTH) // HEAD, (2 * A_WIDTH + 1024) // HEAD, 0

    def tile(t):
        return t // (B_HEADS * nq), (t // nq) % B_HEADS, t % nq

    def cur(l):
        return jnp.minimum(l, steps - 1)

    def q_map(l, lam, ahead):
        b, h, i = tile(jnp.minimum(l + ahead, steps - 1))
        return b, i, qb + h

    def kv_map(l, lam, base):
        b, h, _ = tile(cur(l))
        return b, 0, base + h

    def o_map(l, lam):
        b, h, i = tile(jnp.maximum(l - 1, 0))
        return b, i, h

    def w_spec(w):
        rows = w.shape[0] // steps
        return pl.BlockSpec((rows, w.shape[1]), lambda l, lam: (cur(l), 0))

    w_specs = [w_spec(wg2), w_spec(wu2), w_spec(wd2)]
    grid_spec = pltpu.PrefetchScalarGridSpec(
        num_scalar_prefetch=1,
        grid=(steps + 1,),
        in_specs=[
            pl.BlockSpec((None, ATT_TQ, HEAD), functools.partial(q_map, ahead=0)),
            pl.BlockSpec((None, ATT_TQ, HEAD), functools.partial(q_map, ahead=1)),
            pl.BlockSpec((None, seq, HEAD), functools.partial(kv_map, base=kb)),
            pl.BlockSpec((None, seq, HEAD), functools.partial(kv_map, base=vb)),
            pl.BlockSpec((1, HEAD), lambda l, lam: (0, 0)),
        ] + w_specs,
        out_specs=[pl.BlockSpec((None, ATT_TQ, HEAD), o_map)] + w_specs,
        scratch_shapes=[
            pltpu.VMEM((HEAD + ONES_ROWS, seq), BF16),
            pltpu.VMEM((2, 2, HEAD, ATT_TQ), BF16),
            pltpu.VMEM((2, 2, HEAD + ONES_ROWS, ATT_TQ), F32),
        ],
    )
    return pl.pallas_call(
        functools.partial(_attn_kernel, steps, nq),
        grid_spec=grid_spec,
        out_shape=[jax.ShapeDtypeStruct((bsz, seq, B_HEADS * HEAD), BF16)]
                  + [jax.ShapeDtypeStruct(w.shape, BF16) for w in (wg2, wu2, wd2)],
        compiler_params=_cparams(("arbitrary",)),
        name="attn",
    )(lam, zp3, zp3, zp3, v3, subln_g, wg2, wu2, wd2)


def _out_proj_kernel(n_tiles, a_ref, b_ref, x_ref, wa_ref, wb_ref, g_ref, wr_ref, br_ref,
                     x1_ref, hm_ref, lg_ref, hbuf):
    i = pl.program_id(0)
    slot = i % 2

    def project(s):
        acc = (jnp.dot(a_ref[...], wa_ref[...], preferred_element_type=F32)
               + jnp.dot(b_ref[...], wb_ref[...], preferred_element_type=F32))
        x1 = x_ref[...] + acc
        x1_ref[...] = x1
        ms = jnp.mean(x1 * x1, axis=-1, keepdims=True)
        hbuf[s] = x1 * lax.rsqrt(ms + EPS) * g_ref[...]

    def emit(s):
        hm = hbuf[s]
        hm_ref[...] = _to_token_rows(hm)
        lg_ref[...] = lax.dot_general(wr_ref[...], hm.astype(BF16), (((1,), (1,)), ((), ())),
                                      preferred_element_type=F32) + br_ref[...]

    @pl.when(i == 0)
    def _():
        hbuf[1] = jnp.zeros(hbuf.shape[1:], F32)

    @pl.when(i < n_tiles)
    def _():
        emit(1 - slot)
        project(slot)

    @pl.when(i == n_tiles)
    def _():
        emit(1 - slot)


def _out_proj(out_a, out_b, x2, w_out_bf, g, w_rt, b_rt):
    T = x2.shape[0]
    n_tiles = T // OUT_TM

    def cur(i):
        return jnp.minimum(i, n_tiles - 1)

    def prev(i):
        return jnp.maximum(i - 1, 0)

    return pl.pallas_call(
        functools.partial(_out_proj_kernel, n_tiles),
        grid=(n_tiles + 1,),
        in_specs=[
            pl.BlockSpec((OUT_TM, A_WIDTH), lambda i: (cur(i), 0)),
            pl.BlockSpec((OUT_TM, A_WIDTH), lambda i: (cur(i), 0)),
            pl.BlockSpec((OUT_TM, D_MODEL), lambda i: (cur(i), 0)),
            pl.BlockSpec((A_WIDTH, D_MODEL), lambda i: (0, 0)),
            pl.BlockSpec((A_WIDTH, D_MODEL), lambda i: (1, 0)),
            pl.BlockSpec((1, D_MODEL), lambda i: (0, 0)),
            pl.BlockSpec((LG_ROWS, D_MODEL), lambda i: (0, 0)),
            pl.BlockSpec((LG_ROWS, 1), lambda i: (0, 0)),
        ],
        out_specs=[
            pl.BlockSpec((OUT_TM, D_MODEL), lambda i: (cur(i), 0)),
            pl.BlockSpec((OUT_TM * ROW_SUB, LANES), lambda i: (prev(i), 0)),
            pl.BlockSpec((LG_ROWS, OUT_TM), lambda i: (0, prev(i))),
        ],
        out_shape=[
            jax.ShapeDtypeStruct((T, D_MODEL), F32),
            jax.ShapeDtypeStruct((T * ROW_SUB, LANES), F32),
            jax.ShapeDtypeStruct((LG_ROWS, T), F32),
        ],
        scratch_shapes=[pltpu.VMEM((2, OUT_TM, D_MODEL), F32)],
        compiler_params=_cparams(("arbitrary",)),
        name="out_proj",
    )(out_a, out_b, x2, w_out_bf, w_out_bf, g, w_rt, b_rt)


def _route_kernel(lg_ref, tri_ref, idr_ref, cw_ref, cnt_ref, carry_ref):
    i = pl.program_id(0)

    @pl.when(i == 0)
    def _():
        carry_ref[...] = jnp.zeros_like(carry_ref)

    tn = lg_ref.shape[1]
    row = lax.broadcasted_iota(I32, (SUBLANES, tn), 0)
    neg = jnp.float32(-jnp.inf)

    def top1(v):
        mx = jnp.max(v, axis=0, keepdims=True)
        idx = jnp.min(jnp.where(v == mx, row, SUBLANES), axis=0, keepdims=True)
        return mx, idx

    g = jnp.where(row < N_GROUPS, lg_ref[0:SUBLANES, :], neg)
    gmax, gidx = top1(g)
    g_w = 1.0 / jnp.sum(jnp.exp(g - gmax), axis=0, keepdims=True)
    sel = lg_ref[SUBLANES * N_GROUPS:SUBLANES * (N_GROUPS + 1), :]
    for grp in range(N_GROUPS - 2, -1, -1):
        sel = jnp.where(gidx == grp, lg_ref[SUBLANES * (grp + 1):SUBLANES * (grp + 2), :], sel)
    e1, i1 = top1(sel)
    e2, i2 = top1(jnp.where(row == i1, neg, sel))
    d = jnp.exp(e2 - e1)
    w1 = 1.0 / (1.0 + d)
    w2 = d * w1
    id0 = gidx * EPG + i1
    id1 = gidx * EPG + i2

    erow = lax.broadcasted_iota(I32, (N_EXPERTS, tn), 0)
    o0 = erow == id0
    o1 = erow == id1
    occ = jnp.where(o0 | o1, 1.0, 0.0).astype(F32)
    before = jnp.dot(occ.astype(BF16), tri_ref[...], preferred_element_type=F32) + carry_ref[:, 0:1]
    r0 = jnp.sum(jnp.where(o0, before, 0.0), axis=0, keepdims=True)
    r1 = jnp.sum(jnp.where(o1, before, 0.0), axis=0, keepdims=True)
    total = carry_ref[...] + jnp.sum(occ, axis=1, keepdims=True)
    carry_ref[...] = total
    cnt_ref[...] = total.astype(I32)

    zi = jnp.zeros((SUBLANES - 4, tn), I32)
    idr_ref[...] = jnp.concatenate([id0, id1, r0.astype(I32), r1.astype(I32), zi], axis=0)
    zf = jnp.zeros((SUBLANES - 2, tn), F32)
    cw_ref[...] = jnp.concatenate([g_w * w1, g_w * w2, zf], axis=0)


def _route(lg_t, tri):
    T = lg_t.shape[1]
    return pl.pallas_call(
        _route_kernel,
        grid=(T // RT_TN,),
        in_specs=[
            pl.BlockSpec((LG_ROWS, RT_TN), lambda i: (0, i)),
            pl.BlockSpec((RT_TN, RT_TN), lambda i: (0, 0)),
        ],
        out_specs=[
            pl.BlockSpec((SUBLANES, RT_TN), lambda i: (0, i)),
            pl.BlockSpec((SUBLANES, RT_TN), lambda i: (0, i)),
            pl.BlockSpec((N_EXPERTS, LANES), lambda i: (0, 0)),
        ],
        out_shape=[
            jax.ShapeDtypeStruct((SUBLANES, T), I32),
            jax.ShapeDtypeStruct((SUBLANES, T), F32),
            jax.ShapeDtypeStruct((N_EXPERTS, LANES), I32),
        ],
        scratch_shapes=[pltpu.VMEM((N_EXPERTS, LANES), F32)],
        compiler_params=_cparams(("arbitrary",)),
        name="route",
    )(lg_t, tri)


def _row_copy(src_ref, src_row, dst_ref, dst_row, sem):
    return pltpu.make_async_copy(src_ref.at[pl.ds(src_row * ROW_SUB, ROW_SUB)],
                                 dst_ref.at[pl.ds(dst_row * ROW_SUB, ROW_SUB)], sem)


def _dispatch_kernel(zt_ref, pos_ref, hm_ref, xs_ref, zbuf, sem):
    i = pl.program_id(0)

    def zero_copy(e):
        return pltpu.make_async_copy(
            zbuf, xs_ref.at[pl.ds(pl.multiple_of(zt_ref[e] * ROW_SUB, ROW_SUB), EXP_BLK * ROW_SUB)], sem)

    @pl.when(i == 0)
    def _():
        zbuf[...] = jnp.zeros_like(zbuf)
        for e in range(2 * N_EXPERTS):
            @pl.when(zt_ref[e] >= 0)
            def _():
                zero_copy(e).start()
        for e in range(2 * N_EXPERTS):
            @pl.when(zt_ref[e] >= 0)
            def _():
                zero_copy(e).wait()

    def issue(grp, carry):
        for u in range(DMA_UNROLL):
            t = grp * DMA_UNROLL + u
            for k in range(TOP_K):
                _row_copy(hm_ref, t, xs_ref, pos_ref[k, t], sem).start()
        return carry

    lax.fori_loop(0, DSP_TC // DMA_UNROLL, issue, 0)

    def drain(grp, carry):
        for _ in range(DMA_UNROLL * TOP_K):
            _row_copy(hm_ref, 0, xs_ref, 0, sem).wait()
        return carry

    lax.fori_loop(0, DSP_TC // DMA_UNROLL, drain, 0)


def _dispatch(zt, pos, hm3, n_rows):
    T = pos.shape[1]
    grid_spec = pltpu.PrefetchScalarGridSpec(
        num_scalar_prefetch=1,
        grid=(T // DSP_TC,),
        in_specs=[
            pl.BlockSpec((TOP_K, DSP_TC), lambda i, zt: (0, i), memory_space=pltpu.SMEM),
            pl.BlockSpec((DSP_TC * ROW_SUB, LANES), lambda i, zt: (i, 0)),
        ],
        out_specs=pl.BlockSpec(memory_space=pl.ANY),
        scratch_shapes=[pltpu.VMEM((EXP_BLK * ROW_SUB, LANES), F32), pltpu.SemaphoreType.DMA(())],
    )
    return pl.pallas_call(
        _dispatch_kernel,
        grid_spec=grid_spec,
        out_shape=jax.ShapeDtypeStruct((n_rows * ROW_SUB, LANES), F32),
        compiler_params=_cparams(("arbitrary",)),
        name="dispatch",
    )(zt, pos, hm3)


def _expert_kernel(be_ref, nu_ref, xs_ref, wg_ref, wu_ref, wd_ref, y_ref, xb_ref):
    b = pl.program_id(0)

    @pl.when(b < nu_ref[0])
    def _():
        slabs = _from_token_rows(xs_ref[...])
        for s in range(ROW_SUB):
            xb_ref[:, s * LANES:(s + 1) * LANES] = slabs[s].astype(BF16)
        x = xb_ref[...]
        g = jnp.dot(x, wg_ref[...], preferred_element_type=F32)
        u = jnp.dot(x, wu_ref[...], preferred_element_type=F32)
        a = (jax.nn.silu(g) * u).astype(BF16)
        y = jnp.dot(a, wd_ref[...], preferred_element_type=F32)
        y_ref[...] = _to_token_rows(y)

    @pl.when(b >= nu_ref[0])
    def _():
        y_ref[...] = jnp.zeros_like(y_ref)


def _experts(block_e, n_used, xs3, wg, wu, wd):
    n_blocks = xs3.shape[0] // (EXP_BLK * ROW_SUB)

    def xs_map(b, be, nu):
        return (jnp.minimum(b, nu[0] - 1), 0)

    def w_map(b, be, nu):
        return (be[b], 0, 0)

    grid_spec = pltpu.PrefetchScalarGridSpec(
        num_scalar_prefetch=2,
        grid=(n_blocks,),
        in_specs=[
            pl.BlockSpec((EXP_BLK * ROW_SUB, LANES), xs_map),
            pl.BlockSpec((None, D_MODEL, D_EXPERT), w_map),
            pl.BlockSpec((None, D_MODEL, D_EXPERT), w_map),
            pl.BlockSpec((None, D_EXPERT, D_MODEL), w_map),
        ],
        out_specs=pl.BlockSpec((EXP_BLK * ROW_SUB, LANES), lambda b, be, nu: (b, 0)),
        scratch_shapes=[pltpu.VMEM((EXP_BLK, D_MODEL), BF16)],
    )
    return pl.pallas_call(
        _expert_kernel,
        grid_spec=grid_spec,
        out_shape=jax.ShapeDtypeStruct(xs3.shape, F32),
        compiler_params=_cparams(("arbitrary",)),
        name="experts",
    )(block_e, n_used, xs3, wg, wu, wd)


def _combine_kernel(pos_ref, posn_ref, x1_ref, cw_ref, y_ref, o_ref, yb, sems):
    i = pl.program_id(0)
    slot = i % 2

    def gather(ids_ref, slot_):
        def issue(grp, carry):
            for u in range(DMA_UNROLL):
                t = grp * DMA_UNROLL + u
                for k in range(TOP_K):
                    _row_copy(y_ref, ids_ref[k, t], yb.at[slot_, k], t, sems.at[slot_]).start()
            return carry

        lax.fori_loop(0, CMB_TC // DMA_UNROLL, issue, 0)

    @pl.when(i == 0)
    def _():
        gather(pos_ref, 0)

    @pl.when(i + 1 < pl.num_programs(0))
    def _():
        gather(posn_ref, 1 - slot)

    def drain(grp, carry):
        for _ in range(DMA_UNROLL * TOP_K):
            _row_copy(y_ref, 0, yb.at[slot, 0], 0, sems.at[slot]).wait()
        return carry

    lax.fori_loop(0, CMB_TC // DMA_UNROLL, drain, 0)
    w_t = cw_ref[...].T
    w0 = w_t[:, 0:1]
    w1 = w_t[:, 1:2]
    y0 = _from_token_rows(yb[slot, 0])
    y1 = _from_token_rows(yb[slot, 1])
    for s in range(ROW_SUB):
        sl = slice(s * LANES, (s + 1) * LANES)
        o_ref[:, sl] = x1_ref[:, sl] + w0 * y0[s] + w1 * y1[s]


def _combine(pos, x1, cw, y3):
    T = x1.shape[0]
    steps = T // CMB_TC
    return pl.pallas_call(
        _combine_kernel,
        grid=(steps,),
        in_specs=[
            pl.BlockSpec((TOP_K, CMB_TC), lambda i: (0, i), memory_space=pltpu.SMEM),
            pl.BlockSpec((TOP_K, CMB_TC), lambda i: (0, jnp.minimum(i + 1, steps - 1)),
                         memory_space=pltpu.SMEM),
            pl.BlockSpec((CMB_TC, D_MODEL), lambda i: (i, 0)),
            pl.BlockSpec((SUBLANES, CMB_TC), lambda i: (0, i)),
            pl.BlockSpec(memory_space=pl.ANY),
        ],
        out_specs=pl.BlockSpec((CMB_TC, D_MODEL), lambda i: (i, 0)),
        out_shape=jax.ShapeDtypeStruct((T, D_MODEL), F32),
        scratch_shapes=[
            pltpu.VMEM((2, TOP_K, CMB_TC * ROW_SUB, LANES), F32),
            pltpu.SemaphoreType.DMA((2,)),
        ],
        compiler_params=_cparams(("arbitrary",)),
        name="combine",
    )(pos, pos, x1, cw, y3)


def _rotary_lane_tables(seq):
    pos = jnp.arange(seq, dtype=F32)
    inv_freq = 1.0 / (jnp.float32(ROPE_THETA) ** (jnp.arange(0, ROT_DIM, 2, dtype=F32) / ROT_DIM))
    ang = pos[:, None] * inv_freq[None, :]
    cos, sin = jnp.cos(ang), jnp.sin(ang)
    half = ROT_DIM // 2
    lane = jnp.arange(LANES) % QK_DIM
    fidx = lane % half
    first = (lane < half)[None, :]
    second = ((lane >= half) & (lane < ROT_DIM))[None, :]
    cos_t = jnp.where(first | second, cos[:, fidx], 1.0)
    s_up = jnp.where(first, -sin[:, fidx], 0.0)
    s_dn = jnp.where(second, sin[:, fidx], 0.0)
    return cos_t, s_up, s_dn


def kernel(x, attn_norm_g, w_in, gmlp_ln_g, gmlp_ln_b, gmlp_ws, gmlp_bs, q_norm_g, k_norm_g,
           lambda_q1, lambda_k1, lambda_q2, lambda_k2, subln_g, w_out, ffn_norm_g,
           w_group, b_group, w_router, b_router, w_gate, w_up, w_down):
    bsz, seq, d = x.shape
    T = bsz * seq
    assert d == D_MODEL and seq % IN_TM == 0 and seq % ATT_TQ == 0 and T % RT_TN == 0
    l = 0
    x2 = x.reshape(T, d)

    params = jnp.zeros((8, IN_WIDTH), F32)
    params = params.at[0, A_WIDTH:2 * A_WIDTH].set(gmlp_ln_g[l].reshape(-1))
    params = params.at[1, A_WIDTH:2 * A_WIDTH].set(gmlp_ln_b[l].reshape(-1))
    params = params.at[0, 2 * A_WIDTH:3 * A_WIDTH].set(jnp.tile(q_norm_g[l], 2 * B_HEADS))
    params = params.at[0, 3 * A_WIDTH:4 * A_WIDTH].set(jnp.tile(k_norm_g[l], 2 * B_HEADS))
    blk = jnp.arange(IN_GRP) // QK_DIM
    bd = jnp.where(blk[:, None] == blk[None, :], 1.0 / QK_DIM, 0.0).astype(BF16)
    cos_t, s_up, s_dn = _rotary_lane_tables(seq)
    bs_full = jnp.repeat(jnp.transpose(gmlp_bs[l]), HEAD, axis=1)
    lam = (jnp.exp(jnp.sum(lambda_q1[l] * lambda_k1[l])) - jnp.exp(jnp.sum(lambda_q2[l] * lambda_k2[l]))
           + LAMBDA_INIT).reshape(1).astype(F32)
    w_rt = jnp.zeros((LG_ROWS, D_MODEL), F32)
    w_rt = w_rt.at[:N_GROUPS].set(w_group[l].T).at[SUBLANES:].set(w_router[l].T)
    b_rt = jnp.zeros((LG_ROWS, 1), F32)
    b_rt = b_rt.at[:N_GROUPS, 0].set(b_group[l]).at[SUBLANES:, 0].set(b_router[l])
    tri = (jnp.arange(RT_TN)[:, None] < jnp.arange(RT_TN)[None, :]).astype(BF16)

    zp, vb = _in_proj(x2, attn_norm_g[l].reshape(1, d), w_in[l].astype(BF16), params, bd,
                      cos_t, s_up, s_dn, seq)
    out_a = _gmlp(zp, gmlp_ws[l].astype(BF16), bs_full)
    att_steps = bsz * B_HEADS * (seq // ATT_TQ)
    assert (N_EXPERTS * D_MODEL) % att_steps == 0 and (N_EXPERTS * D_EXPERT) % att_steps == 0
    out_b, wg_bf, wu_bf, wd_bf = _attention(
        lam, zp.reshape(bsz, seq, IN_WIDTH - IN_TN), vb.reshape(bsz, seq, IN_TN), subln_g[l].reshape(1, HEAD),
        w_gate[l].reshape(N_EXPERTS * D_MODEL, D_EXPERT), w_up[l].reshape(N_EXPERTS * D_MODEL, D_EXPERT),
        w_down[l].reshape(N_EXPERTS * D_EXPERT, D_MODEL))
    x1, hm3, lg_t = _out_proj(out_a, out_b.reshape(T, A_WIDTH), x2, w_out[l].astype(BF16),
                              ffn_norm_g[l].reshape(1, d), w_rt.astype(BF16), b_rt)

    idr, cw, cnt = _route(lg_t, tri)
    counts = cnt[:, 0]
    padded = (counts + EXP_BLK - 1) // EXP_BLK * EXP_BLK
    pend = jnp.cumsum(padded)
    pstart = (pend - padded).astype(I32)
    n_blocks = (T * TOP_K) // EXP_BLK + N_EXPERTS
    n_rows = n_blocks * EXP_BLK
    block_e = jnp.minimum(jnp.sum(pend[None, :] <= (jnp.arange(n_blocks) * EXP_BLK)[:, None], axis=1),
                          N_EXPERTS - 1).astype(I32)
    n_used = (pend[-1] // EXP_BLK).reshape(1).astype(I32)
    trail = pend[-1] + jnp.arange(N_EXPERTS) * EXP_BLK
    zt = jnp.concatenate([jnp.where(counts > 0, pend - EXP_BLK, -1),
                          jnp.where(trail < n_rows, trail, -1)]).astype(I32)

    is_e = idr[0:TOP_K, None, :] == jnp.arange(N_EXPERTS, dtype=I32)[None, :, None]
    pos = jnp.sum(jnp.where(is_e, pstart[None, :, None], 0), axis=1) + idr[TOP_K:2 * TOP_K]
    xs3 = _dispatch(zt, pos, hm3, n_rows)
    y3 = _experts(block_e, n_used, xs3, wg_bf.reshape(N_EXPERTS, D_MODEL, D_EXPERT),
                  wu_bf.reshape(N_EXPERTS, D_MODEL, D_EXPERT), wd_bf.reshape(N_EXPERTS, D_EXPERT, D_MODEL))
    out = _combine(pos, x1, cw, y3)
    return out.reshape(bsz, seq, d)
```

```python
import functools
import math

import jax
import jax.numpy as jnp
from jax import lax
from jax.experimental import pallas as pl
from jax.experimental.pallas import tpu as pltpu

F32 = jnp.float32
BF16 = jnp.bfloat16
I32 = jnp.int32

D_MODEL = 2048
A_WIDTH = 1024
A_HEADS = 8
HEAD = 128
CHUNK = 128
B_HEADS = 8
QK_DIM = 64
ROT_DIM = 16
ROPE_THETA = 500000.0
IN_WIDTH = 5120
N_GROUPS = 4
EPG = 8
N_EXPERTS = 32
TOP_K = 2
D_EXPERT = 1024
EPS = 1e-6
LAMBDA_INIT = 0.8 - 0.6 * math.exp(-0.3 * 0)

LANES = 128
SUBLANES = 8
ROW_SUB = D_MODEL // LANES
LG_ROWS = SUBLANES * (1 + N_GROUPS)

IN_TM = 512
IN_TN = 1024
IN_GRP = 256
IN_ROWS = 64
GM_TM = 512
ATT_TQ = 1024
ATT_TK = 512
ONES_ROWS = 16
OUT_TM = 256
RT_TN = 512
EXP_BLK = 256
DSP_TC = 512
CMB_TC = 256
DMA_UNROLL = 8
VMEM_LIMIT = 48 * 1024 * 1024


def _cparams(sem):
    return pltpu.CompilerParams(dimension_semantics=sem, vmem_limit_bytes=VMEM_LIMIT)


def _to_token_rows(x):
    rows = x.shape[0]
    slabs = jnp.stack([x[:, s * LANES:(s + 1) * LANES] for s in range(ROW_SUB)], axis=0)
    return jnp.transpose(slabs, (1, 0, 2)).reshape(rows * ROW_SUB, LANES)


def _from_token_rows(x3):
    rows = x3.shape[0] // ROW_SUB
    return jnp.transpose(x3.reshape(rows, ROW_SUB, LANES), (1, 0, 2))


def _in_proj_kernel(x_ref, g_ref, w_ref, p_ref, bd_ref, c_ref, s1_ref, s2_ref, o_ref, v_ref, hn_ref, z_ref):
    j = pl.program_id(1)
    tm = hn_ref.shape[0]
    ngrp = IN_TN // IN_GRP

    @pl.when(j == 0)
    def _():
        x = x_ref[...]
        ms = jnp.mean(x * x, axis=-1, keepdims=True)
        hn_ref[...] = (x * lax.rsqrt(ms + EPS) * g_ref[...]).astype(BF16)

    def project(slot):
        z_ref[slot] = jnp.dot(hn_ref[...], w_ref[...], preferred_element_type=F32)

    def gelu_section(slot):
        for r in range(tm // IN_ROWS):
            rows = slice(r * IN_ROWS, (r + 1) * IN_ROWS)
            o_ref[rows, :] = jax.nn.gelu(z_ref[slot, rows, :]).astype(BF16)

    def gelu_ln_section(slot):
        for s in range(IN_TN // LANES):
            sl = slice(s * LANES, (s + 1) * LANES)
            gz = jax.nn.gelu(z_ref[slot, :, sl])
            mu = jnp.mean(gz, axis=-1, keepdims=True)
            xc = gz - mu
            var = jnp.mean(xc * xc, axis=-1, keepdims=True)
            o_ref[:, sl] = (xc * lax.rsqrt(var + EPS) * p_ref[0:1, sl] + p_ref[1:2, sl]).astype(BF16)

    def qk_section(slot, scale):
        sq = jnp.concatenate([z_ref[slot, :, g * IN_GRP:(g + 1) * IN_GRP] for g in range(ngrp)], axis=0)
        ms = jnp.dot((sq * sq).astype(BF16), bd_ref[...], preferred_element_type=F32)
        for s in range(IN_TN // LANES):
            sl = slice(s * LANES, (s + 1) * LANES)
            g, c = divmod(s * LANES, IN_GRP)
            ms_s = ms[g * tm:(g + 1) * tm, c:c + LANES]
            y = z_ref[slot, :, sl] * lax.rsqrt(ms_s + EPS) * p_ref[0:1, sl]
            r = (y * c_ref[...] + pltpu.roll(y, LANES - ROT_DIM // 2, 1) * s1_ref[...]
                 + pltpu.roll(y, ROT_DIM // 2, 1) * s2_ref[...])
            o_ref[:, sl] = (r * scale if scale != 1.0 else r).astype(BF16)

    @pl.when(j == 0)
    def _():
        project(0)

    @pl.when(j == 1)
    def _():
        gelu_section(0)
        project(1)

    @pl.when(j == 2)
    def _():
        gelu_ln_section(1)
        project(0)

    @pl.when(j == 3)
    def _():
        qk_section(0, math.log2(math.e) / math.sqrt(QK_DIM))
        project(1)

    @pl.when(j == 4)
    def _():
        qk_section(1, 1.0)
        v_ref[...] = jnp.dot(hn_ref[...], w_ref[...], preferred_element_type=F32).astype(BF16)


def _in_proj(x2, g, w_bf, params, bd, cos_t, sup_t, sdn_t, seq):
    T = x2.shape[0]
    spt = seq // IN_TM
    nsec = IN_WIDTH // IN_TN
    assert nsec == 5
    return pl.pallas_call(
        _in_proj_kernel,
        grid=(T // IN_TM, nsec),
        in_specs=[
            pl.BlockSpec((IN_TM, D_MODEL), lambda i, j: (i, 0)),
            pl.BlockSpec((1, D_MODEL), lambda i, j: (0, 0)),
            pl.BlockSpec((D_MODEL, IN_TN), lambda i, j: (0, j)),
            pl.BlockSpec((8, IN_TN), lambda i, j: (0, jnp.maximum(j - 1, 0))),
            pl.BlockSpec((IN_GRP, IN_GRP), lambda i, j: (0, 0)),
            pl.BlockSpec((IN_TM, LANES), lambda i, j: (i % spt, 0)),
            pl.BlockSpec((IN_TM, LANES), lambda i, j: (i % spt, 0)),
            pl.BlockSpec((IN_TM, LANES), lambda i, j: (i % spt, 0)),
        ],
        out_specs=[
            pl.BlockSpec((IN_TM, IN_TN), lambda i, j: (i, jnp.maximum(j - 1, 0))),
            pl.BlockSpec((IN_TM, IN_TN), lambda i, j: (i, 0)),
        ],
        out_shape=[
            jax.ShapeDtypeStruct((T, IN_WIDTH - IN_TN), BF16),
            jax.ShapeDtypeStruct((T, IN_TN), BF16),
        ],
        scratch_shapes=[pltpu.VMEM((IN_TM, D_MODEL), BF16), pltpu.VMEM((2, IN_TM, IN_TN), F32)],
        compiler_params=_cparams(("parallel", "arbitrary")),
        name="in_proj",
    )(x2, g, w_bf, params, bd, cos_t, sup_t, sdn_t)


def _gmlp_kernel(u_ref, v_ref, ws_ref, bs_ref, o_ref):
    for c in range(GM_TM // CHUNK):
        rows = slice(c * CHUNK, (c + 1) * CHUNK)
        for h in range(A_HEADS):
            cols = slice(h * HEAD, (h + 1) * HEAD)
            s = jnp.dot(ws_ref[h], v_ref[rows, cols], preferred_element_type=F32) + bs_ref[:, cols]
            o_ref[rows, cols] = (u_ref[rows, cols].astype(F32) * s).astype(BF16)


def _gmlp(zp, ws_bf, bs_full):
    T = zp.shape[0]
    return pl.pallas_call(
        _gmlp_kernel,
        grid=(T // GM_TM,),
        in_specs=[
            pl.BlockSpec((GM_TM, A_WIDTH), lambda i: (i, 0)),
            pl.BlockSpec((GM_TM, A_WIDTH), lambda i: (i, 1)),
            pl.BlockSpec((A_HEADS, CHUNK, CHUNK), lambda i: (0, 0, 0)),
            pl.BlockSpec((CHUNK, A_WIDTH), lambda i: (0, 0)),
        ],
        out_specs=pl.BlockSpec((GM_TM, A_WIDTH), lambda i: (i, 0)),
        out_shape=jax.ShapeDtypeStruct((T, A_WIDTH), BF16),
        compiler_params=_cparams(("parallel",)),
        name="gmlp",
    )(zp, zp, ws_bf, bs_full)


def _attn_kernel(n_tiles, nq, lam_ref, qc_ref, qn_ref, k_ref, v_ref, g_ref, wg_ref, wu_ref, wd_ref,
                 o_ref, wgb_ref, wub_ref, wdb_ref, vt_ref, qt_ref, acc_ref):
    l = pl.program_id(0)
    slot = l % 2
    seq = k_ref.shape[0]
    nc = seq // ATT_TK
    tq = qc_ref.shape[0]

    def prep_q(q_ref, s):
        q_t = q_ref[...].astype(F32).T
        row = lax.broadcasted_iota(I32, q_t.shape, 0)
        qt_ref[s, 0] = jnp.where(row < QK_DIM, q_t, 0.0).astype(BF16)
        qt_ref[s, 1] = jnp.where(row >= QK_DIM, q_t, 0.0).astype(BF16)

    def finish(s):
        a1, a2 = acc_ref[s, 0], acc_ref[s, 1]
        o_t = (a1[0:HEAD] / a1[HEAD:HEAD + 1]
               - lam_ref[0] * (a2[0:HEAD] / a2[HEAD:HEAD + 1]))
        o = o_t.T
        ms = jnp.mean(o * o, axis=-1, keepdims=True)
        o_ref[...] = (o * lax.rsqrt(ms + EPS) * g_ref[...] * (1.0 - LAMBDA_INIT)).astype(BF16)

    def scores(c, qh_t):
        return jnp.dot(k_ref[c * ATT_TK:(c + 1) * ATT_TK, :], qh_t, preferred_element_type=F32)

    def update(c, s_t, m, acc):
        m_new = jnp.maximum(m, jnp.max(s_t, axis=0, keepdims=True))
        p = jnp.exp2(s_t - m_new).astype(BF16)
        pv = jnp.dot(vt_ref[:, c * ATT_TK:(c + 1) * ATT_TK], p, preferred_element_type=F32)
        return m_new, jnp.exp2(m - m_new) * acc + pv

    def attend(s):
        q1_t, q2_t = qt_ref[s, 0], qt_ref[s, 1]
        m1 = m2 = jnp.full((1, tq), -jnp.inf, F32)
        a1 = a2 = jnp.zeros((HEAD + ONES_ROWS, tq), F32)
        s1, s2 = scores(0, q1_t), scores(0, q2_t)
        for c in range(nc):
            if c + 1 < nc:
                n1, n2 = scores(c + 1, q1_t), scores(c + 1, q2_t)
            m1, a1 = update(c, s1, m1, a1)
            m2, a2 = update(c, s2, m2, a2)
            if c + 1 < nc:
                s1, s2 = n1, n2
        acc_ref[s, 0] = a1
        acc_ref[s, 1] = a2

    @pl.when(l == 0)
    def _():
        prep_q(qc_ref, 0)
        acc_ref[1] = jnp.ones(acc_ref.shape[1:], F32)

    @pl.when((l < n_tiles) & (l % nq == 0))
    def _():
        for c in range(nc):
            cols = slice(c * ATT_TK, (c + 1) * ATT_TK)
            vt_ref[0:HEAD, cols] = v_ref[cols, :].astype(F32).T.astype(BF16)
        vt_ref[HEAD:, :] = jnp.ones((ONES_ROWS, seq), BF16)

    @pl.when(l < n_tiles)
    def _():
        finish(1 - slot)
        prep_q(qn_ref, 1 - slot)
        wgb_ref[...] = wg_ref[...].astype(BF16)
        wub_ref[...] = wu_ref[...].astype(BF16)
        wdb_ref[...] = wd_ref[...].astype(BF16)
        attend(slot)

    @pl.when(l == n_tiles)
    def _():
        finish(1 - slot)


def _attention(lam, zp3, v3, subln_g, wg2, wu2, wd2):
    bsz, seq, _ = zp3.shape
    nq = seq // ATT_TQ
    steps = bsz * B_HEADS * nq
    qb, kb, vb = (2 * A_WIDTH) // HEAD, (2 * A_WIDTH + 1024) // HEAD, 0

    def tile(t):
        return t // (B_HEADS * nq), (t // nq) % B_HEADS, t % nq

    def cur(l):
        return jnp.minimum(l, steps - 1)

    def q_map(l, lam, ahead):
        b, h, i = tile(jnp.minimum(l + ahead, steps - 1))
        return b, i, qb + h

    def kv_map(l, lam, base):
        b, h, _ = tile(cur(l))
        return b, 0, base + h

    def o_map(l, lam):
        b, h, i = tile(jnp.maximum(l - 1, 0))
        return b, i, h

    def w_spec(w):
        rows = w.shape[0] // steps
        return pl.BlockSpec((rows, w.shape[1]), lambda l, lam: (cur(l), 0))

    w_specs = [w_spec(wg2), w_spec(wu2), w_spec(wd2)]
    grid_spec = pltpu.PrefetchScalarGridSpec(
        num_scalar_prefetch=1,
        grid=(steps + 1,),
        in_specs=[
            pl.BlockSpec((None, ATT_TQ, HEAD), functools.partial(q_map, ahead=0)),
            pl.BlockSpec((None, ATT_TQ, HEAD), functools.partial(q_map, ahead=1)),
            pl.BlockSpec((None, seq, HEAD), functools.partial(kv_map, base=kb)),
            pl.BlockSpec((None, seq, HEAD), functools.partial(kv_map, base=vb)),
            pl.BlockSpec((1, HEAD), lambda l, lam: (0, 0)),
        ] + w_specs,
        out_specs=[pl.BlockSpec((None, ATT_TQ, HEAD), o_map)] + w_specs,
        scratch_shapes=[
            pltpu.VMEM((HEAD + ONES_ROWS, seq), BF16),
            pltpu.VMEM((2, 2, HEAD, ATT_TQ), BF16),
            pltpu.VMEM((2, 2, HEAD + ONES_ROWS, ATT_TQ), F32),
        ],
    )
    return pl.pallas_call(
        functools.partial(_attn_kernel, steps, nq),
        grid_spec=grid_spec,
        out_shape=[jax.ShapeDtypeStruct((bsz, seq, B_HEADS * HEAD), BF16)]
                  + [jax.ShapeDtypeStruct(w.shape, BF16) for w in (wg2, wu2, wd2)],
        compiler_params=_cparams(("arbitrary",)),
        name="attn",
    )(lam, zp3, zp3, zp3, v3, subln_g, wg2, wu2, wd2)


def _out_proj_kernel(n_tiles, a_ref, b_ref, x_ref, wa_ref, wb_ref, g_ref, wr_ref, br_ref,
                     x1_ref, hm_ref, lg_ref, hbuf):
    i = pl.program_id(0)
    slot = i % 2

    def project(s):
        acc = (jnp.dot(a_ref[...], wa_ref[...], preferred_element_type=F32)
               + jnp.dot(b_ref[...], wb_ref[...], preferred_element_type=F32))
        x1 = x_ref[...] + acc
        x1_ref[...] = x1
        ms = jnp.mean(x1 * x1, axis=-1, keepdims=True)
        hbuf[s] = x1 * lax.rsqrt(ms + EPS) * g_ref[...]

    def emit(s):
        hm = hbuf[s]
        hm_ref[...] = _to_token_rows(hm)
        lg_ref[...] = lax.dot_general(wr_ref[...], hm.astype(BF16), (((1,), (1,)), ((), ())),
                                      preferred_element_type=F32) + br_ref[...]

    @pl.when(i == 0)
    def _():
        hbuf[1] = jnp.zeros(hbuf.shape[1:], F32)

    @pl.when(i < n_tiles)
    def _():
        emit(1 - slot)
        project(slot)

    @pl.when(i == n_tiles)
    def _():
        emit(1 - slot)


def _out_proj(out_a, out_b, x2, w_out_bf, g, w_rt, b_rt):
    T = x2.shape[0]
    n_tiles = T // OUT_TM

    def cur(i):
        return jnp.minimum(i, n_tiles - 1)

    def prev(i):
        return jnp.maximum(i - 1, 0)

    return pl.pallas_call(
        functools.partial(_out_proj_kernel, n_tiles),
        grid=(n_tiles + 1,),
        in_specs=[
            pl.BlockSpec((OUT_TM, A_WIDTH), lambda i: (cur(i), 0)),
            pl.BlockSpec((OUT_TM, A_WIDTH), lambda i: (cur(i), 0)),
            pl.BlockSpec((OUT_TM, D_MODEL), lambda i: (cur(i), 0)),
            pl.BlockSpec((A_WIDTH, D_MODEL), lambda i: (0, 0)),
            pl.BlockSpec((A_WIDTH, D_MODEL), lambda i: (1, 0)),
            pl.BlockSpec((1, D_MODEL), lambda i: (0, 0)),
            pl.BlockSpec((LG_ROWS, D_MODEL), lambda i: (0, 0)),
            pl.BlockSpec((LG_ROWS, 1), lambda i: (0, 0)),
        ],
        out_specs=[
            pl.BlockSpec((OUT_TM, D_MODEL), lambda i: (cur(i), 0)),
            pl.BlockSpec((OUT_TM * ROW_SUB, LANES), lambda i: (prev(i), 0)),
            pl.BlockSpec((LG_ROWS, OUT_TM), lambda i: (0, prev(i))),
        ],
        out_shape=[
            jax.ShapeDtypeStruct((T, D_MODEL), F32),
            jax.ShapeDtypeStruct((T * ROW_SUB, LANES), F32),
            jax.ShapeDtypeStruct((LG_ROWS, T), F32),
        ],
        scratch_shapes=[pltpu.VMEM((2, OUT_TM, D_MODEL), F32)],
        compiler_params=_cparams(("arbitrary",)),
        name="out_proj",
    )(out_a, out_b, x2, w_out_bf, w_out_bf, g, w_rt, b_rt)


def _route_kernel(lg_ref, tri_ref, idr_ref, cw_ref, cnt_ref, carry_ref):
    i = pl.program_id(0)

    @pl.when(i == 0)
    def _():
        carry_ref[...] = jnp.zeros_like(carry_ref)

    tn = lg_ref.shape[1]
    row = lax.broadcasted_iota(I32, (SUBLANES, tn), 0)
    neg = jnp.float32(-jnp.inf)

    def top1(v):
        mx = jnp.max(v, axis=0, keepdims=True)
        idx = jnp.min(jnp.where(v == mx, row, SUBLANES), axis=0, keepdims=True)
        return mx, idx

    g = jnp.where(row < N_GROUPS, lg_ref[0:SUBLANES, :], neg)
    gmax, gidx = top1(g)
    g_w = 1.0 / jnp.sum(jnp.exp(g - gmax), axis=0, keepdims=True)
    sel = lg_ref[SUBLANES * N_GROUPS:SUBLANES * (N_GROUPS + 1), :]
    for grp in range(N_GROUPS - 2, -1, -1):
        sel = jnp.where(gidx == grp, lg_ref[SUBLANES * (grp + 1):SUBLANES * (grp + 2), :], sel)
    e1, i1 = top1(sel)
    e2, i2 = top1(jnp.where(row == i1, neg, sel))
    d = jnp.exp(e2 - e1)
    w1 = 1.0 / (1.0 + d)
    w2 = d * w1
    id0 = gidx * EPG + i1
    id1 = gidx * EPG + i2

    erow = lax.broadcasted_iota(I32, (N_EXPERTS, tn), 0)
    o0 = erow == id0
    o1 = erow == id1
    occ = jnp.where(o0 | o1, 1.0, 0.0).astype(F32)
    before = jnp.dot(occ.astype(BF16), tri_ref[...], preferred_element_type=F32) + carry_ref[:, 0:1]
    r0 = jnp.sum(jnp.where(o0, before, 0.0), axis=0, keepdims=True)
    r1 = jnp.sum(jnp.where(o1, before, 0.0), axis=0, keepdims=True)
    total = carry_ref[...] + jnp.sum(occ, axis=1, keepdims=True)
    carry_ref[...] = total
    cnt_ref[...] = total.astype(I32)

    zi = jnp.zeros((SUBLANES - 4, tn), I32)
    idr_ref[...] = jnp.concatenate([id0, id1, r0.astype(I32), r1.astype(I32), zi], axis=0)
    zf = jnp.zeros((SUBLANES - 2, tn), F32)
    cw_ref[...] = jnp.concatenate([g_w * w1, g_w * w2, zf], axis=0)


def _route(lg_t, tri):
    T = lg_t.shape[1]
    return pl.pallas_call(
        _route_kernel,
        grid=(T // RT_TN,),
        in_specs=[
            pl.BlockSpec((LG_ROWS, RT_TN), lambda i: (0, i)),
            pl.BlockSpec((RT_TN, RT_TN), lambda i: (0, 0)),
        ],
        out_specs=[
            pl.BlockSpec((SUBLANES, RT_TN), lambda i: (0, i)),
            pl.BlockSpec((SUBLANES, RT_TN), lambda i: (0, i)),
            pl.BlockSpec((N_EXPERTS, LANES), lambda i: (0, 0)),
        ],
        out_shape=[
            jax.ShapeDtypeStruct((SUBLANES, T), I32),
            jax.ShapeDtypeStruct((SUBLANES, T), F32),
            jax.ShapeDtypeStruct((N_EXPERTS, LANES), I32),
        ],
        scratch_shapes=[pltpu.VMEM((N_EXPERTS, LANES), F32)],
        compiler_params=_cparams(("arbitrary",)),
        name="route",
    )(lg_t, tri)


def _row_copy(src_ref, src_row, dst_ref, dst_row, sem):
    return pltpu.make_async_copy(src_ref.at[pl.ds(src_row * ROW_SUB, ROW_SUB)],
                                 dst_ref.at[pl.ds(dst_row * ROW_SUB, ROW_SUB)], sem)


def _dispatch_kernel(zt_ref, pos_ref, hm_ref, xs_ref, zbuf, sem):
    i = pl.program_id(0)

    def zero_copy(e):
        return pltpu.make_async_copy(
            zbuf, xs_ref.at[pl.ds(pl.multiple_of(zt_ref[e] * ROW_SUB, ROW_SUB), EXP_BLK * ROW_SUB)], sem)

    @pl.when(i == 0)
    def _():
        zbuf[...] = jnp.zeros_like(zbuf)
        for e in range(2 * N_EXPERTS):
            @pl.when(zt_ref[e] >= 0)
            def _():
                zero_copy(e).start()
        for e in range(2 * N_EXPERTS):
            @pl.when(zt_ref[e] >= 0)
            def _():
                zero_copy(e).wait()

    def issue(grp, carry):
        for u in range(DMA_UNROLL):
            t = grp * DMA_UNROLL + u
            for k in range(TOP_K):
                _row_copy(hm_ref, t, xs_ref, pos_ref[k, t], sem).start()
        return carry

    lax.fori_loop(0, DSP_TC // DMA_UNROLL, issue, 0)

    def drain(grp, carry):
        for _ in range(DMA_UNROLL * TOP_K):
            _row_copy(hm_ref, 0, xs_ref, 0, sem).wait()
        return carry

    lax.fori_loop(0, DSP_TC // DMA_UNROLL, drain, 0)


def _dispatch(zt, pos, hm3, n_rows):
    T = pos.shape[1]
    grid_spec = pltpu.PrefetchScalarGridSpec(
        num_scalar_prefetch=1,
        grid=(T // DSP_TC,),
        in_specs=[
            pl.BlockSpec((TOP_K, DSP_TC), lambda i, zt: (0, i), memory_space=pltpu.SMEM),
            pl.BlockSpec((DSP_TC * ROW_SUB, LANES), lambda i, zt: (i, 0)),
        ],
        out_specs=pl.BlockSpec(memory_space=pl.ANY),
        scratch_shapes=[pltpu.VMEM((EXP_BLK * ROW_SUB, LANES), F32), pltpu.SemaphoreType.DMA(())],
    )
    return pl.pallas_call(
        _dispatch_kernel,
        grid_spec=grid_spec,
        out_shape=jax.ShapeDtypeStruct((n_rows * ROW_SUB, LANES), F32),
        compiler_params=_cparams(("arbitrary",)),
        name="dispatch",
    )(zt, pos, hm3)


def _expert_kernel(be_ref, first_ref, slot_ref, nxt_ref, nu_ref, xs_ref, wg_hbm, wu_hbm, wd_hbm,
                   y_ref, xb_ref, wg_buf, wu_buf, wd_buf, sems):
    b = pl.program_id(0)

    def weight_copies(e, s):
        return [pltpu.make_async_copy(hbm.at[e], buf.at[s], sems.at[s])
                for hbm, buf in ((wg_hbm, wg_buf), (wu_hbm, wu_buf), (wd_hbm, wd_buf))]

    @pl.when(b < nu_ref[0])
    def _():
        s = slot_ref[b]

        @pl.when(b == 0)
        def _():
            for cp in weight_copies(be_ref[0], 0):
                cp.start()

        @pl.when(first_ref[b] == 1)
        def _():
            for cp in weight_copies(be_ref[b], s):
                cp.wait()

            @pl.when(nxt_ref[b] >= 0)
            def _():
                for cp in weight_copies(nxt_ref[b], 1 - s):
                    cp.start()

        slabs = _from_token_rows(xs_ref[...])
        for k in range(ROW_SUB):
            xb_ref[:, k * LANES:(k + 1) * LANES] = slabs[k].astype(BF16)
        x = xb_ref[...]
        g = jnp.dot(x, wg_buf[s], preferred_element_type=F32)
        u = jnp.dot(x, wu_buf[s], preferred_element_type=F32)
        a = (jax.nn.silu(g) * u).astype(BF16)
        y = jnp.dot(a, wd_buf[s], preferred_element_type=F32)
        y_ref[...] = _to_token_rows(y)

    @pl.when(b >= nu_ref[0])
    def _():
        y_ref[...] = jnp.zeros_like(y_ref)


def _experts(block_e, first, slot, nxt, n_used, xs3, wg, wu, wd):
    n_blocks = xs3.shape[0] // (EXP_BLK * ROW_SUB)

    def xs_map(b, be, first, slot, nxt, nu):
        return (jnp.minimum(b, nu[0] - 1), 0)

    grid_spec = pltpu.PrefetchScalarGridSpec(
        num_scalar_prefetch=5,
        grid=(n_blocks,),
        in_specs=[
            pl.BlockSpec((EXP_BLK * ROW_SUB, LANES), xs_map),
            pl.BlockSpec(memory_space=pl.ANY),
            pl.BlockSpec(memory_space=pl.ANY),
            pl.BlockSpec(memory_space=pl.ANY),
        ],
        out_specs=pl.BlockSpec((EXP_BLK * ROW_SUB, LANES), lambda b, *_: (b, 0)),
        scratch_shapes=[
            pltpu.VMEM((EXP_BLK, D_MODEL), BF16),
            pltpu.VMEM((2, D_MODEL, D_EXPERT), BF16),
            pltpu.VMEM((2, D_MODEL, D_EXPERT), BF16),
            pltpu.VMEM((2, D_EXPERT, D_MODEL), BF16),
            pltpu.SemaphoreType.DMA((2,)),
        ],
    )
    return pl.pallas_call(
        _expert_kernel,
        grid_spec=grid_spec,
        out_shape=jax.ShapeDtypeStruct(xs3.shape, F32),
        compiler_params=_cparams(("arbitrary",)),
        name="experts",
    )(block_e, first, slot, nxt, n_used, xs3, wg, wu, wd)


def _combine_kernel(pos_ref, posn_ref, x1_ref, cw_ref, y_ref, o_ref, yb, sems):
    i = pl.program_id(0)
    slot = i % 2

    def gather(ids_ref, slot_):
        def issue(grp, carry):
            for u in range(DMA_UNROLL):
                t = grp * DMA_UNROLL + u
                for k in range(TOP_K):
                    _row_copy(y_ref, ids_ref[k, t], yb.at[slot_, k], t, sems.at[slot_]).start()
            return carry

        lax.fori_loop(0, CMB_TC // DMA_UNROLL, issue, 0)

    @pl.when(i == 0)
    def _():
        gather(pos_ref, 0)

    @pl.when(i + 1 < pl.num_programs(0))
    def _():
        gather(posn_ref, 1 - slot)

    def drain(grp, carry):
        for _ in range(DMA_UNROLL * TOP_K):
            _row_copy(y_ref, 0, yb.at[slot, 0], 0, sems.at[slot]).wait()
        return carry

    lax.fori_loop(0, CMB_TC // DMA_UNROLL, drain, 0)
    w_t = cw_ref[...].T
    w0 = w_t[:, 0:1]
    w1 = w_t[:, 1:2]
    y0 = _from_token_rows(yb[slot, 0])
    y1 = _from_token_rows(yb[slot, 1])
    for s in range(ROW_SUB):
        sl = slice(s * LANES, (s + 1) * LANES)
        o_ref[:, sl] = x1_ref[:, sl] + w0 * y0[s] + w1 * y1[s]


def _combine(pos, x1, cw, y3):
    T = x1.shape[0]
    steps = T // CMB_TC
    return pl.pallas_call(
        _combine_kernel,
        grid=(steps,),
        in_specs=[
            pl.BlockSpec((TOP_K, CMB_TC), lambda i: (0, i), memory_space=pltpu.SMEM),
            pl.BlockSpec((TOP_K, CMB_TC), lambda i: (0, jnp.minimum(i + 1, steps - 1)),
                         memory_space=pltpu.SMEM),
            pl.BlockSpec((CMB_TC, D_MODEL), lambda i: (i, 0)),
            pl.BlockSpec((SUBLANES, CMB_TC), lambda i: (0, i)),
            pl.BlockSpec(memory_space=pl.ANY),
        ],
        out_specs=pl.BlockSpec((CMB_TC, D_MODEL), lambda i: (i, 0)),
        out_shape=jax.ShapeDtypeStruct((T, D_MODEL), F32),
        scratch_shapes=[
            pltpu.VMEM((2, TOP_K, CMB_TC * ROW_SUB, LANES), F32),
            pltpu.SemaphoreType.DMA((2,)),
        ],
        compiler_params=_cparams(("arbitrary",)),
        name="combine",
    )(pos, pos, x1, cw, y3)


def _rotary_lane_tables(seq):
    pos = jnp.arange(seq, dtype=F32)
    inv_freq = 1.0 / (jnp.float32(ROPE_THETA) ** (jnp.arange(0, ROT_DIM, 2, dtype=F32) / ROT_DIM))
    ang = pos[:, None] * inv_freq[None, :]
    cos, sin = jnp.cos(ang), jnp.sin(ang)
    half = ROT_DIM // 2
    lane = jnp.arange(LANES) % QK_DIM
    fidx = lane % half
    first = (lane < half)[None, :]
    second = ((lane >= half) & (lane < ROT_DIM))[None, :]
    cos_t = jnp.where(first | second, cos[:, fidx], 1.0)
    s_up = jnp.where(first, -sin[:, fidx], 0.0)
    s_dn = jnp.where(second, sin[:, fidx], 0.0)
    return cos_t, s_up, s_dn


def kernel(x, attn_norm_g, w_in, gmlp_ln_g, gmlp_ln_b, gmlp_ws, gmlp_bs, q_norm_g, k_norm_g,
           lambda_q1, lambda_k1, lambda_q2, lambda_k2, subln_g, w_out, ffn_norm_g,
           w_group, b_group, w_router, b_router, w_gate, w_up, w_down):
    bsz, seq, d = x.shape
    T = bsz * seq
    assert d == D_MODEL and seq % IN_TM == 0 and seq % ATT_TQ == 0 and T % RT_TN == 0
    l = 0
    x2 = x.reshape(T, d)

    params = jnp.zeros((8, IN_WIDTH), F32)
    params = params.at[0, A_WIDTH:2 * A_WIDTH].set(gmlp_ln_g[l].reshape(-1))
    params = params.at[1, A_WIDTH:2 * A_WIDTH].set(gmlp_ln_b[l].reshape(-1))
    params = params.at[0, 2 * A_WIDTH:3 * A_WIDTH].set(jnp.tile(q_norm_g[l], 2 * B_HEADS))
    params = params.at[0, 3 * A_WIDTH:4 * A_WIDTH].set(jnp.tile(k_norm_g[l], 2 * B_HEADS))
    blk = jnp.arange(IN_GRP) // QK_DIM
    bd = jnp.where(blk[:, None] == blk[None, :], 1.0 / QK_DIM, 0.0).astype(BF16)
    cos_t, s_up, s_dn = _rotary_lane_tables(seq)
    bs_full = jnp.repeat(jnp.transpose(gmlp_bs[l]), HEAD, axis=1)
    lam = (jnp.exp(jnp.sum(lambda_q1[l] * lambda_k1[l])) - jnp.exp(jnp.sum(lambda_q2[l] * lambda_k2[l]))
           + LAMBDA_INIT).reshape(1).astype(F32)
    w_rt = jnp.zeros((LG_ROWS, D_MODEL), F32)
    w_rt = w_rt.at[:N_GROUPS].set(w_group[l].T).at[SUBLANES:].set(w_router[l].T)
    b_rt = jnp.zeros((LG_ROWS, 1), F32)
    b_rt = b_rt.at[:N_GROUPS, 0].set(b_group[l]).at[SUBLANES:, 0].set(b_router[l])
    tri = (jnp.arange(RT_TN)[:, None] < jnp.arange(RT_TN)[None, :]).astype(BF16)

    zp, vb = _in_proj(x2, attn_norm_g[l].reshape(1, d), w_in[l].astype(BF16), params, bd,
                      cos_t, s_up, s_dn, seq)
    out_a = _gmlp(zp, gmlp_ws[l].astype(BF16), bs_full)
    att_steps = bsz * B_HEADS * (seq // ATT_TQ)
    assert (N_EXPERTS * D_MODEL) % att_steps == 0 and (N_EXPERTS * D_EXPERT) % att_steps == 0
    out_b, wg_bf, wu_bf, wd_bf = _attention(
        lam, zp.reshape(bsz, seq, IN_WIDTH - IN_TN), vb.reshape(bsz, seq, IN_TN), subln_g[l].reshape(1, HEAD),
        w_gate[l].reshape(N_EXPERTS * D_MODEL, D_EXPERT), w_up[l].reshape(N_EXPERTS * D_MODEL, D_EXPERT),
        w_down[l].reshape(N_EXPERTS * D_EXPERT, D_MODEL))
    x1, hm3, lg_t = _out_proj(out_a, out_b.reshape(T, A_WIDTH), x2, w_out[l].astype(BF16),
                              ffn_norm_g[l].reshape(1, d), w_rt.astype(BF16), b_rt)

    idr, cw, cnt = _route(lg_t, tri)
    counts = cnt[:, 0]
    padded = (counts + EXP_BLK - 1) // EXP_BLK * EXP_BLK
    pend = jnp.cumsum(padded)
    pstart = (pend - padded).astype(I32)
    n_blocks = (T * TOP_K) // EXP_BLK + N_EXPERTS
    n_rows = n_blocks * EXP_BLK
    block_e = jnp.minimum(jnp.sum(pend[None, :] <= (jnp.arange(n_blocks) * EXP_BLK)[:, None], axis=1),
                          N_EXPERTS - 1).astype(I32)
    n_used = (pend[-1] // EXP_BLK).reshape(1).astype(I32)
    eids = jnp.arange(N_EXPERTS, dtype=I32)
    used = jnp.arange(n_blocks) < n_used[0]
    first = ((block_e != jnp.concatenate([jnp.full((1,), -1, I32), block_e[:-1]])) & used).astype(I32)
    slot = ((jnp.cumsum(first) - 1) % 2).astype(I32)
    later = (eids[None, :] > eids[:, None]) & (counts > 0)[None, :]
    nxt_of = jnp.min(jnp.where(later, eids[None, :], N_EXPERTS), axis=1)
    nxt_of = jnp.where(nxt_of == N_EXPERTS, -1, nxt_of)
    nxt = jnp.sum(jnp.where(block_e[:, None] == eids[None, :], nxt_of[None, :], 0), axis=1).astype(I32)
    trail =pend[-1] + jnp.arange(N_EXPERTS) * EXP_BLK
    zt = jnp.concatenate([jnp.where(counts > 0, pend - EXP_BLK, -1),
                          jnp.where(trail < n_rows, trail, -1)]).astype(I32)

    is_e = idr[0:TOP_K, None, :] == jnp.arange(N_EXPERTS, dtype=I32)[None, :, None]
    pos = jnp.sum(jnp.where(is_e, pstart[None, :, None], 0), axis=1) + idr[TOP_K:2 * TOP_K]
    xs3 = _dispatch(zt, pos, hm3, n_rows)
    y3 = _experts(block_e, first, slot, nxt, n_used, xs3, wg_bf.reshape(N_EXPERTS, D_MODEL, D_EXPERT),
                  wu_bf.reshape(N_EXPERTS, D_MODEL, D_EXPERT), wd_bf.reshape(N_EXPERTS, D_EXPERT, D_MODEL))
    out = _combine(pos, x1, cw, y3)
    return out.reshape(bsz, seq, d)
```

```python
import functools
import math

import jax
import jax.numpy as jnp
from jax import lax
from jax.experimental import pallas as pl
from jax.experimental.pallas import tpu as pltpu

F32 = jnp.float32
BF16 = jnp.bfloat16
I32 = jnp.int32

D_MODEL = 2048
A_WIDTH = 1024
A_HEADS = 8
HEAD = 128
CHUNK = 128
B_HEADS = 8
QK_DIM = 64
ROT_DIM = 16
ROPE_THETA = 500000.0
IN_WIDTH = 5120
N_GROUPS = 4
EPG = 8
N_EXPERTS = 32
TOP_K = 2
D_EXPERT = 1024
EPS = 1e-6
LAMBDA_INIT = 0.8 - 0.6 * math.exp(-0.3 * 0)

LANES = 128
SUBLANES = 8
ROW_SUB = D_MODEL // LANES
ROW_DT = jnp.bfloat16
LG_ROWS = SUBLANES * (1 + N_GROUPS)

IN_TM = 512
IN_TN = 1024
IN_GRP = 256
IN_ROWS = 64
GM_TM = 512
ATT_TQ = 1024
ATT_TK = 512
ONES_ROWS = 16
OUT_TM = 256
RT_TN = 512
EXP_BLK = 256
DSP_TC = 512
CMB_TC = 256
DMA_UNROLL = 8
VMEM_LIMIT = 48 * 1024 * 1024


def _cparams(sem):
    return pltpu.CompilerParams(dimension_semantics=sem, vmem_limit_bytes=VMEM_LIMIT)


def _to_token_rows(x):
    rows = x.shape[0]
    slabs = jnp.stack([x[:, s * LANES:(s + 1) * LANES] for s in range(ROW_SUB)], axis=0)
    return jnp.transpose(slabs, (1, 0, 2)).reshape(rows * ROW_SUB, LANES)


def _from_token_rows(x3):
    rows = x3.shape[0] // ROW_SUB
    return jnp.transpose(x3.reshape(rows, ROW_SUB, LANES), (1, 0, 2))


def _in_proj_kernel(x_ref, g_ref, w_ref, p_ref, bd_ref, c_ref, s1_ref, s2_ref, o_ref, v_ref, hn_ref, z_ref):
    j = pl.program_id(1)
    tm = hn_ref.shape[0]
    ngrp = IN_TN // IN_GRP

    @pl.when(j == 0)
    def _():
        x = x_ref[...]
        ms = jnp.mean(x * x, axis=-1, keepdims=True)
        hn_ref[...] = (x * lax.rsqrt(ms + EPS) * g_ref[...]).astype(BF16)

    def project(slot):
        z_ref[slot] = jnp.dot(hn_ref[...], w_ref[...], preferred_element_type=F32)

    def gelu_section(slot):
        for r in range(tm // IN_ROWS):
            rows = slice(r * IN_ROWS, (r + 1) * IN_ROWS)
            o_ref[rows, :] = jax.nn.gelu(z_ref[slot, rows, :]).astype(BF16)

    def gelu_ln_section(slot):
        for s in range(IN_TN // LANES):
            sl = slice(s * LANES, (s + 1) * LANES)
            gz = jax.nn.gelu(z_ref[slot, :, sl])
            mu = jnp.mean(gz, axis=-1, keepdims=True)
            xc = gz - mu
            var = jnp.mean(xc * xc, axis=-1, keepdims=True)
            o_ref[:, sl] = (xc * lax.rsqrt(var + EPS) * p_ref[0:1, sl] + p_ref[1:2, sl]).astype(BF16)

    def qk_section(slot, scale):
        sq = jnp.concatenate([z_ref[slot, :, g * IN_GRP:(g + 1) * IN_GRP] for g in range(ngrp)], axis=0)
        ms = jnp.dot((sq * sq).astype(BF16), bd_ref[...], preferred_element_type=F32)
        for s in range(IN_TN // LANES):
            sl = slice(s * LANES, (s + 1) * LANES)
            g, c = divmod(s * LANES, IN_GRP)
            ms_s = ms[g * tm:(g + 1) * tm, c:c + LANES]
            y = z_ref[slot, :, sl] * lax.rsqrt(ms_s + EPS) * p_ref[0:1, sl]
            r = (y * c_ref[...] + pltpu.roll(y, LANES - ROT_DIM // 2, 1) * s1_ref[...]
                 + pltpu.roll(y, ROT_DIM // 2, 1) * s2_ref[...])
            o_ref[:, sl] = (r * scale if scale != 1.0 else r).astype(BF16)

    @pl.when(j == 0)
    def _():
        project(0)

    @pl.when(j == 1)
    def _():
        gelu_section(0)
        project(1)

    @pl.when(j == 2)
    def _():
        gelu_ln_section(1)
        project(0)

    @pl.when(j == 3)
    def _():
        qk_section(0, math.log2(math.e) / math.sqrt(QK_DIM))
        project(1)

    @pl.when(j == 4)
    def _():
        qk_section(1, 1.0)
        v_ref[...] = jnp.dot(hn_ref[...], w_ref[...], preferred_element_type=F32).astype(BF16)


def _in_proj(x2, g, w_bf, params, bd, cos_t, sup_t, sdn_t, seq):
    T = x2.shape[0]
    spt = seq // IN_TM
    nsec = IN_WIDTH // IN_TN
    assert nsec == 5
    return pl.pallas_call(
        _in_proj_kernel,
        grid=(T // IN_TM, nsec),
        in_specs=[
            pl.BlockSpec((IN_TM, D_MODEL), lambda i, j: (i, 0)),
            pl.BlockSpec((1, D_MODEL), lambda i, j: (0, 0)),
            pl.BlockSpec((D_MODEL, IN_TN), lambda i, j: (0, j)),
            pl.BlockSpec((8, IN_TN), lambda i, j: (0, jnp.maximum(j - 1, 0))),
            pl.BlockSpec((IN_GRP, IN_GRP), lambda i, j: (0, 0)),
            pl.BlockSpec((IN_TM, LANES), lambda i, j: (i % spt, 0)),
            pl.BlockSpec((IN_TM, LANES), lambda i, j: (i % spt, 0)),
            pl.BlockSpec((IN_TM, LANES), lambda i, j: (i % spt, 0)),
        ],
        out_specs=[
            pl.BlockSpec((IN_TM, IN_TN), lambda i, j: (i, jnp.maximum(j - 1, 0))),
            pl.BlockSpec((IN_TM, IN_TN), lambda i, j: (i, 0)),
        ],
        out_shape=[
            jax.ShapeDtypeStruct((T, IN_WIDTH - IN_TN), BF16),
            jax.ShapeDtypeStruct((T, IN_TN), BF16),
        ],
        scratch_shapes=[pltpu.VMEM((IN_TM, D_MODEL), BF16), pltpu.VMEM((2, IN_TM, IN_TN), F32)],
        compiler_params=_cparams(("parallel", "arbitrary")),
        name="in_proj",
    )(x2, g, w_bf, params, bd, cos_t, sup_t, sdn_t)


def _gmlp_kernel(u_ref, v_ref, ws_ref, bs_ref, o_ref):
    for c in range(GM_TM // CHUNK):
        rows = slice(c * CHUNK, (c + 1) * CHUNK)
        for h in range(A_HEADS):
            cols = slice(h * HEAD, (h + 1) * HEAD)
            s = jnp.dot(ws_ref[h], v_ref[rows, cols], preferred_element_type=F32) + bs_ref[:, cols]
            o_ref[rows, cols] = (u_ref[rows, cols].astype(F32) * s).astype(BF16)


def _gmlp(zp, ws_bf, bs_full):
    T = zp.shape[0]
    return pl.pallas_call(
        _gmlp_kernel,
        grid=(T // GM_TM,),
        in_specs=[
            pl.BlockSpec((GM_TM, A_WIDTH), lambda i: (i, 0)),
            pl.BlockSpec((GM_TM, A_WIDTH), lambda i: (i, 1)),
            pl.BlockSpec((A_HEADS, CHUNK, CHUNK), lambda i: (0, 0, 0)),
            pl.BlockSpec((CHUNK, A_WIDTH), lambda i: (0, 0)),
        ],
        out_specs=pl.BlockSpec((GM_TM, A_WIDTH), lambda i: (i, 0)),
        out_shape=jax.ShapeDtypeStruct((T, A_WIDTH), BF16),
        compiler_params=_cparams(("parallel",)),
        name="gmlp",
    )(zp, zp, ws_bf, bs_full)


def _attn_kernel(n_tiles, nq, lam_ref, qc_ref, qn_ref, k_ref, v_ref, g_ref, wg_ref, wu_ref, wd_ref,
                 o_ref, wgb_ref, wub_ref, wdb_ref, vt_ref, qt_ref, acc_ref):
    l = pl.program_id(0)
    slot = l % 2
    seq = k_ref.shape[0]
    nc = seq // ATT_TK
    tq = qc_ref.shape[0]

    def prep_q(q_ref, s):
        q_t = q_ref[...].astype(F32).T
        row = lax.broadcasted_iota(I32, q_t.shape, 0)
        qt_ref[s, 0] = jnp.where(row < QK_DIM, q_t, 0.0).astype(BF16)
        qt_ref[s, 1] = jnp.where(row >= QK_DIM, q_t, 0.0).astype(BF16)

    def finish(s):
        a1, a2 = acc_ref[s, 0], acc_ref[s, 1]
        o_t = (a1[0:HEAD] / a1[HEAD:HEAD + 1]
               - lam_ref[0] * (a2[0:HEAD] / a2[HEAD:HEAD + 1]))
        o = o_t.T
        ms = jnp.mean(o * o, axis=-1, keepdims=True)
        o_ref[...] = (o * lax.rsqrt(ms + EPS) * g_ref[...] * (1.0 - LAMBDA_INIT)).astype(BF16)

    def scores(c, qh_t):
        return jnp.dot(k_ref[c * ATT_TK:(c + 1) * ATT_TK, :], qh_t, preferred_element_type=F32)

    def update(c, s_t, m, acc):
        m_new = jnp.maximum(m, jnp.max(s_t, axis=0, keepdims=True))
        p = jnp.exp2(s_t - m_new).astype(BF16)
        pv = jnp.dot(vt_ref[:, c * ATT_TK:(c + 1) * ATT_TK], p, preferred_element_type=F32)
        return m_new, jnp.exp2(m - m_new) * acc + pv

    def attend(s):
        q1_t, q2_t = qt_ref[s, 0], qt_ref[s, 1]
        m1 = m2 = jnp.full((1, tq), -jnp.inf, F32)
        a1 = a2 = jnp.zeros((HEAD + ONES_ROWS, tq), F32)
        s1, s2 = scores(0, q1_t), scores(0, q2_t)
        for c in range(nc):
            if c + 1 < nc:
                n1, n2 = scores(c + 1, q1_t), scores(c + 1, q2_t)
            m1, a1 = update(c, s1, m1, a1)
            m2, a2 = update(c, s2, m2, a2)
            if c + 1 < nc:
                s1, s2 = n1, n2
        acc_ref[s, 0] = a1
        acc_ref[s, 1] = a2

    @pl.when(l == 0)
    def _():
        prep_q(qc_ref, 0)
        acc_ref[1] = jnp.ones(acc_ref.shape[1:], F32)

    @pl.when((l < n_tiles) & (l % nq == 0))
    def _():
        for c in range(nc):
            cols = slice(c * ATT_TK, (c + 1) * ATT_TK)
            vt_ref[0:HEAD, cols] = v_ref[cols, :].astype(F32).T.astype(BF16)
        vt_ref[HEAD:, :] = jnp.ones((ONES_ROWS, seq), BF16)

    @pl.when(l < n_tiles)
    def _():
        finish(1 - slot)
        prep_q(qn_ref, 1 - slot)
        wgb_ref[...] = wg_ref[...].astype(BF16)
        wub_ref[...] = wu_ref[...].astype(BF16)
        wdb_ref[...] = wd_ref[...].astype(BF16)
        attend(slot)

    @pl.when(l == n_tiles)
    def _():
        finish(1 - slot)


def _attention(lam, zp3, v3, subln_g, wg2, wu2, wd2):
    bsz, seq, _ = zp3.shape
    nq = seq // ATT_TQ
    steps = bsz * B_HEADS * nq
    qb, kb, vb = (2 * A_WIDTH) // HEAD, (2 * A_WIDTH + 1024) // HEAD, 0

    def tile(t):
        return t // (B_HEADS * nq), (t // nq) % B_HEADS, t % nq

    def cur(l):
        return jnp.minimum(l, steps - 1)

    def q_map(l, lam, ahead):
        b, h, i = tile(jnp.minimum(l + ahead, steps - 1))
        return b, i, qb + h

    def kv_map(l, lam, base):
        b, h, _ = tile(cur(l))
        return b, 0, base + h

    def o_map(l, lam):
        b, h, i = tile(jnp.maximum(l - 1, 0))
        return b, i, h

    def w_spec(w):
        rows = w.shape[0] // steps
        return pl.BlockSpec((rows, w.shape[1]), lambda l, lam: (cur(l), 0))

    w_specs = [w_spec(wg2), w_spec(wu2), w_spec(wd2)]
    grid_spec = pltpu.PrefetchScalarGridSpec(
        num_scalar_prefetch=1,
        grid=(steps + 1,),
        in_specs=[
            pl.BlockSpec((None, ATT_TQ, HEAD), functools.partial(q_map, ahead=0)),
            pl.BlockSpec((None, ATT_TQ, HEAD), functools.partial(q_map, ahead=1)),
            pl.BlockSpec((None, seq, HEAD), functools.partial(kv_map, base=kb)),
            pl.BlockSpec((None, seq, HEAD), functools.partial(kv_map, base=vb)),
            pl.BlockSpec((1, HEAD), lambda l, lam: (0, 0)),
        ] + w_specs,
        out_specs=[pl.BlockSpec((None, ATT_TQ, HEAD), o_map)] + w_specs,
        scratch_shapes=[
            pltpu.VMEM((HEAD + ONES_ROWS, seq), BF16),
            pltpu.VMEM((2, 2, HEAD, ATT_TQ), BF16),
            pltpu.VMEM((2, 2, HEAD + ONES_ROWS, ATT_TQ), F32),
        ],
    )
    return pl.pallas_call(
        functools.partial(_attn_kernel, steps, nq),
        grid_spec=grid_spec,
        out_shape=[jax.ShapeDtypeStruct((bsz, seq, B_HEADS * HEAD), BF16)]
                  + [jax.ShapeDtypeStruct(w.shape, BF16) for w in (wg2, wu2, wd2)],
        compiler_params=_cparams(("arbitrary",)),
        name="attn",
    )(lam, zp3, zp3, zp3, v3, subln_g, wg2, wu2, wd2)


def _out_proj_kernel(n_tiles, a_ref, b_ref, x_ref, wa_ref, wb_ref, g_ref, wr_ref, br_ref,
                     x1_ref, hm_ref, lg_ref, hbuf):
    i = pl.program_id(0)
    slot = i % 2

    def project(s):
        acc = (jnp.dot(a_ref[...], wa_ref[...], preferred_element_type=F32)
               + jnp.dot(b_ref[...], wb_ref[...], preferred_element_type=F32))
        x1 = x_ref[...] + acc
        x1_ref[...] = x1
        ms = jnp.mean(x1 * x1, axis=-1, keepdims=True)
        hbuf[s] = x1 * lax.rsqrt(ms + EPS) * g_ref[...]

    def emit(s):
        hm = hbuf[s]
        hm_ref[...] = _to_token_rows(hm.astype(BF16))
        lg_ref[...] = lax.dot_general(wr_ref[...], hm.astype(BF16), (((1,), (1,)), ((), ())),
                                      preferred_element_type=F32) + br_ref[...]

    @pl.when(i == 0)
    def _():
        hbuf[1] = jnp.zeros(hbuf.shape[1:], F32)

    @pl.when(i < n_tiles)
    def _():
        emit(1 - slot)
        project(slot)

    @pl.when(i == n_tiles)
    def _():
        emit(1 - slot)


def _out_proj(out_a, out_b, x2, w_out_bf, g, w_rt, b_rt):
    T = x2.shape[0]
    n_tiles = T // OUT_TM

    def cur(i):
        return jnp.minimum(i, n_tiles - 1)

    def prev(i):
        return jnp.maximum(i - 1, 0)

    return pl.pallas_call(
        functools.partial(_out_proj_kernel, n_tiles),
        grid=(n_tiles + 1,),
        in_specs=[
            pl.BlockSpec((OUT_TM, A_WIDTH), lambda i: (cur(i), 0)),
            pl.BlockSpec((OUT_TM, A_WIDTH), lambda i: (cur(i), 0)),
            pl.BlockSpec((OUT_TM, D_MODEL), lambda i: (cur(i), 0)),
            pl.BlockSpec((A_WIDTH, D_MODEL), lambda i: (0, 0)),
            pl.BlockSpec((A_WIDTH, D_MODEL), lambda i: (1, 0)),
            pl.BlockSpec((1, D_MODEL), lambda i: (0, 0)),
            pl.BlockSpec((LG_ROWS, D_MODEL), lambda i: (0, 0)),
            pl.BlockSpec((LG_ROWS, 1), lambda i: (0, 0)),
        ],
        out_specs=[
            pl.BlockSpec((OUT_TM, D_MODEL), lambda i: (cur(i), 0)),
            pl.BlockSpec((OUT_TM * ROW_SUB, LANES), lambda i: (prev(i), 0)),
            pl.BlockSpec((LG_ROWS, OUT_TM), lambda i: (0, prev(i))),
        ],
        out_shape=[
            jax.ShapeDtypeStruct((T, D_MODEL), F32),
            jax.ShapeDtypeStruct((T * ROW_SUB, LANES), ROW_DT),
            jax.ShapeDtypeStruct((LG_ROWS, T), F32),
        ],
        scratch_shapes=[pltpu.VMEM((2, OUT_TM, D_MODEL), F32)],
        compiler_params=_cparams(("arbitrary",)),
        name="out_proj",
    )(out_a, out_b, x2, w_out_bf, w_out_bf, g, w_rt, b_rt)


def _route_kernel(lg_ref, tri_ref, idr_ref, cw_ref, cnt_ref, carry_ref):
    i = pl.program_id(0)

    @pl.when(i == 0)
    def _():
        carry_ref[...] = jnp.zeros_like(carry_ref)

    tn = lg_ref.shape[1]
    row = lax.broadcasted_iota(I32, (SUBLANES, tn), 0)
    neg = jnp.float32(-jnp.inf)

    def top1(v):
        mx = jnp.max(v, axis=0, keepdims=True)
        idx = jnp.min(jnp.where(v == mx, row, SUBLANES), axis=0, keepdims=True)
        return mx, idx

    g = jnp.where(row < N_GROUPS, lg_ref[0:SUBLANES, :], neg)
    gmax, gidx = top1(g)
    g_w = 1.0 / jnp.sum(jnp.exp(g - gmax), axis=0, keepdims=True)
    sel = lg_ref[SUBLANES * N_GROUPS:SUBLANES * (N_GROUPS + 1), :]
    for grp in range(N_GROUPS - 2, -1, -1):
        sel = jnp.where(gidx == grp, lg_ref[SUBLANES * (grp + 1):SUBLANES * (grp + 2), :], sel)
    e1, i1 = top1(sel)
    e2, i2 = top1(jnp.where(row == i1, neg, sel))
    d = jnp.exp(e2 - e1)
    w1 = 1.0 / (1.0 + d)
    w2 = d * w1
    id0 = gidx * EPG + i1
    id1 = gidx * EPG + i2

    erow = lax.broadcasted_iota(I32, (N_EXPERTS, tn), 0)
    o0 = erow == id0
    o1 = erow == id1
    occ = jnp.where(o0 | o1, 1.0, 0.0).astype(F32)
    before = jnp.dot(occ.astype(BF16), tri_ref[...], preferred_element_type=F32) + carry_ref[:, 0:1]
    r0 = jnp.sum(jnp.where(o0, before, 0.0), axis=0, keepdims=True)
    r1 = jnp.sum(jnp.where(o1, before, 0.0), axis=0, keepdims=True)
    total = carry_ref[...] + jnp.sum(occ, axis=1, keepdims=True)
    carry_ref[...] = total
    cnt_ref[...] = total.astype(I32)

    zi = jnp.zeros((SUBLANES - 4, tn), I32)
    idr_ref[...] = jnp.concatenate([id0, id1, r0.astype(I32), r1.astype(I32), zi], axis=0)
    zf = jnp.zeros((SUBLANES - 2, tn), F32)
    cw_ref[...] = jnp.concatenate([g_w * w1, g_w * w2, zf], axis=0)


def _route(lg_t, tri):
    T = lg_t.shape[1]
    return pl.pallas_call(
        _route_kernel,
        grid=(T // RT_TN,),
        in_specs=[
            pl.BlockSpec((LG_ROWS, RT_TN), lambda i: (0, i)),
            pl.BlockSpec((RT_TN, RT_TN), lambda i: (0, 0)),
        ],
        out_specs=[
            pl.BlockSpec((SUBLANES, RT_TN), lambda i: (0, i)),
            pl.BlockSpec((SUBLANES, RT_TN), lambda i: (0, i)),
            pl.BlockSpec((N_EXPERTS, LANES), lambda i: (0, 0)),
        ],
        out_shape=[
            jax.ShapeDtypeStruct((SUBLANES, T), I32),
            jax.ShapeDtypeStruct((SUBLANES, T), F32),
            jax.ShapeDtypeStruct((N_EXPERTS, LANES), I32),
        ],
        scratch_shapes=[pltpu.VMEM((N_EXPERTS, LANES), F32)],
        compiler_params=_cparams(("arbitrary",)),
        name="route",
    )(lg_t, tri)


def _row_copy(src_ref, src_row, dst_ref, dst_row, sem):
    return pltpu.make_async_copy(src_ref.at[pl.ds(src_row * ROW_SUB, ROW_SUB)],
                                 dst_ref.at[pl.ds(dst_row * ROW_SUB, ROW_SUB)], sem)


def _dispatch_kernel(zt_ref, pos_ref, hm_ref, xs_ref, zbuf, sem):
    i = pl.program_id(0)

    def zero_copy(e):
        return pltpu.make_async_copy(
            zbuf, xs_ref.at[pl.ds(pl.multiple_of(zt_ref[e] * ROW_SUB, ROW_SUB), EXP_BLK * ROW_SUB)], sem)

    @pl.when(i == 0)
    def _():
        zbuf[...] = jnp.zeros_like(zbuf)
        for e in range(2 * N_EXPERTS):
            @pl.when(zt_ref[e] >= 0)
            def _():
                zero_copy(e).start()
        for e in range(2 * N_EXPERTS):
            @pl.when(zt_ref[e] >= 0)
            def _():
                zero_copy(e).wait()

    def issue(grp, carry):
        for u in range(DMA_UNROLL):
            t = grp * DMA_UNROLL + u
            for k in range(TOP_K):
                _row_copy(hm_ref, t, xs_ref, pos_ref[k, t], sem).start()
        return carry

    lax.fori_loop(0, DSP_TC // DMA_UNROLL, issue, 0)

    def drain(grp, carry):
        for _ in range(DMA_UNROLL * TOP_K):
            _row_copy(hm_ref, 0, xs_ref, 0, sem).wait()
        return carry

    lax.fori_loop(0, DSP_TC // DMA_UNROLL, drain, 0)


def _dispatch(zt, pos, hm3, n_rows):
    T = pos.shape[1]
    grid_spec = pltpu.PrefetchScalarGridSpec(
        num_scalar_prefetch=1,
        grid=(T // DSP_TC,),
        in_specs=[
            pl.BlockSpec((TOP_K, DSP_TC), lambda i, zt: (0, i), memory_space=pltpu.SMEM),
            pl.BlockSpec((DSP_TC * ROW_SUB, LANES), lambda i, zt: (i, 0)),
        ],
        out_specs=pl.BlockSpec(memory_space=pl.ANY),
        scratch_shapes=[pltpu.VMEM((EXP_BLK * ROW_SUB, LANES), ROW_DT), pltpu.SemaphoreType.DMA(())],
    )
    return pl.pallas_call(
        _dispatch_kernel,
        grid_spec=grid_spec,
        out_shape=jax.ShapeDtypeStruct((n_rows * ROW_SUB, LANES), ROW_DT),
        compiler_params=_cparams(("arbitrary",)),
        name="dispatch",
    )(zt, pos, hm3)


def _expert_kernel(be_ref, first_ref, slot_ref, nxt_ref, nu_ref, xs_ref, wg_hbm, wu_hbm, wd_hbm,
                   y_ref, xb_ref, wg_buf, wu_buf, wd_buf, sems):
    b = pl.program_id(0)

    def weight_copies(e, s):
        return [pltpu.make_async_copy(hbm.at[e], buf.at[s], sems.at[s])
                for hbm, buf in ((wg_hbm, wg_buf), (wu_hbm, wu_buf), (wd_hbm, wd_buf))]

    @pl.when(b < nu_ref[0])
    def _():
        s = slot_ref[b]

        @pl.when(b == 0)
        def _():
            for cp in weight_copies(be_ref[0], 0):
                cp.start()

        @pl.when(first_ref[b] == 1)
        def _():
            for cp in weight_copies(be_ref[b], s):
                cp.wait()

            @pl.when(nxt_ref[b] >= 0)
            def _():
                for cp in weight_copies(nxt_ref[b], 1 - s):
                    cp.start()

        slabs = _from_token_rows(xs_ref[...])
        for k in range(ROW_SUB):
            xb_ref[:, k * LANES:(k + 1) * LANES] = slabs[k]
        x = xb_ref[...]
        g = jnp.dot(x, wg_buf[s], preferred_element_type=F32)
        u = jnp.dot(x, wu_buf[s], preferred_element_type=F32)
        a = (jax.nn.silu(g) * u).astype(BF16)
        y = jnp.dot(a, wd_buf[s], preferred_element_type=F32)
        y_ref[...] = _to_token_rows(y.astype(ROW_DT))

    @pl.when(b >= nu_ref[0])
    def _():
        y_ref[...] = jnp.zeros_like(y_ref)


def _experts(block_e, first, slot, nxt, n_used, xs3, wg, wu, wd):
    n_blocks = xs3.shape[0] // (EXP_BLK * ROW_SUB)

    def xs_map(b, be, first, slot, nxt, nu):
        return (jnp.minimum(b, nu[0] - 1), 0)

    grid_spec = pltpu.PrefetchScalarGridSpec(
        num_scalar_prefetch=5,
        grid=(n_blocks,),
        in_specs=[
            pl.BlockSpec((EXP_BLK * ROW_SUB, LANES), xs_map),
            pl.BlockSpec(memory_space=pl.ANY),
            pl.BlockSpec(memory_space=pl.ANY),
            pl.BlockSpec(memory_space=pl.ANY),
        ],
        out_specs=pl.BlockSpec((EXP_BLK * ROW_SUB, LANES), lambda b, *_: (b, 0)),
        scratch_shapes=[
            pltpu.VMEM((EXP_BLK, D_MODEL), BF16),
            pltpu.VMEM((2, D_MODEL, D_EXPERT), BF16),
            pltpu.VMEM((2, D_MODEL, D_EXPERT), BF16),
            pltpu.VMEM((2, D_EXPERT, D_MODEL), BF16),
            pltpu.SemaphoreType.DMA((2,)),
        ],
    )
    return pl.pallas_call(
        _expert_kernel,
        grid_spec=grid_spec,
        out_shape=jax.ShapeDtypeStruct(xs3.shape, ROW_DT),
        compiler_params=_cparams(("arbitrary",)),
        name="experts",
    )(block_e, first, slot, nxt, n_used, xs3, wg, wu, wd)


def _combine_kernel(pos_ref, posn_ref, x1_ref, cw_ref, y_ref, o_ref, yb, sems):
    i = pl.program_id(0)
    slot = i % 2

    def gather(ids_ref, slot_):
        def issue(grp, carry):
            for u in range(DMA_UNROLL):
                t = grp * DMA_UNROLL + u
                for k in range(TOP_K):
                    _row_copy(y_ref, ids_ref[k, t], yb.at[slot_, k], t, sems.at[slot_]).start()
            return carry

        lax.fori_loop(0, CMB_TC // DMA_UNROLL, issue, 0)

    @pl.when(i == 0)
    def _():
        gather(pos_ref, 0)

    @pl.when(i + 1 < pl.num_programs(0))
    def _():
        gather(posn_ref, 1 - slot)

    def drain(grp, carry):
        for _ in range(DMA_UNROLL * TOP_K):
            _row_copy(y_ref, 0, yb.at[slot, 0], 0, sems.at[slot]).wait()
        return carry

    lax.fori_loop(0, CMB_TC // DMA_UNROLL, drain, 0)
    w_t = cw_ref[...].T
    w0 = w_t[:, 0:1]
    w1 = w_t[:, 1:2]
    y0 = _from_token_rows(yb[slot, 0])
    y1 = _from_token_rows(yb[slot, 1])
    for s in range(ROW_SUB):
        sl = slice(s * LANES, (s + 1) * LANES)
        o_ref[:, sl] = x1_ref[:, sl] + w0 * y0[s].astype(F32) + w1 * y1[s].astype(F32)


def _combine(pos, x1, cw, y3):
    T = x1.shape[0]
    steps = T // CMB_TC
    return pl.pallas_call(
        _combine_kernel,
        grid=(steps,),
        in_specs=[
            pl.BlockSpec((TOP_K, CMB_TC), lambda i: (0, i), memory_space=pltpu.SMEM),
            pl.BlockSpec((TOP_K, CMB_TC), lambda i: (0, jnp.minimum(i + 1, steps - 1)),
                         memory_space=pltpu.SMEM),
            pl.BlockSpec((CMB_TC, D_MODEL), lambda i: (i, 0)),
            pl.BlockSpec((SUBLANES, CMB_TC), lambda i: (0, i)),
            pl.BlockSpec(memory_space=pl.ANY),
        ],
        out_specs=pl.BlockSpec((CMB_TC, D_MODEL), lambda i: (i, 0)),
        out_shape=jax.ShapeDtypeStruct((T, D_MODEL), F32),
        scratch_shapes=[
            pltpu.VMEM((2, TOP_K, CMB_TC * ROW_SUB, LANES), ROW_DT),
            pltpu.SemaphoreType.DMA((2,)),
        ],
        compiler_params=_cparams(("arbitrary",)),
        name="combine",
    )(pos, pos, x1, cw, y3)


def _rotary_lane_tables(seq):
    pos = jnp.arange(seq, dtype=F32)
    inv_freq = 1.0 / (jnp.float32(ROPE_THETA) ** (jnp.arange(0, ROT_DIM, 2, dtype=F32) / ROT_DIM))
    ang = pos[:, None] * inv_freq[None, :]
    cos, sin = jnp.cos(ang), jnp.sin(ang)
    half = ROT_DIM // 2
    lane = jnp.arange(LANES) % QK_DIM
    fidx = lane % half
    first = (lane < half)[None, :]
    second = ((lane >= half) & (lane < ROT_DIM))[None, :]
    cos_t = jnp.where(first | second, cos[:, fidx], 1.0)
    s_up = jnp.where(first, -sin[:, fidx], 0.0)
    s_dn = jnp.where(second, sin[:, fidx], 0.0)
    return cos_t, s_up, s_dn


def kernel(x, attn_norm_g, w_in, gmlp_ln_g, gmlp_ln_b, gmlp_ws, gmlp_bs, q_norm_g, k_norm_g,
           lambda_q1, lambda_k1, lambda_q2, lambda_k2, subln_g, w_out, ffn_norm_g,
           w_group, b_group, w_router, b_router, w_gate, w_up, w_down):
    bsz, seq, d = x.shape
    T = bsz * seq
    assert d == D_MODEL and seq % IN_TM == 0 and seq % ATT_TQ == 0 and T % RT_TN == 0
    l = 0
    x2 = x.reshape(T, d)

    params = jnp.zeros((8, IN_WIDTH), F32)
    params = params.at[0, A_WIDTH:2 * A_WIDTH].set(gmlp_ln_g[l].reshape(-1))
    params = params.at[1, A_WIDTH:2 * A_WIDTH].set(gmlp_ln_b[l].reshape(-1))
    params = params.at[0, 2 * A_WIDTH:3 * A_WIDTH].set(jnp.tile(q_norm_g[l], 2 * B_HEADS))
    params = params.at[0, 3 * A_WIDTH:4 * A_WIDTH].set(jnp.tile(k_norm_g[l], 2 * B_HEADS))
    blk = jnp.arange(IN_GRP) // QK_DIM
    bd = jnp.where(blk[:, None] == blk[None, :], 1.0 / QK_DIM, 0.0).astype(BF16)
    cos_t, s_up, s_dn = _rotary_lane_tables(seq)
    bs_full = jnp.repeat(jnp.transpose(gmlp_bs[l]), HEAD, axis=1)
    lam = (jnp.exp(jnp.sum(lambda_q1[l] * lambda_k1[l])) - jnp.exp(jnp.sum(lambda_q2[l] * lambda_k2[l]))
           + LAMBDA_INIT).reshape(1).astype(F32)
    w_rt = jnp.zeros((LG_ROWS, D_MODEL), F32)
    w_rt = w_rt.at[:N_GROUPS].set(w_group[l].T).at[SUBLANES:].set(w_router[l].T)
    b_rt = jnp.zeros((LG_ROWS, 1), F32)
    b_rt = b_rt.at[:N_GROUPS, 0].set(b_group[l]).at[SUBLANES:, 0].set(b_router[l])
    tri = (jnp.arange(RT_TN)[:, None] < jnp.arange(RT_TN)[None, :]).astype(BF16)

    zp, vb = _in_proj(x2, attn_norm_g[l].reshape(1, d), w_in[l].astype(BF16), params, bd,
                      cos_t, s_up, s_dn, seq)
    out_a = _gmlp(zp, gmlp_ws[l].astype(BF16), bs_full)
    att_steps = bsz * B_HEADS * (seq // ATT_TQ)
    assert (N_EXPERTS * D_MODEL) % att_steps == 0 and (N_EXPERTS * D_EXPERT) % att_steps == 0
    out_b, wg_bf, wu_bf, wd_bf = _attention(
        lam, zp.reshape(bsz, seq, IN_WIDTH - IN_TN), vb.reshape(bsz, seq, IN_TN), subln_g[l].reshape(1, HEAD),
        w_gate[l].reshape(N_EXPERTS * D_MODEL, D_EXPERT), w_up[l].reshape(N_EXPERTS * D_MODEL, D_EXPERT),
        w_down[l].reshape(N_EXPERTS * D_EXPERT, D_MODEL))
    x1, hm3, lg_t = _out_proj(out_a, out_b.reshape(T, A_WIDTH), x2, w_out[l].astype(BF16),
                              ffn_norm_g[l].reshape(1, d), w_rt.astype(BF16), b_rt)

    idr, cw, cnt = _route(lg_t, tri)
    counts = cnt[:, 0]
    padded = (counts + EXP_BLK - 1) // EXP_BLK * EXP_BLK
    pend = jnp.cumsum(padded)
    pstart = (pend - padded).astype(I32)
    n_blocks = (T * TOP_K) // EXP_BLK + N_EXPERTS
    n_rows = n_blocks * EXP_BLK
    block_e = jnp.minimum(jnp.sum(pend[None, :] <= (jnp.arange(n_blocks) * EXP_BLK)[:, None], axis=1),
                          N_EXPERTS - 1).astype(I32)
    n_used = (pend[-1] // EXP_BLK).reshape(1).astype(I32)
    eids = jnp.arange(N_EXPERTS, dtype=I32)
    used = jnp.arange(n_blocks) < n_used[0]
    first = ((block_e != jnp.concatenate([jnp.full((1,), -1, I32), block_e[:-1]])) & used).astype(I32)
    slot = ((jnp.cumsum(first) - 1) % 2).astype(I32)
    later = (eids[None, :] > eids[:, None]) & (counts > 0)[None, :]
    nxt_of = jnp.min(jnp.where(later, eids[None, :], N_EXPERTS), axis=1)
    nxt_of = jnp.where(nxt_of == N_EXPERTS, -1, nxt_of)
    nxt = jnp.sum(jnp.where(block_e[:, None] == eids[None, :], nxt_of[None, :], 0), axis=1).astype(I32)
    trail =pend[-1] + jnp.arange(N_EXPERTS) * EXP_BLK
    zt = jnp.concatenate([jnp.where(counts > 0, pend - EXP_BLK, -1),
                          jnp.where(trail < n_rows, trail, -1)]).astype(I32)

    is_e = idr[0:TOP_K, None, :] == jnp.arange(N_EXPERTS, dtype=I32)[None, :, None]
    pos = jnp.sum(jnp.where(is_e, pstart[None, :, None], 0), axis=1) + idr[TOP_K:2 * TOP_K]
    xs3 = _dispatch(zt, pos, hm3, n_rows)
    y3 = _experts(block_e, first, slot, nxt, n_used, xs3, wg_bf.reshape(N_EXPERTS, D_MODEL, D_EXPERT),
                  wu_bf.reshape(N_EXPERTS, D_MODEL, D_EXPERT), wd_bf.reshape(N_EXPERTS, D_EXPERT, D_MODEL))
    out = _combine(pos, x1, cw, y3)
    return out.reshape(bsz, seq, d)
```

```python
import functools
import math

import jax
import jax.numpy as jnp
from jax import lax
from jax.experimental import pallas as pl
from jax.experimental.pallas import tpu as pltpu

F32 = jnp.float32
BF16 = jnp.bfloat16
I32 = jnp.int32

D_MODEL = 2048
A_WIDTH = 1024
A_HEADS = 8
HEAD = 128
CHUNK = 128
B_HEADS = 8
QK_DIM = 64
ROT_DIM = 16
ROPE_THETA = 500000.0
IN_WIDTH = 5120
N_GROUPS = 4
EPG = 8
N_EXPERTS = 32
TOP_K = 2
D_EXPERT = 1024
EPS = 1e-6
LAMBDA_INIT = 0.8 - 0.6 * math.exp(-0.3 * 0)

LANES = 128
SUBLANES = 8
ROW_SUB = D_MODEL // LANES
ROW_DT = jnp.bfloat16
LG_ROWS = SUBLANES * (1 + N_GROUPS)

IN_TM = 512
IN_TN = 1024
IN_GRP = 256
IN_ROWS = 64
GM_TM = 512
ATT_TQ = 1024
ATT_TK = 512
ONES_ROWS = 16
OUT_TM = 256
RT_TN = 512
EXP_BLK = 256
DSP_TC = 512
CMB_TC = 256
DMA_UNROLL = 8
VMEM_LIMIT = 48 * 1024 * 1024


def _cparams(sem):
    return pltpu.CompilerParams(dimension_semantics=sem, vmem_limit_bytes=VMEM_LIMIT)


def _to_token_rows(x):
    slabs = jnp.stack([x[:, s * LANES:(s + 1) * LANES] for s in range(ROW_SUB)], axis=0)
    return jnp.transpose(slabs, (1, 0, 2))


def _from_token_rows(x3):
    return jnp.transpose(x3, (1, 0, 2))


def _in_proj_kernel(x_ref, g_ref, w_ref, p_ref, bd_ref, c_ref, s1_ref, s2_ref, o_ref, v_ref, hn_ref, z_ref):
    j = pl.program_id(1)
    tm = hn_ref.shape[0]
    ngrp = IN_TN // IN_GRP

    @pl.when(j == 0)
    def _():
        x = x_ref[...]
        ms = jnp.mean(x * x, axis=-1, keepdims=True)
        hn_ref[...] = (x * lax.rsqrt(ms + EPS) * g_ref[...]).astype(BF16)

    def project(slot):
        z_ref[slot] = jnp.dot(hn_ref[...], w_ref[...], preferred_element_type=F32)

    def gelu_section(slot):
        for r in range(tm // IN_ROWS):
            rows = slice(r * IN_ROWS, (r + 1) * IN_ROWS)
            o_ref[rows, :] = jax.nn.gelu(z_ref[slot, rows, :]).astype(BF16)

    def gelu_ln_section(slot):
        for s in range(IN_TN // LANES):
            sl = slice(s * LANES, (s + 1) * LANES)
            gz = jax.nn.gelu(z_ref[slot, :, sl])
            mu = jnp.mean(gz, axis=-1, keepdims=True)
            xc = gz - mu
            var = jnp.mean(xc * xc, axis=-1, keepdims=True)
            o_ref[:, sl] = (xc * lax.rsqrt(var + EPS) * p_ref[0:1, sl] + p_ref[1:2, sl]).astype(BF16)

    def qk_section(slot, scale):
        sq = jnp.concatenate([z_ref[slot, :, g * IN_GRP:(g + 1) * IN_GRP] for g in range(ngrp)], axis=0)
        ms = jnp.dot((sq * sq).astype(BF16), bd_ref[...], preferred_element_type=F32)
        for s in range(IN_TN // LANES):
            sl = slice(s * LANES, (s + 1) * LANES)
            g, c = divmod(s * LANES, IN_GRP)
            ms_s = ms[g * tm:(g + 1) * tm, c:c + LANES]
            y = z_ref[slot, :, sl] * lax.rsqrt(ms_s + EPS) * p_ref[0:1, sl]
            r = (y * c_ref[...] + pltpu.roll(y, LANES - ROT_DIM // 2, 1) * s1_ref[...]
                 + pltpu.roll(y, ROT_DIM // 2, 1) * s2_ref[...])
            o_ref[:, sl] = (r * scale if scale != 1.0 else r).astype(BF16)

    @pl.when(j == 0)
    def _():
        project(0)

    @pl.when(j == 1)
    def _():
        gelu_section(0)
        project(1)

    @pl.when(j == 2)
    def _():
        gelu_ln_section(1)
        project(0)

    @pl.when(j == 3)
    def _():
        qk_section(0, math.log2(math.e) / math.sqrt(QK_DIM))
        project(1)

    @pl.when(j == 4)
    def _():
        qk_section(1, 1.0)
        v_ref[...] = jnp.dot(hn_ref[...], w_ref[...], preferred_element_type=F32).astype(BF16)


def _in_proj(x2, g, w_bf, params, bd, cos_t, sup_t, sdn_t, seq):
    T = x2.shape[0]
    spt = seq // IN_TM
    nsec = IN_WIDTH // IN_TN
    assert nsec == 5
    return pl.pallas_call(
        _in_proj_kernel,
        grid=(T // IN_TM, nsec),
        in_specs=[
            pl.BlockSpec((IN_TM, D_MODEL), lambda i, j: (i, 0)),
            pl.BlockSpec((1, D_MODEL), lambda i, j: (0, 0)),
            pl.BlockSpec((D_MODEL, IN_TN), lambda i, j: (0, j)),
            pl.BlockSpec((8, IN_TN), lambda i, j: (0, jnp.maximum(j - 1, 0))),
            pl.BlockSpec((IN_GRP, IN_GRP), lambda i, j: (0, 0)),
            pl.BlockSpec((IN_TM, LANES), lambda i, j: (i % spt, 0)),
            pl.BlockSpec((IN_TM, LANES), lambda i, j: (i % spt, 0)),
            pl.BlockSpec((IN_TM, LANES), lambda i, j: (i % spt, 0)),
        ],
        out_specs=[
            pl.BlockSpec((IN_TM, IN_TN), lambda i, j: (i, jnp.maximum(j - 1, 0))),
            pl.BlockSpec((IN_TM, IN_TN), lambda i, j: (i, 0)),
        ],
        out_shape=[
            jax.ShapeDtypeStruct((T, IN_WIDTH - IN_TN), BF16),
            jax.ShapeDtypeStruct((T, IN_TN), BF16),
        ],
        scratch_shapes=[pltpu.VMEM((IN_TM, D_MODEL), BF16), pltpu.VMEM((2, IN_TM, IN_TN), F32)],
        compiler_params=_cparams(("parallel", "arbitrary")),
        name="in_proj",
    )(x2, g, w_bf, params, bd, cos_t, sup_t, sdn_t)


def _gmlp_kernel(u_ref, v_ref, ws_ref, bs_ref, o_ref):
    for c in range(GM_TM // CHUNK):
        rows = slice(c * CHUNK, (c + 1) * CHUNK)
        for h in range(A_HEADS):
            cols = slice(h * HEAD, (h + 1) * HEAD)
            s = jnp.dot(ws_ref[h], v_ref[rows, cols], preferred_element_type=F32) + bs_ref[:, cols]
            o_ref[rows, cols] = (u_ref[rows, cols].astype(F32) * s).astype(BF16)


def _gmlp(zp, ws_bf, bs_full):
    T = zp.shape[0]
    return pl.pallas_call(
        _gmlp_kernel,
        grid=(T // GM_TM,),
        in_specs=[
            pl.BlockSpec((GM_TM, A_WIDTH), lambda i: (i, 0)),
            pl.BlockSpec((GM_TM, A_WIDTH), lambda i: (i, 1)),
            pl.BlockSpec((A_HEADS, CHUNK, CHUNK), lambda i: (0, 0, 0)),
            pl.BlockSpec((CHUNK, A_WIDTH), lambda i: (0, 0)),
        ],
        out_specs=pl.BlockSpec((GM_TM, A_WIDTH), lambda i: (i, 0)),
        out_shape=jax.ShapeDtypeStruct((T, A_WIDTH), BF16),
        compiler_params=_cparams(("parallel",)),
        name="gmlp",
    )(zp, zp, ws_bf, bs_full)


def _attn_kernel(n_tiles, nq, lam_ref, qc_ref, qn_ref, k_ref, v_ref, g_ref, wg_ref, wu_ref, wd_ref,
                 o_ref, wgb_ref, wub_ref, wdb_ref, vt_ref, qt_ref, acc_ref):
    l = pl.program_id(0)
    slot = l % 2
    seq = k_ref.shape[0]
    nc = seq // ATT_TK
    tq = qc_ref.shape[0]

    def prep_q(q_ref, s):
        q_t = q_ref[...].astype(F32).T
        row = lax.broadcasted_iota(I32, q_t.shape, 0)
        qt_ref[s, 0] = jnp.where(row < QK_DIM, q_t, 0.0).astype(BF16)
        qt_ref[s, 1] = jnp.where(row >= QK_DIM, q_t, 0.0).astype(BF16)

    def finish(s):
        a1, a2 = acc_ref[s, 0], acc_ref[s, 1]
        o_t = (a1[0:HEAD] / a1[HEAD:HEAD + 1]
               - lam_ref[0] * (a2[0:HEAD] / a2[HEAD:HEAD + 1]))
        o = o_t.T
        ms = jnp.mean(o * o, axis=-1, keepdims=True)
        o_ref[...] = (o * lax.rsqrt(ms + EPS) * g_ref[...] * (1.0 - LAMBDA_INIT)).astype(BF16)

    def scores(c, qh_t):
        return jnp.dot(k_ref[c * ATT_TK:(c + 1) * ATT_TK, :], qh_t, preferred_element_type=F32)

    def update(c, s_t, m, acc):
        m_new = jnp.maximum(m, jnp.max(s_t, axis=0, keepdims=True))
        p = jnp.exp2(s_t - m_new).astype(BF16)
        pv = jnp.dot(vt_ref[:, c * ATT_TK:(c + 1) * ATT_TK], p, preferred_element_type=F32)
        return m_new, jnp.exp2(m - m_new) * acc + pv

    def attend(s):
        q1_t, q2_t = qt_ref[s, 0], qt_ref[s, 1]
        m1 = m2 = jnp.full((1, tq), -jnp.inf, F32)
        a1 = a2 = jnp.zeros((HEAD + ONES_ROWS, tq), F32)
        s1, s2 = scores(0, q1_t), scores(0, q2_t)
        for c in range(nc):
            if c + 1 < nc:
                n1, n2 = scores(c + 1, q1_t), scores(c + 1, q2_t)
            m1, a1 = update(c, s1, m1, a1)
            m2, a2 = update(c, s2, m2, a2)
            if c + 1 < nc:
                s1, s2 = n1, n2
        acc_ref[s, 0] = a1
        acc_ref[s, 1] = a2

    @pl.when(l == 0)
    def _():
        prep_q(qc_ref, 0)
        acc_ref[1] = jnp.ones(acc_ref.shape[1:], F32)

    @pl.when((l < n_tiles) & (l % nq == 0))
    def _():
        for c in range(nc):
            cols = slice(c * ATT_TK, (c + 1) * ATT_TK)
            vt_ref[0:HEAD, cols] = v_ref[cols, :].astype(F32).T.astype(BF16)
        vt_ref[HEAD:, :] = jnp.ones((ONES_ROWS, seq), BF16)

    @pl.when(l < n_tiles)
    def _():
        finish(1 - slot)
        prep_q(qn_ref, 1 - slot)
        wgb_ref[...] = wg_ref[...].astype(BF16)
        wub_ref[...] = wu_ref[...].astype(BF16)
        wdb_ref[...] = wd_ref[...].astype(BF16)
        attend(slot)

    @pl.when(l == n_tiles)
    def _():
        finish(1 - slot)


def _attention(lam, zp3, v3, subln_g, wg2, wu2, wd2):
    bsz, seq, _ = zp3.shape
    nq = seq // ATT_TQ
    steps = bsz * B_HEADS * nq
    qb, kb, vb = (2 * A_WIDTH) // HEAD, (2 * A_WIDTH + 1024) // HEAD, 0

    def tile(t):
        return t // (B_HEADS * nq), (t // nq) % B_HEADS, t % nq

    def cur(l):
        return jnp.minimum(l, steps - 1)

    def q_map(l, lam, ahead):
        b, h, i = tile(jnp.minimum(l + ahead, steps - 1))
        return b, i, qb + h

    def kv_map(l, lam, base):
        b, h, _ = tile(cur(l))
        return b, 0, base + h

    def o_map(l, lam):
        b, h, i = tile(jnp.maximum(l - 1, 0))
        return b, i, h

    def w_spec(w):
        rows = w.shape[0] // steps
        return pl.BlockSpec((rows, w.shape[1]), lambda l, lam: (cur(l), 0))

    w_specs = [w_spec(wg2), w_spec(wu2), w_spec(wd2)]
    grid_spec = pltpu.PrefetchScalarGridSpec(
        num_scalar_prefetch=1,
        grid=(steps + 1,),
        in_specs=[
            pl.BlockSpec((None, ATT_TQ, HEAD), functools.partial(q_map, ahead=0)),
            pl.BlockSpec((None, ATT_TQ, HEAD), functools.partial(q_map, ahead=1)),
            pl.BlockSpec((None, seq, HEAD), functools.partial(kv_map, base=kb)),
            pl.BlockSpec((None, seq, HEAD), functools.partial(kv_map, base=vb)),
            pl.BlockSpec((1, HEAD), lambda l, lam: (0, 0)),
        ] + w_specs,
        out_specs=[pl.BlockSpec((None, ATT_TQ, HEAD), o_map)] + w_specs,
        scratch_shapes=[
            pltpu.VMEM((HEAD + ONES_ROWS, seq), BF16),
            pltpu.VMEM((2, 2, HEAD, ATT_TQ), BF16),
            pltpu.VMEM((2, 2, HEAD + ONES_ROWS, ATT_TQ), F32),
        ],
    )
    return pl.pallas_call(
        functools.partial(_attn_kernel, steps, nq),
        grid_spec=grid_spec,
        out_shape=[jax.ShapeDtypeStruct((bsz, seq, B_HEADS * HEAD), BF16)]
                  + [jax.ShapeDtypeStruct(w.shape, BF16) for w in (wg2, wu2, wd2)],
        compiler_params=_cparams(("arbitrary",)),
        name="attn",
    )(lam, zp3, zp3, zp3, v3, subln_g, wg2, wu2, wd2)


def _out_proj_kernel(n_tiles, a_ref, b_ref, x_ref, wa_ref, wb_ref, g_ref, wr_ref, br_ref,
                     x1_ref, hm_ref, lg_ref, hbuf):
    i = pl.program_id(0)
    slot = i % 2

    def project(s):
        acc = (jnp.dot(a_ref[...], wa_ref[...], preferred_element_type=F32)
               + jnp.dot(b_ref[...], wb_ref[...], preferred_element_type=F32))
        x1 = x_ref[...] + acc
        x1_ref[...] = x1
        ms = jnp.mean(x1 * x1, axis=-1, keepdims=True)
        hbuf[s] = x1 * lax.rsqrt(ms + EPS) * g_ref[...]

    def emit(s):
        hm = hbuf[s]
        hm_ref[...] = _to_token_rows(hm.astype(BF16))
        lg_ref[...] = lax.dot_general(wr_ref[...], hm.astype(BF16), (((1,), (1,)), ((), ())),
                                      preferred_element_type=F32) + br_ref[...]

    @pl.when(i == 0)
    def _():
        hbuf[1] = jnp.zeros(hbuf.shape[1:], F32)

    @pl.when(i < n_tiles)
    def _():
        emit(1 - slot)
        project(slot)

    @pl.when(i == n_tiles)
    def _():
        emit(1 - slot)


def _out_proj(out_a, out_b, x2, w_out_bf, g, w_rt, b_rt):
    T = x2.shape[0]
    n_tiles = T // OUT_TM

    def cur(i):
        return jnp.minimum(i, n_tiles - 1)

    def prev(i):
        return jnp.maximum(i - 1, 0)

    return pl.pallas_call(
        functools.partial(_out_proj_kernel, n_tiles),
        grid=(n_tiles + 1,),
        in_specs=[
            pl.BlockSpec((OUT_TM, A_WIDTH), lambda i: (cur(i), 0)),
            pl.BlockSpec((OUT_TM, A_WIDTH), lambda i: (cur(i), 0)),
            pl.BlockSpec((OUT_TM, D_MODEL), lambda i: (cur(i), 0)),
            pl.BlockSpec((A_WIDTH, D_MODEL), lambda i: (0, 0)),
            pl.BlockSpec((A_WIDTH, D_MODEL), lambda i: (1, 0)),
            pl.BlockSpec((1, D_MODEL), lambda i: (0, 0)),
            pl.BlockSpec((LG_ROWS, D_MODEL), lambda i: (0, 0)),
            pl.BlockSpec((LG_ROWS, 1), lambda i: (0, 0)),
        ],
        out_specs=[
            pl.BlockSpec((OUT_TM, D_MODEL), lambda i: (cur(i), 0)),
            pl.BlockSpec((OUT_TM, ROW_SUB, LANES), lambda i: (prev(i), 0, 0)),
            pl.BlockSpec((LG_ROWS, OUT_TM), lambda i: (0, prev(i))),
        ],
        out_shape=[
            jax.ShapeDtypeStruct((T, D_MODEL), F32),
            jax.ShapeDtypeStruct((T, ROW_SUB, LANES), ROW_DT),
            jax.ShapeDtypeStruct((LG_ROWS, T), F32),
        ],
        scratch_shapes=[pltpu.VMEM((2, OUT_TM, D_MODEL), F32)],
        compiler_params=_cparams(("arbitrary",)),
        name="out_proj",
    )(out_a, out_b, x2, w_out_bf, w_out_bf, g, w_rt, b_rt)


def _route_kernel(lg_ref, tri_ref, idr_ref, cw_ref, cnt_ref, carry_ref):
    i = pl.program_id(0)

    @pl.when(i == 0)
    def _():
        carry_ref[...] = jnp.zeros_like(carry_ref)

    tn = lg_ref.shape[1]
    row = lax.broadcasted_iota(I32, (SUBLANES, tn), 0)
    neg = jnp.float32(-jnp.inf)

    def top1(v):
        mx = jnp.max(v, axis=0, keepdims=True)
        idx = jnp.min(jnp.where(v == mx, row, SUBLANES), axis=0, keepdims=True)
        return mx, idx

    g = jnp.where(row < N_GROUPS, lg_ref[0:SUBLANES, :], neg)
    gmax, gidx = top1(g)
    g_w = 1.0 / jnp.sum(jnp.exp(g - gmax), axis=0, keepdims=True)
    sel = lg_ref[SUBLANES * N_GROUPS:SUBLANES * (N_GROUPS + 1), :]
    for grp in range(N_GROUPS - 2, -1, -1):
        sel = jnp.where(gidx == grp, lg_ref[SUBLANES * (grp + 1):SUBLANES * (grp + 2), :], sel)
    e1, i1 = top1(sel)
    e2, i2 = top1(jnp.where(row == i1, neg, sel))
    d = jnp.exp(e2 - e1)
    w1 = 1.0 / (1.0 + d)
    w2 = d * w1
    id0 = gidx * EPG + i1
    id1 = gidx * EPG + i2

    erow = lax.broadcasted_iota(I32, (N_EXPERTS, tn), 0)
    o0 = erow == id0
    o1 = erow == id1
    occ = jnp.where(o0 | o1, 1.0, 0.0).astype(F32)
    before = jnp.dot(occ.astype(BF16), tri_ref[...], preferred_element_type=F32) + carry_ref[:, 0:1]
    r0 = jnp.sum(jnp.where(o0, before, 0.0), axis=0, keepdims=True)
    r1 = jnp.sum(jnp.where(o1, before, 0.0), axis=0, keepdims=True)
    total = carry_ref[...] + jnp.sum(occ, axis=1, keepdims=True)
    carry_ref[...] = total
    cnt_ref[...] = total.astype(I32)

    zi = jnp.zeros((SUBLANES - 4, tn), I32)
    idr_ref[...] = jnp.concatenate([id0, id1, r0.astype(I32), r1.astype(I32), zi], axis=0)
    zf = jnp.zeros((SUBLANES - 2, tn), F32)
    cw_ref[...] = jnp.concatenate([g_w * w1, g_w * w2, zf], axis=0)


def _route(lg_t, tri):
    T = lg_t.shape[1]
    return pl.pallas_call(
        _route_kernel,
        grid=(T // RT_TN,),
        in_specs=[
            pl.BlockSpec((LG_ROWS, RT_TN), lambda i: (0, i)),
            pl.BlockSpec((RT_TN, RT_TN), lambda i: (0, 0)),
        ],
        out_specs=[
            pl.BlockSpec((SUBLANES, RT_TN), lambda i: (0, i)),
            pl.BlockSpec((SUBLANES, RT_TN), lambda i: (0, i)),
            pl.BlockSpec((N_EXPERTS, LANES), lambda i: (0, 0)),
        ],
        out_shape=[
            jax.ShapeDtypeStruct((SUBLANES, T), I32),
            jax.ShapeDtypeStruct((SUBLANES, T), F32),
            jax.ShapeDtypeStruct((N_EXPERTS, LANES), I32),
        ],
        scratch_shapes=[pltpu.VMEM((N_EXPERTS, LANES), F32)],
        compiler_params=_cparams(("arbitrary",)),
        name="route",
    )(lg_t, tri)


def _row_copy(src_ref, src_row, dst_ref, dst_row, sem):
    return pltpu.make_async_copy(src_ref.at[src_row], dst_ref.at[dst_row], sem)


def _dispatch_kernel(zt_ref, pos_ref, hm_ref, xs_ref, zbuf, sem):
    i = pl.program_id(0)

    def zero_copy(e):
        return pltpu.make_async_copy(
            zbuf, xs_ref.at[pl.ds(zt_ref[e], EXP_BLK)], sem)

    @pl.when(i == 0)
    def _():
        zbuf[...] = jnp.zeros_like(zbuf)
        for e in range(2 * N_EXPERTS):
            @pl.when(zt_ref[e] >= 0)
            def _():
                zero_copy(e).start()
        for e in range(2 * N_EXPERTS):
            @pl.when(zt_ref[e] >= 0)
            def _():
                zero_copy(e).wait()

    def issue(grp, carry):
        for u in range(DMA_UNROLL):
            t = grp * DMA_UNROLL + u
            for k in range(TOP_K):
                _row_copy(hm_ref, t, xs_ref, pos_ref[k, t], sem).start()
        return carry

    lax.fori_loop(0, DSP_TC // DMA_UNROLL, issue, 0)

    def drain(grp, carry):
        for _ in range(DMA_UNROLL * TOP_K):
            _row_copy(hm_ref, 0, xs_ref, 0, sem).wait()
        return carry

    lax.fori_loop(0, DSP_TC // DMA_UNROLL, drain, 0)


def _dispatch(zt, pos, hm3, n_rows):
    T = pos.shape[1]
    grid_spec = pltpu.PrefetchScalarGridSpec(
        num_scalar_prefetch=1,
        grid=(T // DSP_TC,),
        in_specs=[
            pl.BlockSpec((TOP_K, DSP_TC), lambda i, zt: (0, i), memory_space=pltpu.SMEM),
            pl.BlockSpec((DSP_TC, ROW_SUB, LANES), lambda i, zt: (i, 0, 0)),
        ],
        out_specs=pl.BlockSpec(memory_space=pl.ANY),
        scratch_shapes=[pltpu.VMEM((EXP_BLK, ROW_SUB, LANES), ROW_DT), pltpu.SemaphoreType.DMA(())],
    )
    return pl.pallas_call(
        _dispatch_kernel,
        grid_spec=grid_spec,
        out_shape=jax.ShapeDtypeStruct((n_rows, ROW_SUB, LANES), ROW_DT),
        compiler_params=_cparams(("arbitrary",)),
        name="dispatch",
    )(zt, pos, hm3)


def _expert_kernel(be_ref, first_ref, slot_ref, nxt_ref, nu_ref, xs_ref, wg_hbm, wu_hbm, wd_hbm,
                   y_ref, xb_ref, wg_buf, wu_buf, wd_buf, sems):
    b = pl.program_id(0)

    def weight_copies(e, s):
        return [pltpu.make_async_copy(hbm.at[e], buf.at[s], sems.at[s])
                for hbm, buf in ((wg_hbm, wg_buf), (wu_hbm, wu_buf), (wd_hbm, wd_buf))]

    @pl.when(b < nu_ref[0])
    def _():
        s = slot_ref[b]

        @pl.when(b == 0)
        def _():
            for cp in weight_copies(be_ref[0], 0):
                cp.start()

        @pl.when(first_ref[b] == 1)
        def _():
            for cp in weight_copies(be_ref[b], s):
                cp.wait()

            @pl.when(nxt_ref[b] >= 0)
            def _():
                for cp in weight_copies(nxt_ref[b], 1 - s):
                    cp.start()

        slabs = _from_token_rows(xs_ref[...])
        for k in range(ROW_SUB):
            xb_ref[:, k * LANES:(k + 1) * LANES] = slabs[k]
        x = xb_ref[...]
        g = jnp.dot(x, wg_buf[s], preferred_element_type=F32)
        u = jnp.dot(x, wu_buf[s], preferred_element_type=F32)
        a = (jax.nn.silu(g) * u).astype(BF16)
        y = jnp.dot(a, wd_buf[s], preferred_element_type=F32)
        y_ref[...] = _to_token_rows(y.astype(ROW_DT))

    @pl.when(b >= nu_ref[0])
    def _():
        y_ref[...] = jnp.zeros_like(y_ref)


def _experts(block_e, first, slot, nxt, n_used, xs3, wg, wu, wd):
    n_blocks = xs3.shape[0] // EXP_BLK

    def xs_map(b, be, first, slot, nxt, nu):
        return (jnp.minimum(b, nu[0] - 1), 0, 0)

    grid_spec = pltpu.PrefetchScalarGridSpec(
        num_scalar_prefetch=5,
        grid=(n_blocks,),
        in_specs=[
            pl.BlockSpec((EXP_BLK, ROW_SUB, LANES), xs_map),
            pl.BlockSpec(memory_space=pl.ANY),
            pl.BlockSpec(memory_space=pl.ANY),
            pl.BlockSpec(memory_space=pl.ANY),
        ],
        out_specs=pl.BlockSpec((EXP_BLK, ROW_SUB, LANES), lambda b, *_: (b, 0, 0)),
        scratch_shapes=[
            pltpu.VMEM((EXP_BLK, D_MODEL), BF16),
            pltpu.VMEM((2, D_MODEL, D_EXPERT), BF16),
            pltpu.VMEM((2, D_MODEL, D_EXPERT), BF16),
            pltpu.VMEM((2, D_EXPERT, D_MODEL), BF16),
            pltpu.SemaphoreType.DMA((2,)),
        ],
    )
    return pl.pallas_call(
        _expert_kernel,
        grid_spec=grid_spec,
        out_shape=jax.ShapeDtypeStruct(xs3.shape, ROW_DT),
        compiler_params=_cparams(("arbitrary",)),
        name="experts",
    )(block_e, first, slot, nxt, n_used, xs3, wg, wu, wd)


def _combine_kernel(pos_ref, posn_ref, x1_ref, cw_ref, y_ref, o_ref, yb, sems):
    i = pl.program_id(0)
    slot = i % 2

    def gather(ids_ref, slot_):
        def issue(grp, carry):
            for u in range(DMA_UNROLL):
                t = grp * DMA_UNROLL + u
                for k in range(TOP_K):
                    _row_copy(y_ref, ids_ref[k, t], yb.at[slot_, k], t, sems.at[slot_]).start()
            return carry

        lax.fori_loop(0, CMB_TC // DMA_UNROLL, issue, 0)

    def gather_inline(ids_ref, slot_):
        for t in range(CMB_TC):
            for k in range(TOP_K):
                _row_copy(y_ref, ids_ref[k, t], yb.at[slot_, k], t, sems.at[slot_]).start()

    def weighted_sum():
        w_t = cw_ref[...].T
        w0 = w_t[:, 0:1]
        w1 = w_t[:, 1:2]
        y0 = _from_token_rows(yb[slot, 0])
        y1 = _from_token_rows(yb[slot, 1])
        for s in range(ROW_SUB):
            sl = slice(s * LANES, (s + 1) * LANES)
            o_ref[:, sl] = x1_ref[:, sl] + w0 * y0[s].astype(F32) + w1 * y1[s].astype(F32)

    @pl.when(i == 0)
    def _():
        gather(pos_ref, 0)

    def drain(grp, carry):
        for _ in range(DMA_UNROLL * TOP_K):
            _row_copy(y_ref, 0, yb.at[slot, 0], 0, sems.at[slot]).wait()
        return carry

    lax.fori_loop(0, CMB_TC // DMA_UNROLL, drain, 0)

    @pl.when(i + 1 < pl.num_programs(0))
    def _():
        gather_inline(posn_ref, 1 - slot)
        weighted_sum()

    @pl.when(i + 1 == pl.num_programs(0))
    def _():
        weighted_sum()


def _combine(pos, x1, cw, y3):
    T = x1.shape[0]
    steps = T // CMB_TC
    return pl.pallas_call(
        _combine_kernel,
        grid=(steps,),
        in_specs=[
            pl.BlockSpec((TOP_K, CMB_TC), lambda i: (0, i), memory_space=pltpu.SMEM),
            pl.BlockSpec((TOP_K, CMB_TC), lambda i: (0, jnp.minimum(i + 1, steps - 1)),
                         memory_space=pltpu.SMEM),
            pl.BlockSpec((CMB_TC, D_MODEL), lambda i: (i, 0)),
            pl.BlockSpec((SUBLANES, CMB_TC), lambda i: (0, i)),
            pl.BlockSpec(memory_space=pl.ANY),
        ],
        out_specs=pl.BlockSpec((CMB_TC, D_MODEL), lambda i: (i, 0)),
        out_shape=jax.ShapeDtypeStruct((T, D_MODEL), F32),
        scratch_shapes=[
            pltpu.VMEM((2, TOP_K, CMB_TC, ROW_SUB, LANES), ROW_DT),
            pltpu.SemaphoreType.DMA((2,)),
        ],
        compiler_params=_cparams(("arbitrary",)),
        name="combine",
    )(pos, pos, x1, cw, y3)


def _rotary_lane_tables(seq):
    pos = jnp.arange(seq, dtype=F32)
    inv_freq = 1.0 / (jnp.float32(ROPE_THETA) ** (jnp.arange(0, ROT_DIM, 2, dtype=F32) / ROT_DIM))
    ang = pos[:, None] * inv_freq[None, :]
    cos, sin = jnp.cos(ang), jnp.sin(ang)
    half = ROT_DIM // 2
    lane = jnp.arange(LANES) % QK_DIM
    fidx = lane % half
    first = (lane < half)[None, :]
    second = ((lane >= half) & (lane < ROT_DIM))[None, :]
    cos_t = jnp.where(first | second, cos[:, fidx], 1.0)
    s_up = jnp.where(first, -sin[:, fidx], 0.0)
    s_dn = jnp.where(second, sin[:, fidx], 0.0)
    return cos_t, s_up, s_dn


def kernel(x, attn_norm_g, w_in, gmlp_ln_g, gmlp_ln_b, gmlp_ws, gmlp_bs, q_norm_g, k_norm_g,
           lambda_q1, lambda_k1, lambda_q2, lambda_k2, subln_g, w_out, ffn_norm_g,
           w_group, b_group, w_router, b_router, w_gate, w_up, w_down):
    bsz, seq, d = x.shape
    T = bsz * seq
    assert d == D_MODEL and seq % IN_TM == 0 and seq % ATT_TQ == 0 and T % RT_TN == 0
    l = 0
    x2 = x.reshape(T, d)

    params = jnp.zeros((8, IN_WIDTH), F32)
    params = params.at[0, A_WIDTH:2 * A_WIDTH].set(gmlp_ln_g[l].reshape(-1))
    params = params.at[1, A_WIDTH:2 * A_WIDTH].set(gmlp_ln_b[l].reshape(-1))
    params = params.at[0, 2 * A_WIDTH:3 * A_WIDTH].set(jnp.tile(q_norm_g[l], 2 * B_HEADS))
    params = params.at[0, 3 * A_WIDTH:4 * A_WIDTH].set(jnp.tile(k_norm_g[l], 2 * B_HEADS))
    blk = jnp.arange(IN_GRP) // QK_DIM
    bd = jnp.where(blk[:, None] == blk[None, :], 1.0 / QK_DIM, 0.0).astype(BF16)
    cos_t, s_up, s_dn = _rotary_lane_tables(seq)
    bs_full = jnp.repeat(jnp.transpose(gmlp_bs[l]), HEAD, axis=1)
    lam = (jnp.exp(jnp.sum(lambda_q1[l] * lambda_k1[l])) - jnp.exp(jnp.sum(lambda_q2[l] * lambda_k2[l]))
           + LAMBDA_INIT).reshape(1).astype(F32)
    w_rt = jnp.zeros((LG_ROWS, D_MODEL), F32)
    w_rt = w_rt.at[:N_GROUPS].set(w_group[l].T).at[SUBLANES:].set(w_router[l].T)
    b_rt = jnp.zeros((LG_ROWS, 1), F32)
    b_rt = b_rt.at[:N_GROUPS, 0].set(b_group[l]).at[SUBLANES:, 0].set(b_router[l])
    tri = (jnp.arange(RT_TN)[:, None] < jnp.arange(RT_TN)[None, :]).astype(BF16)

    zp, vb = _in_proj(x2, attn_norm_g[l].reshape(1, d), w_in[l].astype(BF16), params, bd,
                      cos_t, s_up, s_dn, seq)
    out_a = _gmlp(zp, gmlp_ws[l].astype(BF16), bs_full)
    att_steps = bsz * B_HEADS * (seq // ATT_TQ)
    assert (N_EXPERTS * D_MODEL) % att_steps == 0 and (N_EXPERTS * D_EXPERT) % att_steps == 0
    out_b, wg_bf, wu_bf, wd_bf = _attention(
        lam, zp.reshape(bsz, seq, IN_WIDTH - IN_TN), vb.reshape(bsz, seq, IN_TN), subln_g[l].reshape(1, HEAD),
        w_gate[l].reshape(N_EXPERTS * D_MODEL, D_EXPERT), w_up[l].reshape(N_EXPERTS * D_MODEL, D_EXPERT),
        w_down[l].reshape(N_EXPERTS * D_EXPERT, D_MODEL))
    x1, hm3, lg_t = _out_proj(out_a, out_b.reshape(T, A_WIDTH), x2, w_out[l].astype(BF16),
                              ffn_norm_g[l].reshape(1, d), w_rt.astype(BF16), b_rt)

    idr, cw, cnt = _route(lg_t, tri)
    counts = cnt[:, 0]
    padded = (counts + EXP_BLK - 1) // EXP_BLK * EXP_BLK
    pend = jnp.cumsum(padded)
    pstart = (pend - padded).astype(I32)
    n_blocks = (T * TOP_K) // EXP_BLK + N_EXPERTS
    n_rows = n_blocks * EXP_BLK
    block_e = jnp.minimum(jnp.sum(pend[None, :] <= (jnp.arange(n_blocks) * EXP_BLK)[:, None], axis=1),
                          N_EXPERTS - 1).astype(I32)
    n_used = (pend[-1] // EXP_BLK).reshape(1).astype(I32)
    eids = jnp.arange(N_EXPERTS, dtype=I32)
    used = jnp.arange(n_blocks) < n_used[0]
    first = ((block_e != jnp.concatenate([jnp.full((1,), -1, I32), block_e[:-1]])) & used).astype(I32)
    slot = ((jnp.cumsum(first) - 1) % 2).astype(I32)
    later = (eids[None, :] > eids[:, None]) & (counts > 0)[None, :]
    nxt_of = jnp.min(jnp.where(later, eids[None, :], N_EXPERTS), axis=1)
    nxt_of = jnp.where(nxt_of == N_EXPERTS, -1, nxt_of)
    nxt = jnp.sum(jnp.where(block_e[:, None] == eids[None, :], nxt_of[None, :], 0), axis=1).astype(I32)
    trail =pend[-1] + jnp.arange(N_EXPERTS) * EXP_BLK
    zt = jnp.concatenate([jnp.where(counts > 0, pend - EXP_BLK, -1),
                          jnp.where(trail < n_rows, trail, -1)]).astype(I32)

    is_e = idr[0:TOP_K, None, :] == jnp.arange(N_EXPERTS, dtype=I32)[None, :, None]
    pos = jnp.sum(jnp.where(is_e, pstart[None, :, None], 0), axis=1) + idr[TOP_K:2 * TOP_K]
    xs3 = _dispatch(zt, pos, hm3, n_rows)
    y3 = _experts(block_e, first, slot, nxt, n_used, xs3, wg_bf.reshape(N_EXPERTS, D_MODEL, D_EXPERT),
                  wu_bf.reshape(N_EXPERTS, D_MODEL, D_EXPERT), wd_bf.reshape(N_EXPERTS, D_EXPERT, D_MODEL))
    out = _combine(pos, x1, cw, y3)
    return out.reshape(bsz, seq, d)
```

```python
import functools
import math

import jax
import jax.numpy as jnp
from jax import lax
from jax.experimental import pallas as pl
from jax.experimental.pallas import tpu as pltpu

F32 = jnp.float32
BF16 = jnp.bfloat16
I32 = jnp.int32

D_MODEL = 2048
A_WIDTH = 1024
A_HEADS = 8
HEAD = 128
CHUNK = 128
B_HEADS = 8
QK_DIM = 64
ROT_DIM = 16
ROPE_THETA = 500000.0
IN_WIDTH = 5120
N_GROUPS = 4
EPG = 8
N_EXPERTS = 32
TOP_K = 2
D_EXPERT = 1024
EPS = 1e-6
LAMBDA_INIT = 0.8 - 0.6 * math.exp(-0.3 * 0)

LANES = 128
SUBLANES = 8
ROW_SUB = D_MODEL // LANES
ROW_DT = jnp.bfloat16
LG_ROWS = SUBLANES * (1 + N_GROUPS)

IN_TM = 512
IN_TN = 1024
IN_GRP = 256
IN_ROWS = 64
GM_TM = 512
ATT_TQ = 1024
ATT_TK = 512
ONES_ROWS = 16
OUT_TM = 256
RT_TN = 512
EXP_BLK = 256
DSP_TC = 1024
CMB_TC = 512
DMA_UNROLL = 8
VMEM_LIMIT = 48 * 1024 * 1024


def _cparams(sem):
    return pltpu.CompilerParams(dimension_semantics=sem, vmem_limit_bytes=VMEM_LIMIT)


def _to_token_rows(x):
    slabs = jnp.stack([x[:, s * LANES:(s + 1) * LANES] for s in range(ROW_SUB)], axis=0)
    return jnp.transpose(slabs, (1, 0, 2))


def _from_token_rows(x3):
    return jnp.transpose(x3, (1, 0, 2))


def _in_proj_kernel(x_ref, g_ref, w_ref, p_ref, bd_ref, c_ref, s1_ref, s2_ref, o_ref, v_ref, hn_ref, z_ref):
    j = pl.program_id(1)
    tm = hn_ref.shape[0]
    ngrp = IN_TN // IN_GRP

    @pl.when(j == 0)
    def _():
        x = x_ref[...]
        ms = jnp.mean(x * x, axis=-1, keepdims=True)
        hn_ref[...] = (x * lax.rsqrt(ms + EPS) * g_ref[...]).astype(BF16)

    def project(slot):
        z_ref[slot] = jnp.dot(hn_ref[...], w_ref[...], preferred_element_type=F32)

    def gelu_section(slot):
        for r in range(tm // IN_ROWS):
            rows = slice(r * IN_ROWS, (r + 1) * IN_ROWS)
            o_ref[rows, :] = jax.nn.gelu(z_ref[slot, rows, :]).astype(BF16)

    def gelu_ln_section(slot):
        for s in range(IN_TN // LANES):
            sl = slice(s * LANES, (s + 1) * LANES)
            gz = jax.nn.gelu(z_ref[slot, :, sl])
            mu = jnp.mean(gz, axis=-1, keepdims=True)
            xc = gz - mu
            var = jnp.mean(xc * xc, axis=-1, keepdims=True)
            o_ref[:, sl] = (xc * lax.rsqrt(var + EPS) * p_ref[0:1, sl] + p_ref[1:2, sl]).astype(BF16)

    def qk_section(slot, scale):
        sq = jnp.concatenate([z_ref[slot, :, g * IN_GRP:(g + 1) * IN_GRP] for g in range(ngrp)], axis=0)
        ms = jnp.dot((sq * sq).astype(BF16), bd_ref[...], preferred_element_type=F32)
        for s in range(IN_TN // LANES):
            sl = slice(s * LANES, (s + 1) * LANES)
            g, c = divmod(s * LANES, IN_GRP)
            ms_s = ms[g * tm:(g + 1) * tm, c:c + LANES]
            y = z_ref[slot, :, sl] * lax.rsqrt(ms_s + EPS) * p_ref[0:1, sl]
            r = (y * c_ref[...] + pltpu.roll(y, LANES - ROT_DIM // 2, 1) * s1_ref[...]
                 + pltpu.roll(y, ROT_DIM // 2, 1) * s2_ref[...])
            o_ref[:, sl] = (r * scale if scale != 1.0 else r).astype(BF16)

    @pl.when(j == 0)
    def _():
        project(0)

    @pl.when(j == 1)
    def _():
        gelu_section(0)
        project(1)

    @pl.when(j == 2)
    def _():
        gelu_ln_section(1)
        project(0)

    @pl.when(j == 3)
    def _():
        qk_section(0, math.log2(math.e) / math.sqrt(QK_DIM))
        project(1)

    @pl.when(j == 4)
    def _():
        qk_section(1, 1.0)
        v_ref[...] = jnp.dot(hn_ref[...], w_ref[...], preferred_element_type=F32).astype(BF16)


def _in_proj(x2, g, w_bf, params, bd, cos_t, sup_t, sdn_t, seq):
    T = x2.shape[0]
    spt = seq // IN_TM
    nsec = IN_WIDTH // IN_TN
    assert nsec == 5
    return pl.pallas_call(
        _in_proj_kernel,
        grid=(T // IN_TM, nsec),
        in_specs=[
            pl.BlockSpec((IN_TM, D_MODEL), lambda i, j: (i, 0)),
            pl.BlockSpec((1, D_MODEL), lambda i, j: (0, 0)),
            pl.BlockSpec((D_MODEL, IN_TN), lambda i, j: (0, j)),
            pl.BlockSpec((8, IN_TN), lambda i, j: (0, jnp.maximum(j - 1, 0))),
            pl.BlockSpec((IN_GRP, IN_GRP), lambda i, j: (0, 0)),
            pl.BlockSpec((IN_TM, LANES), lambda i, j: (i % spt, 0)),
            pl.BlockSpec((IN_TM, LANES), lambda i, j: (i % spt, 0)),
            pl.BlockSpec((IN_TM, LANES), lambda i, j: (i % spt, 0)),
        ],
        out_specs=[
            pl.BlockSpec((IN_TM, IN_TN), lambda i, j: (i, jnp.maximum(j - 1, 0))),
            pl.BlockSpec((IN_TM, IN_TN), lambda i, j: (i, 0)),
        ],
        out_shape=[
            jax.ShapeDtypeStruct((T, IN_WIDTH - IN_TN), BF16),
            jax.ShapeDtypeStruct((T, IN_TN), BF16),
        ],
        scratch_shapes=[pltpu.VMEM((IN_TM, D_MODEL), BF16), pltpu.VMEM((2, IN_TM, IN_TN), F32)],
        compiler_params=_cparams(("parallel", "arbitrary")),
        name="in_proj",
    )(x2, g, w_bf, params, bd, cos_t, sup_t, sdn_t)


def _gmlp_kernel(u_ref, v_ref, ws_ref, bs_ref, o_ref):
    for c in range(GM_TM // CHUNK):
        rows = slice(c * CHUNK, (c + 1) * CHUNK)
        for h in range(A_HEADS):
            cols = slice(h * HEAD, (h + 1) * HEAD)
            s = jnp.dot(ws_ref[h], v_ref[rows, cols], preferred_element_type=F32) + bs_ref[:, cols]
            o_ref[rows, cols] = (u_ref[rows, cols].astype(F32) * s).astype(BF16)


def _gmlp(zp, ws_bf, bs_full):
    T = zp.shape[0]
    return pl.pallas_call(
        _gmlp_kernel,
        grid=(T // GM_TM,),
        in_specs=[
            pl.BlockSpec((GM_TM, A_WIDTH), lambda i: (i, 0)),
            pl.BlockSpec((GM_TM, A_WIDTH), lambda i: (i, 1)),
            pl.BlockSpec((A_HEADS, CHUNK, CHUNK), lambda i: (0, 0, 0)),
            pl.BlockSpec((CHUNK, A_WIDTH), lambda i: (0, 0)),
        ],
        out_specs=pl.BlockSpec((GM_TM, A_WIDTH), lambda i: (i, 0)),
        out_shape=jax.ShapeDtypeStruct((T, A_WIDTH), BF16),
        compiler_params=_cparams(("parallel",)),
        name="gmlp",
    )(zp, zp, ws_bf, bs_full)


def _attn_kernel(n_tiles, nq, lam_ref, qc_ref, qn_ref, k_ref, v_ref, g_ref, wg_ref, wu_ref, wd_ref,
                 o_ref, wgb_ref, wub_ref, wdb_ref, vt_ref, qt_ref, acc_ref):
    l = pl.program_id(0)
    slot = l % 2
    seq = k_ref.shape[0]
    nc = seq // ATT_TK
    tq = qc_ref.shape[0]

    def prep_q(q_ref, s):
        q_t = q_ref[...].astype(F32).T
        row = lax.broadcasted_iota(I32, q_t.shape, 0)
        qt_ref[s, 0] = jnp.where(row < QK_DIM, q_t, 0.0).astype(BF16)
        qt_ref[s, 1] = jnp.where(row >= QK_DIM, q_t, 0.0).astype(BF16)

    def finish(s):
        a1, a2 = acc_ref[s, 0], acc_ref[s, 1]
        o_t = (a1[0:HEAD] / a1[HEAD:HEAD + 1]
               - lam_ref[0] * (a2[0:HEAD] / a2[HEAD:HEAD + 1]))
        o = o_t.T
        ms = jnp.mean(o * o, axis=-1, keepdims=True)
        o_ref[...] = (o * lax.rsqrt(ms + EPS) * g_ref[...] * (1.0 - LAMBDA_INIT)).astype(BF16)

    def scores(c, qh_t):
        return jnp.dot(k_ref[c * ATT_TK:(c + 1) * ATT_TK, :], qh_t, preferred_element_type=F32)

    def update(c, s_t, m, acc):
        m_new = jnp.maximum(m, jnp.max(s_t, axis=0, keepdims=True))
        p = jnp.exp2(s_t - m_new).astype(BF16)
        pv = jnp.dot(vt_ref[:, c * ATT_TK:(c + 1) * ATT_TK], p, preferred_element_type=F32)
        return m_new, jnp.exp2(m - m_new) * acc + pv

    def attend(s):
        q1_t, q2_t = qt_ref[s, 0], qt_ref[s, 1]
        m1 = m2 = jnp.full((1, tq), -jnp.inf, F32)
        a1 = a2 = jnp.zeros((HEAD + ONES_ROWS, tq), F32)
        s1, s2 = scores(0, q1_t), scores(0, q2_t)
        for c in range(nc):
            if c + 1 < nc:
                n1, n2 = scores(c + 1, q1_t), scores(c + 1, q2_t)
            m1, a1 = update(c, s1, m1, a1)
            m2, a2 = update(c, s2, m2, a2)
            if c + 1 < nc:
                s1, s2 = n1, n2
        acc_ref[s, 0] = a1
        acc_ref[s, 1] = a2

    @pl.when(l == 0)
    def _():
        prep_q(qc_ref, 0)
        acc_ref[1] = jnp.ones(acc_ref.shape[1:], F32)

    @pl.when((l < n_tiles) & (l % nq == 0))
    def _():
        for c in range(nc):
            cols = slice(c * ATT_TK, (c + 1) * ATT_TK)
            vt_ref[0:HEAD, cols] = v_ref[cols, :].astype(F32).T.astype(BF16)
        vt_ref[HEAD:, :] = jnp.ones((ONES_ROWS, seq), BF16)

    @pl.when(l < n_tiles)
    def _():
        finish(1 - slot)
        prep_q(qn_ref, 1 - slot)
        wgb_ref[...] = wg_ref[...].astype(BF16)
        wub_ref[...] = wu_ref[...].astype(BF16)
        wdb_ref[...] = wd_ref[...].astype(BF16)
        attend(slot)

    @pl.when(l == n_tiles)
    def _():
        finish(1 - slot)


def _attention(lam, zp3, v3, subln_g, wg2, wu2, wd2):
    bsz, seq, _ = zp3.shape
    nq = seq // ATT_TQ
    steps = bsz * B_HEADS * nq
    qb, kb, vb = (2 * A_WIDTH) // HEAD, (2 * A_WIDTH + 1024) // HEAD, 0

    def tile(t):
        return t // (B_HEADS * nq), (t // nq) % B_HEADS, t % nq

    def cur(l):
        return jnp.minimum(l, steps - 1)

    def q_map(l, lam, ahead):
        b, h, i = tile(jnp.minimum(l + ahead, steps - 1))
        return b, i, qb + h

    def kv_map(l, lam, base):
        b, h, _ = tile(cur(l))
        return b, 0, base + h

    def o_map(l, lam):
        b, h, i = tile(jnp.maximum(l - 1, 0))
        return b, i, h

    def w_spec(w):
        rows = w.shape[0] // steps
        return pl.BlockSpec((rows, w.shape[1]), lambda l, lam: (cur(l), 0))

    w_specs = [w_spec(wg2), w_spec(wu2), w_spec(wd2)]
    grid_spec = pltpu.PrefetchScalarGridSpec(
        num_scalar_prefetch=1,
        grid=(steps + 1,),
        in_specs=[
            pl.BlockSpec((None, ATT_TQ, HEAD), functools.partial(q_map, ahead=0)),
            pl.BlockSpec((None, ATT_TQ, HEAD), functools.partial(q_map, ahead=1)),
            pl.BlockSpec((None, seq, HEAD), functools.partial(kv_map, base=kb)),
            pl.BlockSpec((None, seq, HEAD), functools.partial(kv_map, base=vb)),
            pl.BlockSpec((1, HEAD), lambda l, lam: (0, 0)),
        ] + w_specs,
        out_specs=[pl.BlockSpec((None, ATT_TQ, HEAD), o_map)] + w_specs,
        scratch_shapes=[
            pltpu.VMEM((HEAD + ONES_ROWS, seq), BF16),
            pltpu.VMEM((2, 2, HEAD, ATT_TQ), BF16),
            pltpu.VMEM((2, 2, HEAD + ONES_ROWS, ATT_TQ), F32),
        ],
    )
    return pl.pallas_call(
        functools.partial(_attn_kernel, steps, nq),
        grid_spec=grid_spec,
        out_shape=[jax.ShapeDtypeStruct((bsz, seq, B_HEADS * HEAD), BF16)]
                  + [jax.ShapeDtypeStruct(w.shape, BF16) for w in (wg2, wu2, wd2)],
        compiler_params=_cparams(("arbitrary",)),
        name="attn",
    )(lam, zp3, zp3, zp3, v3, subln_g, wg2, wu2, wd2)


def _out_proj_kernel(n_tiles, a_ref, b_ref, x_ref, wa_ref, wb_ref, g_ref, wr_ref, br_ref,
                     x1_ref, hm_ref, lg_ref, hbuf):
    i = pl.program_id(0)
    slot = i % 2

    def project(s):
        acc = (jnp.dot(a_ref[...], wa_ref[...], preferred_element_type=F32)
               + jnp.dot(b_ref[...], wb_ref[...], preferred_element_type=F32))
        x1 = x_ref[...] + acc
        x1_ref[...] = x1
        ms = jnp.mean(x1 * x1, axis=-1, keepdims=True)
        hbuf[s] = x1 * lax.rsqrt(ms + EPS) * g_ref[...]

    def emit(s):
        hm = hbuf[s]
        hm_ref[...] = _to_token_rows(hm.astype(BF16))
        lg_ref[...] = lax.dot_general(wr_ref[...], hm.astype(BF16), (((1,), (1,)), ((), ())),
                                      preferred_element_type=F32) + br_ref[...]

    @pl.when(i == 0)
    def _():
        hbuf[1] = jnp.zeros(hbuf.shape[1:], F32)

    @pl.when(i < n_tiles)
    def _():
        emit(1 - slot)
        project(slot)

    @pl.when(i == n_tiles)
    def _():
        emit(1 - slot)


def _out_proj(out_a, out_b, x2, w_out_bf, g, w_rt, b_rt):
    T = x2.shape[0]
    n_tiles = T // OUT_TM

    def cur(i):
        return jnp.minimum(i, n_tiles - 1)

    def prev(i):
        return jnp.maximum(i - 1, 0)

    return pl.pallas_call(
        functools.partial(_out_proj_kernel, n_tiles),
        grid=(n_tiles + 1,),
        in_specs=[
            pl.BlockSpec((OUT_TM, A_WIDTH), lambda i: (cur(i), 0)),
            pl.BlockSpec((OUT_TM, A_WIDTH), lambda i: (cur(i), 0)),
            pl.BlockSpec((OUT_TM, D_MODEL), lambda i: (cur(i), 0)),
            pl.BlockSpec((A_WIDTH, D_MODEL), lambda i: (0, 0)),
            pl.BlockSpec((A_WIDTH, D_MODEL), lambda i: (1, 0)),
            pl.BlockSpec((1, D_MODEL), lambda i: (0, 0)),
            pl.BlockSpec((LG_ROWS, D_MODEL), lambda i: (0, 0)),
            pl.BlockSpec((LG_ROWS, 1), lambda i: (0, 0)),
        ],
        out_specs=[
            pl.BlockSpec((OUT_TM, D_MODEL), lambda i: (cur(i), 0)),
            pl.BlockSpec((OUT_TM, ROW_SUB, LANES), lambda i: (prev(i), 0, 0)),
            pl.BlockSpec((LG_ROWS, OUT_TM), lambda i: (0, prev(i))),
        ],
        out_shape=[
            jax.ShapeDtypeStruct((T, D_MODEL), F32),
            jax.ShapeDtypeStruct((T, ROW_SUB, LANES), ROW_DT),
            jax.ShapeDtypeStruct((LG_ROWS, T), F32),
        ],
        scratch_shapes=[pltpu.VMEM((2, OUT_TM, D_MODEL), F32)],
        compiler_params=_cparams(("arbitrary",)),
        name="out_proj",
    )(out_a, out_b, x2, w_out_bf, w_out_bf, g, w_rt, b_rt)


def _route_kernel(lg_ref, tri_ref, idr_ref, cw_ref, cnt_ref, carry_ref):
    i = pl.program_id(0)

    @pl.when(i == 0)
    def _():
        carry_ref[...] = jnp.zeros_like(carry_ref)

    tn = lg_ref.shape[1]
    row = lax.broadcasted_iota(I32, (SUBLANES, tn), 0)
    neg = jnp.float32(-jnp.inf)

    def top1(v):
        mx = jnp.max(v, axis=0, keepdims=True)
        idx = jnp.min(jnp.where(v == mx, row, SUBLANES), axis=0, keepdims=True)
        return mx, idx

    g = jnp.where(row < N_GROUPS, lg_ref[0:SUBLANES, :], neg)
    gmax, gidx = top1(g)
    g_w = 1.0 / jnp.sum(jnp.exp(g - gmax), axis=0, keepdims=True)
    sel = lg_ref[SUBLANES * N_GROUPS:SUBLANES * (N_GROUPS + 1), :]
    for grp in range(N_GROUPS - 2, -1, -1):
        sel = jnp.where(gidx == grp, lg_ref[SUBLANES * (grp + 1):SUBLANES * (grp + 2), :], sel)
    e1, i1 = top1(sel)
    e2, i2 = top1(jnp.where(row == i1, neg, sel))
    d = jnp.exp(e2 - e1)
    w1 = 1.0 / (1.0 + d)
    w2 = d * w1
    id0 = gidx * EPG + i1
    id1 = gidx * EPG + i2

    erow = lax.broadcasted_iota(I32, (N_EXPERTS, tn), 0)
    o0 = erow == id0
    o1 = erow == id1
    occ = jnp.where(o0 | o1, 1.0, 0.0).astype(F32)
    before = jnp.dot(occ.astype(BF16), tri_ref[...], preferred_element_type=F32) + carry_ref[:, 0:1]
    r0 = jnp.sum(jnp.where(o0, before, 0.0), axis=0, keepdims=True)
    r1 = jnp.sum(jnp.where(o1, before, 0.0), axis=0, keepdims=True)
    total = carry_ref[...] + jnp.sum(occ, axis=1, keepdims=True)
    carry_ref[...] = total
    cnt_ref[...] = total.astype(I32)

    zi = jnp.zeros((SUBLANES - 4, tn), I32)
    idr_ref[...] = jnp.concatenate([id0, id1, r0.astype(I32), r1.astype(I32), zi], axis=0)
    zf = jnp.zeros((SUBLANES - 2, tn), F32)
    cw_ref[...] = jnp.concatenate([g_w * w1, g_w * w2, zf], axis=0)


def _route(lg_t, tri):
    T = lg_t.shape[1]
    return pl.pallas_call(
        _route_kernel,
        grid=(T // RT_TN,),
        in_specs=[
            pl.BlockSpec((LG_ROWS, RT_TN), lambda i: (0, i)),
            pl.BlockSpec((RT_TN, RT_TN), lambda i: (0, 0)),
        ],
        out_specs=[
            pl.BlockSpec((SUBLANES, RT_TN), lambda i: (0, i)),
            pl.BlockSpec((SUBLANES, RT_TN), lambda i: (0, i)),
            pl.BlockSpec((N_EXPERTS, LANES), lambda i: (0, 0)),
        ],
        out_shape=[
            jax.ShapeDtypeStruct((SUBLANES, T), I32),
            jax.ShapeDtypeStruct((SUBLANES, T), F32),
            jax.ShapeDtypeStruct((N_EXPERTS, LANES), I32),
        ],
        scratch_shapes=[pltpu.VMEM((N_EXPERTS, LANES), F32)],
        compiler_params=_cparams(("arbitrary",)),
        name="route",
    )(lg_t, tri)


def _row_copy(src_ref, src_row, dst_ref, dst_row, sem):
    return pltpu.make_async_copy(src_ref.at[src_row], dst_ref.at[dst_row], sem)


def _dispatch_kernel(zt_ref, pos_ref, hm_ref, xs_ref, zbuf, sem):
    i = pl.program_id(0)

    def zero_copy(e):
        return pltpu.make_async_copy(
            zbuf, xs_ref.at[pl.ds(zt_ref[e], EXP_BLK)], sem)

    @pl.when(i == 0)
    def _():
        zbuf[...] = jnp.zeros_like(zbuf)
        for e in range(2 * N_EXPERTS):
            @pl.when(zt_ref[e] >= 0)
            def _():
                zero_copy(e).start()
        for e in range(2 * N_EXPERTS):
            @pl.when(zt_ref[e] >= 0)
            def _():
                zero_copy(e).wait()

    def issue(grp, carry):
        for u in range(DMA_UNROLL):
            t = grp * DMA_UNROLL + u
            for k in range(TOP_K):
                _row_copy(hm_ref, t, xs_ref, pos_ref[k, t], sem).start()
        return carry

    lax.fori_loop(0, DSP_TC // DMA_UNROLL, issue, 0)

    def drain(grp, carry):
        for _ in range(DMA_UNROLL * TOP_K):
            _row_copy(hm_ref, 0, xs_ref, 0, sem).wait()
        return carry

    lax.fori_loop(0, DSP_TC // DMA_UNROLL, drain, 0)


def _dispatch(zt, pos, hm3, n_rows):
    T = pos.shape[1]
    grid_spec = pltpu.PrefetchScalarGridSpec(
        num_scalar_prefetch=1,
        grid=(T // DSP_TC,),
        in_specs=[
            pl.BlockSpec((TOP_K, DSP_TC), lambda i, zt: (0, i), memory_space=pltpu.SMEM),
            pl.BlockSpec((DSP_TC, ROW_SUB, LANES), lambda i, zt: (i, 0, 0)),
        ],
        out_specs=pl.BlockSpec(memory_space=pl.ANY),
        scratch_shapes=[pltpu.VMEM((EXP_BLK, ROW_SUB, LANES), ROW_DT), pltpu.SemaphoreType.DMA(())],
    )
    return pl.pallas_call(
        _dispatch_kernel,
        grid_spec=grid_spec,
        out_shape=jax.ShapeDtypeStruct((n_rows, ROW_SUB, LANES), ROW_DT),
        compiler_params=_cparams(("arbitrary",)),
        name="dispatch",
    )(zt, pos, hm3)


def _expert_kernel(be_ref, first_ref, slot_ref, nxt_ref, nu_ref, xs_ref, wg_hbm, wu_hbm, wd_hbm,
                   y_ref, xb_ref, wg_buf, wu_buf, wd_buf, sems):
    b = pl.program_id(0)

    def weight_copies(e, s):
        return [pltpu.make_async_copy(hbm.at[e], buf.at[s], sems.at[s])
                for hbm, buf in ((wg_hbm, wg_buf), (wu_hbm, wu_buf), (wd_hbm, wd_buf))]

    @pl.when(b < nu_ref[0])
    def _():
        s = slot_ref[b]

        @pl.when(b == 0)
        def _():
            for cp in weight_copies(be_ref[0], 0):
                cp.start()

        @pl.when(first_ref[b] == 1)
        def _():
            for cp in weight_copies(be_ref[b], s):
                cp.wait()

            @pl.when(nxt_ref[b] >= 0)
            def _():
                for cp in weight_copies(nxt_ref[b], 1 - s):
                    cp.start()

        slabs = _from_token_rows(xs_ref[...])
        for k in range(ROW_SUB):
            xb_ref[:, k * LANES:(k + 1) * LANES] = slabs[k]
        x = xb_ref[...]
        g = jnp.dot(x, wg_buf[s], preferred_element_type=F32)
        u = jnp.dot(x, wu_buf[s], preferred_element_type=F32)
        a = (jax.nn.silu(g) * u).astype(BF16)
        y = jnp.dot(a, wd_buf[s], preferred_element_type=F32)
        y_ref[...] = _to_token_rows(y.astype(ROW_DT))

    @pl.when(b >= nu_ref[0])
    def _():
        y_ref[...] = jnp.zeros_like(y_ref)


def _experts(block_e, first, slot, nxt, n_used, xs3, wg, wu, wd):
    n_blocks = xs3.shape[0] // EXP_BLK

    def xs_map(b, be, first, slot, nxt, nu):
        return (jnp.minimum(b, nu[0] - 1), 0, 0)

    grid_spec = pltpu.PrefetchScalarGridSpec(
        num_scalar_prefetch=5,
        grid=(n_blocks,),
        in_specs=[
            pl.BlockSpec((EXP_BLK, ROW_SUB, LANES), xs_map),
            pl.BlockSpec(memory_space=pl.ANY),
            pl.BlockSpec(memory_space=pl.ANY),
            pl.BlockSpec(memory_space=pl.ANY),
        ],
        out_specs=pl.BlockSpec((EXP_BLK, ROW_SUB, LANES), lambda b, *_: (b, 0, 0)),
        scratch_shapes=[
            pltpu.VMEM((EXP_BLK, D_MODEL), BF16),
            pltpu.VMEM((2, D_MODEL, D_EXPERT), BF16),
            pltpu.VMEM((2, D_MODEL, D_EXPERT), BF16),
            pltpu.VMEM((2, D_EXPERT, D_MODEL), BF16),
            pltpu.SemaphoreType.DMA((2,)),
        ],
    )
    return pl.pallas_call(
        _expert_kernel,
        grid_spec=grid_spec,
        out_shape=jax.ShapeDtypeStruct(xs3.shape, ROW_DT),
        compiler_params=_cparams(("arbitrary",)),
        name="experts",
    )(block_e, first, slot, nxt, n_used, xs3, wg, wu, wd)


def _combine_kernel(pos_ref, posn_ref, x1_ref, cw_ref, y_ref, o_ref, yb, sems):
    i = pl.program_id(0)
    slot = i % 2

    def gather(ids_ref, slot_):
        def issue(grp, carry):
            for u in range(DMA_UNROLL):
                t = grp * DMA_UNROLL + u
                for k in range(TOP_K):
                    _row_copy(y_ref, ids_ref[k, t], yb.at[slot_, k], t, sems.at[slot_]).start()
            return carry

        lax.fori_loop(0, CMB_TC // DMA_UNROLL, issue, 0)

    def gather_inline(ids_ref, slot_):
        for t in range(CMB_TC):
            for k in range(TOP_K):
                _row_copy(y_ref, ids_ref[k, t], yb.at[slot_, k], t, sems.at[slot_]).start()

    def weighted_sum():
        w_t = cw_ref[...].T
        w0 = w_t[:, 0:1]
        w1 = w_t[:, 1:2]
        y0 = _from_token_rows(yb[slot, 0])
        y1 = _from_token_rows(yb[slot, 1])
        for s in range(ROW_SUB):
            sl = slice(s * LANES, (s + 1) * LANES)
            o_ref[:, sl] = x1_ref[:, sl] + w0 * y0[s].astype(F32) + w1 * y1[s].astype(F32)

    @pl.when(i == 0)
    def _():
        gather(pos_ref, 0)

    def drain(grp, carry):
        for _ in range(DMA_UNROLL * TOP_K):
            _row_copy(y_ref, 0, yb.at[slot, 0], 0, sems.at[slot]).wait()
        return carry

    lax.fori_loop(0, CMB_TC // DMA_UNROLL, drain, 0)

    @pl.when(i + 1 < pl.num_programs(0))
    def _():
        gather_inline(posn_ref, 1 - slot)
        weighted_sum()

    @pl.when(i + 1 == pl.num_programs(0))
    def _():
        weighted_sum()


def _combine(pos, x1, cw, y3):
    T = x1.shape[0]
    steps = T // CMB_TC
    return pl.pallas_call(
        _combine_kernel,
        grid=(steps,),
        in_specs=[
            pl.BlockSpec((TOP_K, CMB_TC), lambda i: (0, i), memory_space=pltpu.SMEM),
            pl.BlockSpec((TOP_K, CMB_TC), lambda i: (0, jnp.minimum(i + 1, steps - 1)),
                         memory_space=pltpu.SMEM),
            pl.BlockSpec((CMB_TC, D_MODEL), lambda i: (i, 0)),
            pl.BlockSpec((SUBLANES, CMB_TC), lambda i: (0, i)),
            pl.BlockSpec(memory_space=pl.ANY),
        ],
        out_specs=pl.BlockSpec((CMB_TC, D_MODEL), lambda i: (i, 0)),
        out_shape=jax.ShapeDtypeStruct((T, D_MODEL), F32),
        scratch_shapes=[
            pltpu.VMEM((2, TOP_K, CMB_TC, ROW_SUB, LANES), ROW_DT),
            pltpu.SemaphoreType.DMA((2,)),
        ],
        compiler_params=_cparams(("arbitrary",)),
        name="combine",
    )(pos, pos, x1, cw, y3)


def _rotary_lane_tables(seq):
    pos = jnp.arange(seq, dtype=F32)
    inv_freq = 1.0 / (jnp.float32(ROPE_THETA) ** (jnp.arange(0, ROT_DIM, 2, dtype=F32) / ROT_DIM))
    ang = pos[:, None] * inv_freq[None, :]
    cos, sin = jnp.cos(ang), jnp.sin(ang)
    half = ROT_DIM // 2
    lane = jnp.arange(LANES) % QK_DIM
    fidx = lane % half
    first = (lane < half)[None, :]
    second = ((lane >= half) & (lane < ROT_DIM))[None, :]
    cos_t = jnp.where(first | second, cos[:, fidx], 1.0)
    s_up = jnp.where(first, -sin[:, fidx], 0.0)
    s_dn = jnp.where(second, sin[:, fidx], 0.0)
    return cos_t, s_up, s_dn


def kernel(x, attn_norm_g, w_in, gmlp_ln_g, gmlp_ln_b, gmlp_ws, gmlp_bs, q_norm_g, k_norm_g,
           lambda_q1, lambda_k1, lambda_q2, lambda_k2, subln_g, w_out, ffn_norm_g,
           w_group, b_group, w_router, b_router, w_gate, w_up, w_down):
    bsz, seq, d = x.shape
    T = bsz * seq
    assert d == D_MODEL and seq % IN_TM == 0 and seq % ATT_TQ == 0 and T % RT_TN == 0
    l = 0
    x2 = x.reshape(T, d)

    params = jnp.zeros((8, IN_WIDTH), F32)
    params = params.at[0, A_WIDTH:2 * A_WIDTH].set(gmlp_ln_g[l].reshape(-1))
    params = params.at[1, A_WIDTH:2 * A_WIDTH].set(gmlp_ln_b[l].reshape(-1))
    params = params.at[0, 2 * A_WIDTH:3 * A_WIDTH].set(jnp.tile(q_norm_g[l], 2 * B_HEADS))
    params = params.at[0, 3 * A_WIDTH:4 * A_WIDTH].set(jnp.tile(k_norm_g[l], 2 * B_HEADS))
    blk = jnp.arange(IN_GRP) // QK_DIM
    bd = jnp.where(blk[:, None] == blk[None, :], 1.0 / QK_DIM, 0.0).astype(BF16)
    cos_t, s_up, s_dn = _rotary_lane_tables(seq)
    bs_full = jnp.repeat(jnp.transpose(gmlp_bs[l]), HEAD, axis=1)
    lam = (jnp.exp(jnp.sum(lambda_q1[l] * lambda_k1[l])) - jnp.exp(jnp.sum(lambda_q2[l] * lambda_k2[l]))
           + LAMBDA_INIT).reshape(1).astype(F32)
    w_rt = jnp.zeros((LG_ROWS, D_MODEL), F32)
    w_rt = w_rt.at[:N_GROUPS].set(w_group[l].T).at[SUBLANES:].set(w_router[l].T)
    b_rt = jnp.zeros((LG_ROWS, 1), F32)
    b_rt = b_rt.at[:N_GROUPS, 0].set(b_group[l]).at[SUBLANES:, 0].set(b_router[l])
    tri = (jnp.arange(RT_TN)[:, None] < jnp.arange(RT_TN)[None, :]).astype(BF16)

    zp, vb = _in_proj(x2, attn_norm_g[l].reshape(1, d), w_in[l].astype(BF16), params, bd,
                      cos_t, s_up, s_dn, seq)
    out_a = _gmlp(zp, gmlp_ws[l].astype(BF16), bs_full)
    att_steps = bsz * B_HEADS * (seq // ATT_TQ)
    assert (N_EXPERTS * D_MODEL) % att_steps == 0 and (N_EXPERTS * D_EXPERT) % att_steps == 0
    out_b, wg_bf, wu_bf, wd_bf = _attention(
        lam, zp.reshape(bsz, seq, IN_WIDTH - IN_TN), vb.reshape(bsz, seq, IN_TN), subln_g[l].reshape(1, HEAD),
        w_gate[l].reshape(N_EXPERTS * D_MODEL, D_EXPERT), w_up[l].reshape(N_EXPERTS * D_MODEL, D_EXPERT),
        w_down[l].reshape(N_EXPERTS * D_EXPERT, D_MODEL))
    x1, hm3, lg_t = _out_proj(out_a, out_b.reshape(T, A_WIDTH), x2, w_out[l].astype(BF16),
                              ffn_norm_g[l].reshape(1, d), w_rt.astype(BF16), b_rt)

    idr, cw, cnt = _route(lg_t, tri)
    counts = cnt[:, 0]
    padded = (counts + EXP_BLK - 1) // EXP_BLK * EXP_BLK
    pend = jnp.cumsum(padded)
    pstart = (pend - padded).astype(I32)
    n_blocks = (T * TOP_K) // EXP_BLK + N_EXPERTS
    n_rows = n_blocks * EXP_BLK
    block_e = jnp.minimum(jnp.sum(pend[None, :] <= (jnp.arange(n_blocks) * EXP_BLK)[:, None], axis=1),
                          N_EXPERTS - 1).astype(I32)
    n_used = (pend[-1] // EXP_BLK).reshape(1).astype(I32)
    eids = jnp.arange(N_EXPERTS, dtype=I32)
    used = jnp.arange(n_blocks) < n_used[0]
    first = ((block_e != jnp.concatenate([jnp.full((1,), -1, I32), block_e[:-1]])) & used).astype(I32)
    slot = ((jnp.cumsum(first) - 1) % 2).astype(I32)
    later = (eids[None, :] > eids[:, None]) & (counts > 0)[None, :]
    nxt_of = jnp.min(jnp.where(later, eids[None, :], N_EXPERTS), axis=1)
    nxt_of = jnp.where(nxt_of == N_EXPERTS, -1, nxt_of)
    nxt = jnp.sum(jnp.where(block_e[:, None] == eids[None, :], nxt_of[None, :], 0), axis=1).astype(I32)
    trail =pend[-1] + jnp.arange(N_EXPERTS) * EXP_BLK
    zt = jnp.concatenate([jnp.where(counts > 0, pend - EXP_BLK, -1),
                          jnp.where(trail < n_rows, trail, -1)]).astype(I32)

    is_e = idr[0:TOP_K, None, :] == jnp.arange(N_EXPERTS, dtype=I32)[None, :, None]
    pos = jnp.sum(jnp.where(is_e, pstart[None, :, None], 0), axis=1) + idr[TOP_K:2 * TOP_K]
    xs3 = _dispatch(zt, pos, hm3, n_rows)
    y3 = _experts(block_e, first, slot, nxt, n_used, xs3, wg_bf.reshape(N_EXPERTS, D_MODEL, D_EXPERT),
                  wu_bf.reshape(N_EXPERTS, D_MODEL, D_EXPERT), wd_bf.reshape(N_EXPERTS, D_EXPERT, D_MODEL))
    out = _combine(pos, x1, cw, y3)
    return out.reshape(bsz, seq, d)
```

```python
import functools
import math

import jax
import jax.numpy as jnp
from jax import lax
from jax.experimental import pallas as pl
from jax.experimental.pallas import tpu as pltpu

F32 = jnp.float32
BF16 = jnp.bfloat16
I32 = jnp.int32

D_MODEL = 2048
A_WIDTH = 1024
A_HEADS = 8
HEAD = 128
CHUNK = 128
B_HEADS = 8
QK_DIM = 64
ROT_DIM = 16
ROPE_THETA = 500000.0
IN_WIDTH = 5120
N_GROUPS = 4
EPG = 8
N_EXPERTS = 32
TOP_K = 2
D_EXPERT = 1024
EPS = 1e-6
LAMBDA_INIT = 0.8 - 0.6 * math.exp(-0.3 * 0)

LANES = 128
SUBLANES = 8
ROW_SUB = D_MODEL // LANES
ROW_DT = jnp.bfloat16
LG_ROWS = SUBLANES * (1 + N_GROUPS)

IN_TM = 512
IN_TN = 1024
IN_GRP = 256
IN_ROWS = 64
GM_TM = 512
ATT_TQ = 1024
ATT_TK = 512
ONES_ROWS = 16
OUT_TM = 256
RT_TN = 512
EXP_BLK = 256
DSP_TC = 512
CMB_TC = 256
DMA_UNROLL = 8
VMEM_LIMIT = 48 * 1024 * 1024


def _cparams(sem):
    return pltpu.CompilerParams(dimension_semantics=sem, vmem_limit_bytes=VMEM_LIMIT)


def _to_token_rows(x):
    slabs = jnp.stack([x[:, s * LANES:(s + 1) * LANES] for s in range(ROW_SUB)], axis=0)
    return jnp.transpose(slabs, (1, 0, 2))


def _from_token_rows(x3):
    return jnp.transpose(x3, (1, 0, 2))


def _in_proj_kernel(x_ref, g_ref, w_ref, p_ref, bd_ref, c_ref, s1_ref, s2_ref, o_ref, v_ref, hn_ref, z_ref):
    j = pl.program_id(1)
    tm = hn_ref.shape[0]
    ngrp = IN_TN // IN_GRP

    @pl.when(j == 0)
    def _():
        x = x_ref[...]
        ms = jnp.mean(x * x, axis=-1, keepdims=True)
        hn_ref[...] = (x * lax.rsqrt(ms + EPS) * g_ref[...]).astype(BF16)

    def project(slot):
        z_ref[slot] = jnp.dot(hn_ref[...], w_ref[...], preferred_element_type=F32)

    def gelu_section(slot):
        for r in range(tm // IN_ROWS):
            rows = slice(r * IN_ROWS, (r + 1) * IN_ROWS)
            o_ref[rows, :] = jax.nn.gelu(z_ref[slot, rows, :]).astype(BF16)

    def gelu_ln_section(slot):
        for s in range(IN_TN // LANES):
            sl = slice(s * LANES, (s + 1) * LANES)
            gz = jax.nn.gelu(z_ref[slot, :, sl])
            mu = jnp.mean(gz, axis=-1, keepdims=True)
            xc = gz - mu
            var = jnp.mean(xc * xc, axis=-1, keepdims=True)
            o_ref[:, sl] = (xc * lax.rsqrt(var + EPS) * p_ref[0:1, sl] + p_ref[1:2, sl]).astype(BF16)

    def qk_section(slot, scale):
        sq = jnp.concatenate([z_ref[slot, :, g * IN_GRP:(g + 1) * IN_GRP] for g in range(ngrp)], axis=0)
        ms = jnp.dot((sq * sq).astype(BF16), bd_ref[...], preferred_element_type=F32)
        for s in range(IN_TN // LANES):
            sl = slice(s * LANES, (s + 1) * LANES)
            g, c = divmod(s * LANES, IN_GRP)
            ms_s = ms[g * tm:(g + 1) * tm, c:c + LANES]
            y = z_ref[slot, :, sl] * lax.rsqrt(ms_s + EPS) * p_ref[0:1, sl]
            r = (y * c_ref[...] + pltpu.roll(y, LANES - ROT_DIM // 2, 1) * s1_ref[...]
                 + pltpu.roll(y, ROT_DIM // 2, 1) * s2_ref[...])
            o_ref[:, sl] = (r * scale if scale != 1.0 else r).astype(BF16)

    @pl.when(j == 0)
    def _():
        project(0)

    @pl.when(j == 1)
    def _():
        gelu_section(0)
        project(1)

    @pl.when(j == 2)
    def _():
        gelu_ln_section(1)
        project(0)

    @pl.when(j == 3)
    def _():
        qk_section(0, math.log2(math.e) / math.sqrt(QK_DIM))
        project(1)

    @pl.when(j == 4)
    def _():
        qk_section(1, 1.0)
        v_ref[...] = jnp.dot(hn_ref[...], w_ref[...], preferred_element_type=F32).astype(BF16)


def _in_proj(x2, g, w_bf, params, bd, cos_t, sup_t, sdn_t, seq):
    T = x2.shape[0]
    spt = seq // IN_TM
    nsec = IN_WIDTH // IN_TN
    assert nsec == 5
    return pl.pallas_call(
        _in_proj_kernel,
        grid=(T // IN_TM, nsec),
        in_specs=[
            pl.BlockSpec((IN_TM, D_MODEL), lambda i, j: (i, 0)),
            pl.BlockSpec((1, D_MODEL), lambda i, j: (0, 0)),
            pl.BlockSpec((D_MODEL, IN_TN), lambda i, j: (0, j)),
            pl.BlockSpec((8, IN_TN), lambda i, j: (0, jnp.maximum(j - 1, 0))),
            pl.BlockSpec((IN_GRP, IN_GRP), lambda i, j: (0, 0)),
            pl.BlockSpec((IN_TM, LANES), lambda i, j: (i % spt, 0)),
            pl.BlockSpec((IN_TM, LANES), lambda i, j: (i % spt, 0)),
            pl.BlockSpec((IN_TM, LANES), lambda i, j: (i % spt, 0)),
        ],
        out_specs=[
            pl.BlockSpec((IN_TM, IN_TN), lambda i, j: (i, jnp.maximum(j - 1, 0))),
            pl.BlockSpec((IN_TM, IN_TN), lambda i, j: (i, 0)),
        ],
        out_shape=[
            jax.ShapeDtypeStruct((T, IN_WIDTH - IN_TN), BF16),
            jax.ShapeDtypeStruct((T, IN_TN), BF16),
        ],
        scratch_shapes=[pltpu.VMEM((IN_TM, D_MODEL), BF16), pltpu.VMEM((2, IN_TM, IN_TN), F32)],
        compiler_params=_cparams(("parallel", "arbitrary")),
        name="in_proj",
    )(x2, g, w_bf, params, bd, cos_t, sup_t, sdn_t)


def _gmlp_kernel(u_ref, v_ref, ws_ref, bs_ref, o_ref):
    for c in range(GM_TM // CHUNK):
        rows = slice(c * CHUNK, (c + 1) * CHUNK)
        for h in range(A_HEADS):
            cols = slice(h * HEAD, (h + 1) * HEAD)
            s = jnp.dot(ws_ref[h], v_ref[rows, cols], preferred_element_type=F32) + bs_ref[:, cols]
            o_ref[rows, cols] = (u_ref[rows, cols].astype(F32) * s).astype(BF16)


def _gmlp(zp, ws_bf, bs_full):
    T = zp.shape[0]
    return pl.pallas_call(
        _gmlp_kernel,
        grid=(T // GM_TM,),
        in_specs=[
            pl.BlockSpec((GM_TM, A_WIDTH), lambda i: (i, 0)),
            pl.BlockSpec((GM_TM, A_WIDTH), lambda i: (i, 1)),
            pl.BlockSpec((A_HEADS, CHUNK, CHUNK), lambda i: (0, 0, 0)),
            pl.BlockSpec((CHUNK, A_WIDTH), lambda i: (0, 0)),
        ],
        out_specs=pl.BlockSpec((GM_TM, A_WIDTH), lambda i: (i, 0)),
        out_shape=jax.ShapeDtypeStruct((T, A_WIDTH), BF16),
        compiler_params=_cparams(("parallel",)),
        name="gmlp",
    )(zp, zp, ws_bf, bs_full)


def _attn_kernel(n_tiles, nq, lam_ref, qc_ref, qn_ref, k_ref, v_ref, g_ref, wg_ref, wu_ref, wd_ref,
                 o_ref, wgb_ref, wub_ref, wdb_ref, vt_ref, qt_ref, acc_ref):
    l = pl.program_id(0)
    slot = l % 2
    seq = k_ref.shape[0]
    nc = seq // ATT_TK
    tq = qc_ref.shape[0]

    def prep_q(q_ref, s):
        q_t = q_ref[...].astype(F32).T
        row = lax.broadcasted_iota(I32, q_t.shape, 0)
        qt_ref[s, 0] = jnp.where(row < QK_DIM, q_t, 0.0).astype(BF16)
        qt_ref[s, 1] = jnp.where(row >= QK_DIM, q_t, 0.0).astype(BF16)

    def finish(s):
        a1, a2 = acc_ref[s, 0], acc_ref[s, 1]
        o_t = (a1[0:HEAD] / a1[HEAD:HEAD + 1]
               - lam_ref[0] * (a2[0:HEAD] / a2[HEAD:HEAD + 1]))
        o = o_t.T
        ms = jnp.mean(o * o, axis=-1, keepdims=True)
        o_ref[...] = (o * lax.rsqrt(ms + EPS) * g_ref[...] * (1.0 - LAMBDA_INIT)).astype(BF16)

    def scores(c, qh_t):
        return jnp.dot(k_ref[c * ATT_TK:(c + 1) * ATT_TK, :], qh_t, preferred_element_type=F32)

    def update(c, s_t, m, acc):
        m_new = jnp.maximum(m, jnp.max(s_t, axis=0, keepdims=True))
        p = jnp.exp2(s_t - m_new).astype(BF16)
        pv = jnp.dot(vt_ref[:, c * ATT_TK:(c + 1) * ATT_TK], p, preferred_element_type=F32)
        return m_new, jnp.exp2(m - m_new) * acc + pv

    def attend(s):
        q1_t, q2_t = qt_ref[s, 0], qt_ref[s, 1]
        m1 = m2 = jnp.full((1, tq), -jnp.inf, F32)
        a1 = a2 = jnp.zeros((HEAD + ONES_ROWS, tq), F32)
        s1, s2 = scores(0, q1_t), scores(0, q2_t)
        for c in range(nc):
            if c + 1 < nc:
                n1, n2 = scores(c + 1, q1_t), scores(c + 1, q2_t)
            m1, a1 = update(c, s1, m1, a1)
            m2, a2 = update(c, s2, m2, a2)
            if c + 1 < nc:
                s1, s2 = n1, n2
        acc_ref[s, 0] = a1
        acc_ref[s, 1] = a2

    @pl.when(l == 0)
    def _():
        prep_q(qc_ref, 0)
        acc_ref[1] = jnp.ones(acc_ref.shape[1:], F32)

    @pl.when((l < n_tiles) & (l % nq == 0))
    def _():
        for c in range(nc):
            cols = slice(c * ATT_TK, (c + 1) * ATT_TK)
            vt_ref[0:HEAD, cols] = v_ref[cols, :].astype(F32).T.astype(BF16)
        vt_ref[HEAD:, :] = jnp.ones((ONES_ROWS, seq), BF16)

    @pl.when(l < n_tiles)
    def _():
        finish(1 - slot)
        prep_q(qn_ref, 1 - slot)
        wgb_ref[...] = wg_ref[...].astype(BF16)
        wub_ref[...] = wu_ref[...].astype(BF16)
        wdb_ref[...] = wd_ref[...].astype(BF16)
        attend(slot)

    @pl.when(l == n_tiles)
    def _():
        finish(1 - slot)


def _attention(lam, zp3, v3, subln_g, wg2, wu2, wd2):
    bsz, seq, _ = zp3.shape
    nq = seq // ATT_TQ
    steps = bsz * B_HEADS * nq
    qb, kb, vb = (2 * A_WIDTH) // HEAD, (2 * A_WIDTH + 1024) // HEAD, 0

    def tile(t):
        return t // (B_HEADS * nq), (t // nq) % B_HEADS, t % nq

    def cur(l):
        return jnp.minimum(l, steps - 1)

    def q_map(l, lam, ahead):
        b, h, i = tile(jnp.minimum(l + ahead, steps - 1))
        return b, i, qb + h

    def kv_map(l, lam, base):
        b, h, _ = tile(cur(l))
        return b, 0, base + h

    def o_map(l, lam):
        b, h, i = tile(jnp.maximum(l - 1, 0))
        return b, i, h

    def w_spec(w):
        rows = w.shape[0] // steps
        return pl.BlockSpec((rows, w.shape[1]), lambda l, lam: (cur(l), 0))

    w_specs = [w_spec(wg2), w_spec(wu2), w_spec(wd2)]
    grid_spec = pltpu.PrefetchScalarGridSpec(
        num_scalar_prefetch=1,
        grid=(steps + 1,),
        in_specs=[
            pl.BlockSpec((None, ATT_TQ, HEAD), functools.partial(q_map, ahead=0)),
            pl.BlockSpec((None, ATT_TQ, HEAD), functools.partial(q_map, ahead=1)),
            pl.BlockSpec((None, seq, HEAD), functools.partial(kv_map, base=kb)),
            pl.BlockSpec((None, seq, HEAD), functools.partial(kv_map, base=vb)),
            pl.BlockSpec((1, HEAD), lambda l, lam: (0, 0)),
        ] + w_specs,
        out_specs=[pl.BlockSpec((None, ATT_TQ, HEAD), o_map)] + w_specs,
        scratch_shapes=[
            pltpu.VMEM((HEAD + ONES_ROWS, seq), BF16),
            pltpu.VMEM((2, 2, HEAD, ATT_TQ), BF16),
            pltpu.VMEM((2, 2, HEAD + ONES_ROWS, ATT_TQ), F32),
        ],
    )
    return pl.pallas_call(
        functools.partial(_attn_kernel, steps, nq),
        grid_spec=grid_spec,
        out_shape=[jax.ShapeDtypeStruct((bsz, seq, B_HEADS * HEAD), BF16)]
                  + [jax.ShapeDtypeStruct(w.shape, BF16) for w in (wg2, wu2, wd2)],
        compiler_params=_cparams(("arbitrary",)),
        name="attn",
    )(lam, zp3, zp3, zp3, v3, subln_g, wg2, wu2, wd2)


def _out_proj_kernel(n_tiles, a_ref, b_ref, x_ref, wa_ref, wb_ref, g_ref, wr_ref, br_ref,
                     x1_ref, hm_ref, lg_ref, hbuf):
    i = pl.program_id(0)
    slot = i % 2

    def project(s):
        acc = (jnp.dot(a_ref[...], wa_ref[...], preferred_element_type=F32)
               + jnp.dot(b_ref[...], wb_ref[...], preferred_element_type=F32))
        x1 = x_ref[...] + acc
        x1_ref[...] = x1
        ms = jnp.mean(x1 * x1, axis=-1, keepdims=True)
        hbuf[s] = x1 * lax.rsqrt(ms + EPS) * g_ref[...]

    def emit(s):
        hm = hbuf[s]
        hm_ref[...] = _to_token_rows(hm.astype(BF16))
        lg_ref[...] = lax.dot_general(wr_ref[...], hm.astype(BF16), (((1,), (1,)), ((), ())),
                                      preferred_element_type=F32) + br_ref[...]

    @pl.when(i == 0)
    def _():
        hbuf[1] = jnp.zeros(hbuf.shape[1:], F32)

    @pl.when(i < n_tiles)
    def _():
        emit(1 - slot)
        project(slot)

    @pl.when(i == n_tiles)
    def _():
        emit(1 - slot)


def _out_proj(out_a, out_b, x2, w_out_bf, g, w_rt, b_rt):
    T = x2.shape[0]
    n_tiles = T // OUT_TM

    def cur(i):
        return jnp.minimum(i, n_tiles - 1)

    def prev(i):
        return jnp.maximum(i - 1, 0)

    return pl.pallas_call(
        functools.partial(_out_proj_kernel, n_tiles),
        grid=(n_tiles + 1,),
        in_specs=[
            pl.BlockSpec((OUT_TM, A_WIDTH), lambda i: (cur(i), 0)),
            pl.BlockSpec((OUT_TM, A_WIDTH), lambda i: (cur(i), 0)),
            pl.BlockSpec((OUT_TM, D_MODEL), lambda i: (cur(i), 0)),
            pl.BlockSpec((A_WIDTH, D_MODEL), lambda i: (0, 0)),
            pl.BlockSpec((A_WIDTH, D_MODEL), lambda i: (1, 0)),
            pl.BlockSpec((1, D_MODEL), lambda i: (0, 0)),
            pl.BlockSpec((LG_ROWS, D_MODEL), lambda i: (0, 0)),
            pl.BlockSpec((LG_ROWS, 1), lambda i: (0, 0)),
        ],
        out_specs=[
            pl.BlockSpec((OUT_TM, D_MODEL), lambda i: (cur(i), 0)),
            pl.BlockSpec((OUT_TM, ROW_SUB, LANES), lambda i: (prev(i), 0, 0)),
            pl.BlockSpec((LG_ROWS, OUT_TM), lambda i: (0, prev(i))),
        ],
        out_shape=[
            jax.ShapeDtypeStruct((T, D_MODEL), F32),
            jax.ShapeDtypeStruct((T, ROW_SUB, LANES), ROW_DT),
            jax.ShapeDtypeStruct((LG_ROWS, T), F32),
        ],
        scratch_shapes=[pltpu.VMEM((2, OUT_TM, D_MODEL), F32)],
        compiler_params=_cparams(("arbitrary",)),
        name="out_proj",
    )(out_a, out_b, x2, w_out_bf, w_out_bf, g, w_rt, b_rt)


def _route_kernel(lg_ref, tri_ref, idr_ref, cw_ref, cnt_ref, carry_ref):
    i = pl.program_id(0)

    @pl.when(i == 0)
    def _():
        carry_ref[...] = jnp.zeros_like(carry_ref)

    tn = lg_ref.shape[1]
    row = lax.broadcasted_iota(I32, (SUBLANES, tn), 0)
    neg = jnp.float32(-jnp.inf)

    def top1(v):
        mx = jnp.max(v, axis=0, keepdims=True)
        idx = jnp.min(jnp.where(v == mx, row, SUBLANES), axis=0, keepdims=True)
        return mx, idx

    g = jnp.where(row < N_GROUPS, lg_ref[0:SUBLANES, :], neg)
    gmax, gidx = top1(g)
    g_w = 1.0 / jnp.sum(jnp.exp(g - gmax), axis=0, keepdims=True)
    sel = lg_ref[SUBLANES * N_GROUPS:SUBLANES * (N_GROUPS + 1), :]
    for grp in range(N_GROUPS - 2, -1, -1):
        sel = jnp.where(gidx == grp, lg_ref[SUBLANES * (grp + 1):SUBLANES * (grp + 2), :], sel)
    e1, i1 = top1(sel)
    e2, i2 = top1(jnp.where(row == i1, neg, sel))
    d = jnp.exp(e2 - e1)
    w1 = 1.0 / (1.0 + d)
    w2 = d * w1
    id0 = gidx * EPG + i1
    id1 = gidx * EPG + i2

    erow = lax.broadcasted_iota(I32, (N_EXPERTS, tn), 0)
    o0 = erow == id0
    o1 = erow == id1
    occ = jnp.where(o0 | o1, 1.0, 0.0).astype(F32)
    before = jnp.dot(occ.astype(BF16), tri_ref[...], preferred_element_type=F32) + carry_ref[:, 0:1]
    r0 = jnp.sum(jnp.where(o0, before, 0.0), axis=0, keepdims=True)
    r1 = jnp.sum(jnp.where(o1, before, 0.0), axis=0, keepdims=True)
    total = carry_ref[...] + jnp.sum(occ, axis=1, keepdims=True)
    carry_ref[...] = total
    cnt_ref[...] = total.astype(I32)

    zi = jnp.zeros((SUBLANES - 4, tn), I32)
    idr_ref[...] = jnp.concatenate([id0, id1, r0.astype(I32), r1.astype(I32), zi], axis=0)
    zf = jnp.zeros((SUBLANES - 2, tn), F32)
    cw_ref[...] = jnp.concatenate([g_w * w1, g_w * w2, zf], axis=0)


def _route(lg_t, tri):
    T = lg_t.shape[1]
    return pl.pallas_call(
        _route_kernel,
        grid=(T // RT_TN,),
        in_specs=[
            pl.BlockSpec((LG_ROWS, RT_TN), lambda i: (0, i)),
            pl.BlockSpec((RT_TN, RT_TN), lambda i: (0, 0)),
        ],
        out_specs=[
            pl.BlockSpec((SUBLANES, RT_TN), lambda i: (0, i)),
            pl.BlockSpec((SUBLANES, RT_TN), lambda i: (0, i)),
            pl.BlockSpec((N_EXPERTS, LANES), lambda i: (0, 0)),
        ],
        out_shape=[
            jax.ShapeDtypeStruct((SUBLANES, T), I32),
            jax.ShapeDtypeStruct((SUBLANES, T), F32),
            jax.ShapeDtypeStruct((N_EXPERTS, LANES), I32),
        ],
        scratch_shapes=[pltpu.VMEM((N_EXPERTS, LANES), F32)],
        compiler_params=_cparams(("arbitrary",)),
        name="route",
    )(lg_t, tri)


def _row_copy(src_ref, src_row, dst_ref, dst_row, sem):
    return pltpu.make_async_copy(src_ref.at[src_row], dst_ref.at[dst_row], sem)


def _dispatch_kernel(zt_ref, pos_ref, hm_ref, xs_ref, zbuf, sem):
    i = pl.program_id(0)

    def zero_copy(e):
        return pltpu.make_async_copy(
            zbuf, xs_ref.at[pl.ds(zt_ref[e], EXP_BLK)], sem)

    @pl.when(i == 0)
    def _():
        zbuf[...] = jnp.zeros_like(zbuf)
        for e in range(2 * N_EXPERTS):
            @pl.when(zt_ref[e] >= 0)
            def _():
                zero_copy(e).start()
        for e in range(2 * N_EXPERTS):
            @pl.when(zt_ref[e] >= 0)
            def _():
                zero_copy(e).wait()

    def issue(grp, carry):
        for u in range(DMA_UNROLL):
            t = grp * DMA_UNROLL + u
            for k in range(TOP_K):
                _row_copy(hm_ref, t, xs_ref, pos_ref[k, t], sem).start(priority=k % 2)
        return carry

    lax.fori_loop(0, DSP_TC // DMA_UNROLL, issue, 0)

    def drain(grp, carry):
        for _ in range(DMA_UNROLL * TOP_K):
            _row_copy(hm_ref, 0, xs_ref, 0, sem).wait()
        return carry

    lax.fori_loop(0, DSP_TC // DMA_UNROLL, drain, 0)


def _dispatch(zt, pos, hm3, n_rows):
    T = pos.shape[1]
    grid_spec = pltpu.PrefetchScalarGridSpec(
        num_scalar_prefetch=1,
        grid=(T // DSP_TC,),
        in_specs=[
            pl.BlockSpec((TOP_K, DSP_TC), lambda i, zt: (0, i), memory_space=pltpu.SMEM),
            pl.BlockSpec((DSP_TC, ROW_SUB, LANES), lambda i, zt: (i, 0, 0)),
        ],
        out_specs=pl.BlockSpec(memory_space=pl.ANY),
        scratch_shapes=[pltpu.VMEM((EXP_BLK, ROW_SUB, LANES), ROW_DT), pltpu.SemaphoreType.DMA(())],
    )
    return pl.pallas_call(
        _dispatch_kernel,
        grid_spec=grid_spec,
        out_shape=jax.ShapeDtypeStruct((n_rows, ROW_SUB, LANES), ROW_DT),
        compiler_params=_cparams(("arbitrary",)),
        name="dispatch",
    )(zt, pos, hm3)


def _expert_kernel(be_ref, first_ref, slot_ref, nxt_ref, nu_ref, xs_ref, wg_hbm, wu_hbm, wd_hbm,
                   y_ref, xb_ref, wg_buf, wu_buf, wd_buf, sems):
    b = pl.program_id(0)

    def weight_copies(e, s):
        return [pltpu.make_async_copy(hbm.at[e], buf.at[s], sems.at[s])
                for hbm, buf in ((wg_hbm, wg_buf), (wu_hbm, wu_buf), (wd_hbm, wd_buf))]

    @pl.when(b < nu_ref[0])
    def _():
        s = slot_ref[b]

        @pl.when(b == 0)
        def _():
            for cp in weight_copies(be_ref[0], 0):
                cp.start()

        @pl.when(first_ref[b] == 1)
        def _():
            for cp in weight_copies(be_ref[b], s):
                cp.wait()

            @pl.when(nxt_ref[b] >= 0)
            def _():
                for cp in weight_copies(nxt_ref[b], 1 - s):
                    cp.start()

        slabs = _from_token_rows(xs_ref[...])
        for k in range(ROW_SUB):
            xb_ref[:, k * LANES:(k + 1) * LANES] = slabs[k]
        x = xb_ref[...]
        g = jnp.dot(x, wg_buf[s], preferred_element_type=F32)
        u = jnp.dot(x, wu_buf[s], preferred_element_type=F32)
        a = (jax.nn.silu(g) * u).astype(BF16)
        y = jnp.dot(a, wd_buf[s], preferred_element_type=F32)
        y_ref[...] = _to_token_rows(y.astype(ROW_DT))

    @pl.when(b >= nu_ref[0])
    def _():
        y_ref[...] = jnp.zeros_like(y_ref)


def _experts(block_e, first, slot, nxt, n_used, xs3, wg, wu, wd):
    n_blocks = xs3.shape[0] // EXP_BLK

    def xs_map(b, be, first, slot, nxt, nu):
        return (jnp.minimum(b, nu[0] - 1), 0, 0)

    grid_spec = pltpu.PrefetchScalarGridSpec(
        num_scalar_prefetch=5,
        grid=(n_blocks,),
        in_specs=[
            pl.BlockSpec((EXP_BLK, ROW_SUB, LANES), xs_map),
            pl.BlockSpec(memory_space=pl.ANY),
            pl.BlockSpec(memory_space=pl.ANY),
            pl.BlockSpec(memory_space=pl.ANY),
        ],
        out_specs=pl.BlockSpec((EXP_BLK, ROW_SUB, LANES), lambda b, *_: (b, 0, 0)),
        scratch_shapes=[
            pltpu.VMEM((EXP_BLK, D_MODEL), BF16),
            pltpu.VMEM((2, D_MODEL, D_EXPERT), BF16),
            pltpu.VMEM((2, D_MODEL, D_EXPERT), BF16),
            pltpu.VMEM((2, D_EXPERT, D_MODEL), BF16),
            pltpu.SemaphoreType.DMA((2,)),
        ],
    )
    return pl.pallas_call(
        _expert_kernel,
        grid_spec=grid_spec,
        out_shape=jax.ShapeDtypeStruct(xs3.shape, ROW_DT),
        compiler_params=_cparams(("arbitrary",)),
        name="experts",
    )(block_e, first, slot, nxt, n_used, xs3, wg, wu, wd)


def _combine_kernel(pos_ref, posn_ref, x1_ref, cw_ref, y_ref, o_ref, yb, sems):
    i = pl.program_id(0)
    slot = i % 2

    def gather(ids_ref, slot_):
        def issue(grp, carry):
            for u in range(DMA_UNROLL):
                t = grp * DMA_UNROLL + u
                for k in range(TOP_K):
                    _row_copy(y_ref, ids_ref[k, t], yb.at[slot_, k], t, sems.at[slot_]).start(priority=k % 2)
            return carry

        lax.fori_loop(0, CMB_TC // DMA_UNROLL, issue, 0)

    def gather_inline(ids_ref, slot_):
        for t in range(CMB_TC):
            for k in range(TOP_K):
                _row_copy(y_ref, ids_ref[k, t], yb.at[slot_, k], t, sems.at[slot_]).start(priority=k % 2)

    def weighted_sum():
        w_t = cw_ref[...].T
        w0 = w_t[:, 0:1]
        w1 = w_t[:, 1:2]
        y0 = _from_token_rows(yb[slot, 0])
        y1 = _from_token_rows(yb[slot, 1])
        for s in range(ROW_SUB):
            sl = slice(s * LANES, (s + 1) * LANES)
            o_ref[:, sl] = x1_ref[:, sl] + w0 * y0[s].astype(F32) + w1 * y1[s].astype(F32)

    @pl.when(i == 0)
    def _():
        gather(pos_ref, 0)

    def drain(grp, carry):
        for _ in range(DMA_UNROLL * TOP_K):
            _row_copy(y_ref, 0, yb.at[slot, 0], 0, sems.at[slot]).wait()
        return carry

    lax.fori_loop(0, CMB_TC // DMA_UNROLL, drain, 0)

    @pl.when(i + 1 < pl.num_programs(0))
    def _():
        gather_inline(posn_ref, 1 - slot)
        weighted_sum()

    @pl.when(i + 1 == pl.num_programs(0))
    def _():
        weighted_sum()


def _combine(pos, x1, cw, y3):
    T = x1.shape[0]
    steps = T // CMB_TC
    return pl.pallas_call(
        _combine_kernel,
        grid=(steps,),
        in_specs=[
            pl.BlockSpec((TOP_K, CMB_TC), lambda i: (0, i), memory_space=pltpu.SMEM),
            pl.BlockSpec((TOP_K, CMB_TC), lambda i: (0, jnp.minimum(i + 1, steps - 1)),
                         memory_space=pltpu.SMEM),
            pl.BlockSpec((CMB_TC, D_MODEL), lambda i: (i, 0)),
            pl.BlockSpec((SUBLANES, CMB_TC), lambda i: (0, i)),
            pl.BlockSpec(memory_space=pl.ANY),
        ],
        out_specs=pl.BlockSpec((CMB_TC, D_MODEL), lambda i: (i, 0)),
        out_shape=jax.ShapeDtypeStruct((T, D_MODEL), F32),
        scratch_shapes=[
            pltpu.VMEM((2, TOP_K, CMB_TC, ROW_SUB, LANES), ROW_DT),
            pltpu.SemaphoreType.DMA((2,)),
        ],
        compiler_params=_cparams(("arbitrary",)),
        name="combine",
    )(pos, pos, x1, cw, y3)


def _rotary_lane_tables(seq):
    pos = jnp.arange(seq, dtype=F32)
    inv_freq = 1.0 / (jnp.float32(ROPE_THETA) ** (jnp.arange(0, ROT_DIM, 2, dtype=F32) / ROT_DIM))
    ang = pos[:, None] * inv_freq[None, :]
    cos, sin = jnp.cos(ang), jnp.sin(ang)
    half = ROT_DIM // 2
    lane = jnp.arange(LANES) % QK_DIM
    fidx = lane % half
    first = (lane < half)[None, :]
    second = ((lane >= half) & (lane < ROT_DIM))[None, :]
    cos_t = jnp.where(first | second, cos[:, fidx], 1.0)
    s_up = jnp.where(first, -sin[:, fidx], 0.0)
    s_dn = jnp.where(second, sin[:, fidx], 0.0)
    return cos_t, s_up, s_dn


def kernel(x, attn_norm_g, w_in, gmlp_ln_g, gmlp_ln_b, gmlp_ws, gmlp_bs, q_norm_g, k_norm_g,
           lambda_q1, lambda_k1, lambda_q2, lambda_k2, subln_g, w_out, ffn_norm_g,
           w_group, b_group, w_router, b_router, w_gate, w_up, w_down):
    bsz, seq, d = x.shape
    T = bsz * seq
    assert d == D_MODEL and seq % IN_TM == 0 and seq % ATT_TQ == 0 and T % RT_TN == 0
    l = 0
    x2 = x.reshape(T, d)

    params = jnp.zeros((8, IN_WIDTH), F32)
    params = params.at[0, A_WIDTH:2 * A_WIDTH].set(gmlp_ln_g[l].reshape(-1))
    params = params.at[1, A_WIDTH:2 * A_WIDTH].set(gmlp_ln_b[l].reshape(-1))
    params = params.at[0, 2 * A_WIDTH:3 * A_WIDTH].set(jnp.tile(q_norm_g[l], 2 * B_HEADS))
    params = params.at[0, 3 * A_WIDTH:4 * A_WIDTH].set(jnp.tile(k_norm_g[l], 2 * B_HEADS))
    blk = jnp.arange(IN_GRP) // QK_DIM
    bd = jnp.where(blk[:, None] == blk[None, :], 1.0 / QK_DIM, 0.0).astype(BF16)
    cos_t, s_up, s_dn = _rotary_lane_tables(seq)
    bs_full = jnp.repeat(jnp.transpose(gmlp_bs[l]), HEAD, axis=1)
    lam = (jnp.exp(jnp.sum(lambda_q1[l] * lambda_k1[l])) - jnp.exp(jnp.sum(lambda_q2[l] * lambda_k2[l]))
           + LAMBDA_INIT).reshape(1).astype(F32)
    w_rt = jnp.zeros((LG_ROWS, D_MODEL), F32)
    w_rt = w_rt.at[:N_GROUPS].set(w_group[l].T).at[SUBLANES:].set(w_router[l].T)
    b_rt = jnp.zeros((LG_ROWS, 1), F32)
    b_rt = b_rt.at[:N_GROUPS, 0].set(b_group[l]).at[SUBLANES:, 0].set(b_router[l])
    tri = (jnp.arange(RT_TN)[:, None] < jnp.arange(RT_TN)[None, :]).astype(BF16)

    zp, vb = _in_proj(x2, attn_norm_g[l].reshape(1, d), w_in[l].astype(BF16), params, bd,
                      cos_t, s_up, s_dn, seq)
    out_a = _gmlp(zp, gmlp_ws[l].astype(BF16), bs_full)
    att_steps = bsz * B_HEADS * (seq // ATT_TQ)
    assert (N_EXPERTS * D_MODEL) % att_steps == 0 and (N_EXPERTS * D_EXPERT) % att_steps == 0
    out_b, wg_bf, wu_bf, wd_bf = _attention(
        lam, zp.reshape(bsz, seq, IN_WIDTH - IN_TN), vb.reshape(bsz, seq, IN_TN), subln_g[l].reshape(1, HEAD),
        w_gate[l].reshape(N_EXPERTS * D_MODEL, D_EXPERT), w_up[l].reshape(N_EXPERTS * D_MODEL, D_EXPERT),
        w_down[l].reshape(N_EXPERTS * D_EXPERT, D_MODEL))
    x1, hm3, lg_t = _out_proj(out_a, out_b.reshape(T, A_WIDTH), x2, w_out[l].astype(BF16),
                              ffn_norm_g[l].reshape(1, d), w_rt.astype(BF16), b_rt)

    idr, cw, cnt = _route(lg_t, tri)
    counts = cnt[:, 0]
    padded = (counts + EXP_BLK - 1) // EXP_BLK * EXP_BLK
    pend = jnp.cumsum(padded)
    pstart = (pend - padded).astype(I32)
    n_blocks = (T * TOP_K) // EXP_BLK + N_EXPERTS
    n_rows = n_blocks * EXP_BLK
    block_e = jnp.minimum(jnp.sum(pend[None, :] <= (jnp.arange(n_blocks) * EXP_BLK)[:, None], axis=1),
                          N_EXPERTS - 1).astype(I32)
    n_used = (pend[-1] // EXP_BLK).reshape(1).astype(I32)
    eids = jnp.arange(N_EXPERTS, dtype=I32)
    used = jnp.arange(n_blocks) < n_used[0]
    first = ((block_e != jnp.concatenate([jnp.full((1,), -1, I32), block_e[:-1]])) & used).astype(I32)
    slot = ((jnp.cumsum(first) - 1) % 2).astype(I32)
    later = (eids[None, :] > eids[:, None]) & (counts > 0)[None, :]
    nxt_of = jnp.min(jnp.where(later, eids[None, :], N_EXPERTS), axis=1)
    nxt_of = jnp.where(nxt_of == N_EXPERTS, -1, nxt_of)
    nxt = jnp.sum(jnp.where(block_e[:, None] == eids[None, :], nxt_of[None, :], 0), axis=1).astype(I32)
    trail =pend[-1] + jnp.arange(N_EXPERTS) * EXP_BLK
    zt = jnp.concatenate([jnp.where(counts > 0, pend - EXP_BLK, -1),
                          jnp.where(trail < n_rows, trail, -1)]).astype(I32)

    is_e = idr[0:TOP_K, None, :] == jnp.arange(N_EXPERTS, dtype=I32)[None, :, None]
    pos = jnp.sum(jnp.where(is_e, pstart[None, :, None], 0), axis=1) + idr[TOP_K:2 * TOP_K]
    xs3 = _dispatch(zt, pos, hm3, n_rows)
    y3 = _experts(block_e, first, slot, nxt, n_used, xs3, wg_bf.reshape(N_EXPERTS, D_MODEL, D_EXPERT),
                  wu_bf.reshape(N_EXPERTS, D_MODEL, D_EXPERT), wd_bf.reshape(N_EXPERTS, D_EXPERT, D_MODEL))
    out = _combine(pos, x1, cw, y3)
    return out.reshape(bsz, seq, d)
```
